```python
import math
import jax, jax.numpy as jnp
from jax import lax
import numpy as np

D_MODEL = 1024
BATCH = 2
SEQ = 8192
DEPTH = 2

N_A_LAYERS = DEPTH // 2
N_B_LAYERS = DEPTH - N_A_LAYERS
LRU_WIDTH = 1280
LRU_BLOCK = 256
LRU_HEADS = LRU_WIDTH // LRU_BLOCK
CONV_WIDTH = 4
LRU_C = 8.0
N_HEADS = 8
N_KV_HEADS = 4
HEAD_DIM = 128
KV_GROUP = N_HEADS // N_KV_HEADS
MOBA_BLOCK = 256
MOBA_TOPK = 3
Q_CHUNK = 16
REL_BUCKETS = 32
REL_MAX_DIST = 128
D_FF = -(-8 * D_MODEL // (3 * 256)) * 256
RMS_EPS = 1e-6
NEG_INF = -1e30

kernel_name = "yoco_rglru_moba_adaln_trunk"


def rmsnorm(x, g):
    xf = x.astype(jnp.float32)
    y = xf * lax.rsqrt(jnp.mean(xf * xf, axis=-1, keepdims=True) + RMS_EPS)
    return (y * g.astype(jnp.float32)).astype(x.dtype)


def modulate(h, shift, scale):
    return h * (1 + scale[:, None, :]) + shift[:, None, :]


def swiglu(h, w_gate, w_up, w_down):
    return (jax.nn.silu(h @ w_gate) * (h @ w_up)) @ w_down


def _lru_combine(left, right):
    a_l, b_l = left
    a_r, b_r = right
    return a_l * a_r, a_r * b_l + b_r


def rglru_mixer(h, w_in, conv_w, conv_b, w_gates, b_gates, lru_lambda, w_out):
    B, S, _ = h.shape
    u = h @ w_in
    y_br = jax.nn.gelu(u[..., :LRU_WIDTH])
    x_br = u[..., LRU_WIDTH:]
    xp = jnp.pad(x_br, ((0, 0), (CONV_WIDTH - 1, 0), (0, 0)))
    xc = sum(xp[:, k:k + S, :] * conv_w[k] for k in range(CONV_WIDTH)) + conv_b
    xb = xc.reshape(B, S, LRU_HEADS, LRU_BLOCK)
    g = jnp.einsum('bshi,hio->bsho', xb, w_gates)
    gr = g[..., :LRU_BLOCK].reshape(B, S, LRU_WIDTH) + b_gates[0]
    gi = g[..., LRU_BLOCK:].reshape(B, S, LRU_WIDTH) + b_gates[1]
    r = jax.nn.sigmoid(gr.astype(jnp.float32))
    i = jax.nn.sigmoid(gi.astype(jnp.float32))
    log_a = -LRU_C * r * jax.nn.softplus(-lru_lambda.astype(jnp.float32))
    a = jnp.exp(log_a)
    mult = jnp.sqrt(-jnp.expm1(2.0 * log_a))
    b = mult * (i * xc.astype(jnp.float32))
    _, hs = lax.associative_scan(_lru_combine, (a, b), axis=1)
    return (hs.astype(h.dtype) * y_br) @ w_out


def shared_kv(x, shift, scale, g, w_kv):
    B, S, _ = x.shape
    h = modulate(rmsnorm(x, g), shift, scale)
    kv = h @ w_kv
    nb = -(-S // MOBA_BLOCK)
    pad = nb * MOBA_BLOCK - S
    def blk(t):
        t = t.reshape(B, S, N_KV_HEADS, HEAD_DIM).transpose(0, 2, 1, 3)
        t = jnp.pad(t, ((0, 0), (0, 0), (0, pad), (0, 0)))
        return t.reshape(B, N_KV_HEADS, nb, MOBA_BLOCK, HEAD_DIM)
    k_blocks = blk(kv[..., :N_KV_HEADS * HEAD_DIM])
    v_blocks = blk(kv[..., N_KV_HEADS * HEAD_DIM:])
    k_mean = jnp.mean(k_blocks.astype(jnp.float32), axis=3)
    return k_blocks, v_blocks, k_mean


def t5_bucket(dist):
    dist = jnp.maximum(dist, 0)
    max_exact = REL_BUCKETS // 2
    d = jnp.maximum(dist, 1).astype(jnp.float32)
    large = max_exact + (jnp.log(d / max_exact) / math.log(REL_MAX_DIST / max_exact)
                         * (REL_BUCKETS - max_exact)).astype(jnp.int32)
    large = jnp.minimum(large, REL_BUCKETS - 1)
    return jnp.where(dist < max_exact, dist, large)


def moba_mixer(h, w_q, w_o, k_blocks, v_blocks, k_mean, rel_bias):
    B, S, _ = h.shape
    nb = k_blocks.shape[2]
    s_pad = nb * MOBA_BLOCK
    n_sel = min(MOBA_TOPK, nb)
    scale = HEAD_DIM ** -0.5
    q = (h @ w_q).reshape(B, S, N_HEADS, HEAD_DIM).transpose(0, 2, 1, 3)
    q = jnp.pad(q, ((0, 0), (0, 0), (0, s_pad - S), (0, 0)))
    kv_head = jnp.arange(N_HEADS) // KV_GROUP
    km = k_mean[:, kv_head]
    bi = jnp.arange(B)[:, None, None, None]
    kvh = kv_head[None, :, None, None]
    hi5 = jnp.arange(N_HEADS)[None, :, None, None, None]
    bias_tab = rel_bias.astype(jnp.float32)
    blk_pos = jnp.arange(MOBA_BLOCK)

    def chunk(ci):
        t0 = ci * Q_CHUNK
        qf = lax.dynamic_slice_in_dim(q, t0, Q_CHUNK, axis=2).astype(jnp.float32)
        pos = t0 + jnp.arange(Q_CHUNK)
        own = t0 // MOBA_BLOCK
        gate = jnp.einsum('bhcd,bhnd->bhcn', qf, km)
        gate = jnp.where(jnp.arange(nb) < own, gate, NEG_INF)
        _, idx = lax.top_k(gate, n_sel)
        sel_valid = jnp.arange(n_sel) < own
        ksel = k_blocks[bi, kvh, idx].astype(jnp.float32)
        vsel = v_blocks[bi, kvh, idx].astype(jnp.float32)
        s_sel = jnp.einsum('bhcd,bhcnkd->bhcnk', qf, ksel) * scale
        kpos_sel = idx[..., None] * MOBA_BLOCK + blk_pos
        dist_sel = pos[None, None, :, None, None] - kpos_sel
        s_sel = jnp.where(sel_valid[:, None], s_sel + bias_tab[hi5, t5_bucket(dist_sel)], NEG_INF)
        k_own = lax.dynamic_index_in_dim(k_blocks, own, axis=2, keepdims=False)[:, kv_head].astype(jnp.float32)
        v_own = lax.dynamic_index_in_dim(v_blocks, own, axis=2, keepdims=False)[:, kv_head].astype(jnp.float32)
        s_own = jnp.einsum('bhcd,bhkd->bhck', qf, k_own) * scale
        dist_own = pos[:, None] - (own * MOBA_BLOCK + blk_pos)[None, :]
        s_own = jnp.where(dist_own >= 0, s_own + bias_tab[:, t5_bucket(dist_own)], NEG_INF)
        logits = jnp.concatenate([s_sel.reshape(B, N_HEADS, Q_CHUNK, n_sel * MOBA_BLOCK), s_own], axis=-1)
        p = jax.nn.softmax(logits, axis=-1)
        p_sel = p[..., :n_sel * MOBA_BLOCK].reshape(B, N_HEADS, Q_CHUNK, n_sel, MOBA_BLOCK)
        p_own = p[..., n_sel * MOBA_BLOCK:]
        o = jnp.einsum('bhcnk,bhcnkd->bhcd', p_sel, vsel) + jnp.einsum('bhck,bhkd->bhcd', p_own, v_own)
        return o.astype(h.dtype)

    o = lax.map(chunk, jnp.arange(s_pad // Q_CHUNK))
    o = o.transpose(1, 0, 3, 2, 4).reshape(B, s_pad, N_HEADS * HEAD_DIM)[:, :S]
    return o @ w_o


def setup_inputs(seed: int = 0) -> dict:
    key = jax.random.key(seed)
    ks = jax.random.split(key, 32)
    nrm = lambda k, shape, s: jax.random.normal(k, shape, jnp.float32) * s
    D = D_MODEL
    u = jax.random.uniform(ks[10], (N_A_LAYERS, LRU_WIDTH), jnp.float32, 0.9, 0.999)
    a_base = u ** (1.0 / LRU_C)
    lru_lambda = jnp.log(a_base) - jnp.log1p(-a_base)
    return {
        "x": nrm(ks[0], (BATCH, SEQ, D), 1.0),
        "c": nrm(ks[1], (BATCH, D), 1.0),
        "mod_w": nrm(ks[2], (DEPTH, D, 6 * D), 0.5 * D ** -0.5),
        "mod_b": nrm(ks[3], (DEPTH, 6 * D), 0.02),
        "norm_mix": 1.0 + nrm(ks[4], (DEPTH, D), 0.02),
        "norm_ffn": 1.0 + nrm(ks[5], (DEPTH, D), 0.02),
        "lru_w_in": nrm(ks[6], (N_A_LAYERS, D, 2 * LRU_WIDTH), D ** -0.5),
        "lru_conv_w": nrm(ks[7], (N_A_LAYERS, CONV_WIDTH, LRU_WIDTH), CONV_WIDTH ** -0.5),
        "lru_conv_b": nrm(ks[8], (N_A_LAYERS, LRU_WIDTH), 0.02),
        "lru_w_gates": nrm(ks[9], (N_A_LAYERS, LRU_HEADS, LRU_BLOCK, 2 * LRU_BLOCK), LRU_BLOCK ** -0.5),
        "lru_b_gates": nrm(ks[11], (N_A_LAYERS, 2, LRU_WIDTH), 0.02),
        "lru_lambda": lru_lambda,
        "lru_w_out": nrm(ks[12], (N_A_LAYERS, LRU_WIDTH, D), LRU_WIDTH ** -0.5),
        "kv_mod_w": nrm(ks[13], (D, 2 * D), 0.5 * D ** -0.5),
        "kv_mod_b": nrm(ks[14], (2 * D,), 0.02),
        "kv_norm": 1.0 + nrm(ks[15], (D,), 0.02),
        "w_kv": nrm(ks[16], (D, 2 * N_KV_HEADS * HEAD_DIM), D ** -0.5),
        "attn_w_q": nrm(ks[17], (N_B_LAYERS, D, N_HEADS * HEAD_DIM), D ** -0.5),
        "attn_w_o": nrm(ks[18], (N_B_LAYERS, N_HEADS * HEAD_DIM, D), (N_HEADS * HEAD_DIM) ** -0.5),
        "rel_bias": nrm(ks[19], (N_HEADS, REL_BUCKETS), 0.2),
        "ffn_w_gate": nrm(ks[20], (DEPTH, D, D_FF), D ** -0.5),
        "ffn_w_up": nrm(ks[21], (DEPTH, D, D_FF), D ** -0.5),
        "ffn_w_down": nrm(ks[22], (DEPTH, D_FF, D), D_FF ** -0.5),
        "final_norm": 1.0 + nrm(ks[23], (D,), 0.02),
    }


def reference(x, c, mod_w, mod_b, norm_mix, norm_ffn, lru_w_in, lru_conv_w, lru_conv_b,
              lru_w_gates, lru_b_gates, lru_lambda, lru_w_out, kv_mod_w, kv_mod_b, kv_norm,
              w_kv, attn_w_q, attn_w_o, rel_bias, ffn_w_gate, ffn_w_up, ffn_w_down, final_norm):
    D = D_MODEL
    cs = jax.nn.silu(c)
    kv = None
    for l in range(DEPTH):
        if l == N_A_LAYERS:
            kv_mod = cs @ kv_mod_w + kv_mod_b
            kv = shared_kv(x, kv_mod[:, :D], kv_mod[:, D:], kv_norm, w_kv)
        mod = cs @ mod_w[l] + mod_b[l]
        sh_m, sc_m, g_m = mod[:, :D], mod[:, D:2 * D], mod[:, 2 * D:3 * D]
        sh_f, sc_f, g_f = mod[:, 3 * D:4 * D], mod[:, 4 * D:5 * D], mod[:, 5 * D:]
        h = modulate(rmsnorm(x, norm_mix[l]), sh_m, sc_m)
        if l < N_A_LAYERS:
            mix = rglru_mixer(h, lru_w_in[l], lru_conv_w[l], lru_conv_b[l], lru_w_gates[l],
                              lru_b_gates[l], lru_lambda[l], lru_w_out[l])
        else:
            j = l - N_A_LAYERS
            mix = moba_mixer(h, attn_w_q[j], attn_w_o[j], kv[0], kv[1], kv[2], rel_bias)
        x = x + g_m[:, None, :] * mix
        h = modulate(rmsnorm(x, norm_ffn[l]), sh_f, sc_f)
        x = x + g_f[:, None, :] * swiglu(h, ffn_w_gate[l], ffn_w_up[l], ffn_w_down[l])
    return rmsnorm(x, final_norm)
```

```python
import functools
import math

import numpy as np
import jax
import jax.numpy as jnp
from jax import lax
from jax.experimental import pallas as pl
from jax.experimental.pallas import tpu as pltpu

F32 = jnp.float32
BF16 = jnp.bfloat16
HIGHEST = lax.Precision.HIGHEST

LRU_BLOCK = 256
CONV_WIDTH = 4
LRU_C = 8.0
N_HEADS = 8
N_KV_HEADS = 4
HEAD_DIM = 128
KV_GROUP = N_HEADS // N_KV_HEADS
MOBA_BLOCK = 256
MOBA_TOPK = 3
REL_BUCKETS = 32
REL_MAX_DIST = 128
RMS_EPS = 1e-6
NEG_INF = -1e30

SUBLANES = 8
LANES = 128
VMEM_LIMIT = 56 * 1024 * 1024

TM_LRU = 256
TM_FFN = 512
FF_CHUNK = 256
MOD_TN = 2048


def _cparams(n_axes):
    return pltpu.CompilerParams(dimension_semantics=("arbitrary",) * n_axes,
                                vmem_limit_bytes=VMEM_LIMIT)


def _const_spec(shape):
    nd = len(shape)
    return pl.BlockSpec(shape, lambda *_: (0,) * nd, pipeline_mode=pl.Buffered(1))


def _dot(a, b):
    return jnp.dot(a, b, preferred_element_type=F32)


def _dot_nt(a, b, precision=None):
    return lax.dot_general(a, b, (((1,), (1,)), ((), ())), preferred_element_type=F32,
                           precision=precision)


def _norm_mod(xt, gain, shift, scale):
    inv = lax.rsqrt(jnp.mean(xt * xt, axis=-1, keepdims=True) + RMS_EPS)
    return (xt * inv * gain) * (1.0 + scale) + shift


def _mod_kernel(c_ref, w_ref, b_ref, o_ref):
    c = c_ref[...]
    cs = c * jax.nn.sigmoid(c)
    o_ref[0] = jnp.dot(cs, w_ref[0], preferred_element_type=F32, precision=HIGHEST) + b_ref[0]


def _adaln_mod(c_pad, w, b):
    n_layers, d, n = w.shape
    tn = min(MOD_TN, n)
    return pl.pallas_call(
        _mod_kernel,
        grid=(n_layers, n // tn),
        in_specs=[pl.BlockSpec((SUBLANES, d), lambda l, j: (0, 0)),
                  pl.BlockSpec((1, d, tn), lambda l, j: (l, 0, j)),
                  pl.BlockSpec((1, 1, tn), lambda l, j: (l, 0, j))],
        out_specs=pl.BlockSpec((1, SUBLANES, tn), lambda l, j: (l, 0, j)),
        out_shape=jax.ShapeDtypeStruct((n_layers, SUBLANES, n), F32),
        compiler_params=_cparams(2),
        name="adaln_mod",
    )(c_pad, w, b.reshape(n_layers, 1, n))


def _t5_thresholds():
    max_exact = REL_BUCKETS // 2
    d = np.arange(0, 4 * REL_MAX_DIST)
    dd = np.maximum(d, 1).astype(np.float32)
    large = max_exact + (np.log(dd / max_exact) / math.log(REL_MAX_DIST / max_exact)
                         * (REL_BUCKETS - max_exact)).astype(np.int32)
    large = np.minimum(large, REL_BUCKETS - 1)
    bucket = np.where(d < max_exact, d, large)
    return [int(np.argmax(bucket >= b)) for b in range(REL_BUCKETS)]


def _t5_kernel(rb_ref, o_ref, *, thr):
    h = pl.program_id(0)
    shape = (2 * MOBA_BLOCK, MOBA_BLOCK)
    key = lax.broadcasted_iota(jnp.int32, shape, 0)
    qry = lax.broadcasted_iota(jnp.int32, shape, 1)
    dist = qry + MOBA_BLOCK - key
    val = jnp.full(shape, rb_ref[h, REL_BUCKETS - 1], F32)
    for b in range(REL_BUCKETS - 2, -1, -1):
        val = jnp.where(dist < thr[b + 1], rb_ref[h, b], val)
    o_ref[0] = jnp.where(dist < 0, NEG_INF, val)


def _t5_table(rel_bias):
    return pl.pallas_call(
        functools.partial(_t5_kernel, thr=_t5_thresholds()),
        grid=(N_HEADS,),
        in_specs=[pl.BlockSpec(memory_space=pltpu.SMEM)],
        out_specs=pl.BlockSpec((1, 2 * MOBA_BLOCK, MOBA_BLOCK), lambda h: (h, 0, 0)),
        out_shape=jax.ShapeDtypeStruct((N_HEADS, 2 * MOBA_BLOCK, MOBA_BLOCK), F32),
        compiler_params=_cparams(1),
        name="t5_table",
    )(rel_bias.astype(F32))


def _shift_rows(cur, prev_tail, k):
    rolled = pltpu.roll(cur, k, 0)
    fix = pltpu.roll(prev_tail, k, 0)
    row = lax.broadcasted_iota(jnp.int32, prev_tail.shape, 0)
    top = jnp.where(row < k, fix, rolled[:SUBLANES])
    return jnp.concatenate([top, rolled[SUBLANES:]], axis=0)


def _lru_scan(a, b, h0):
    t, c = a.shape
    row = lax.broadcasted_iota(jnp.int32, a.shape, 0)
    s = 1
    while s < t:
        if s < SUBLANES:
            a_sh = jnp.where(row < s, 1.0, pltpu.roll(a, s, 0))
            b_sh = jnp.where(row < s, 0.0, pltpu.roll(b, s, 0))
        else:
            a_sh = jnp.concatenate([jnp.ones((s, c), F32), a[:t - s]], axis=0)
            b_sh = jnp.concatenate([jnp.zeros((s, c), F32), b[:t - s]], axis=0)
        b = a * b_sh + b
        a = a * a_sh
        s *= 2
    return a * h0 + b


def _rglru_kernel(x_ref, mod_ref, g_ref, win_ref, cw_ref, cb_ref, wg_ref, bg_ref, lam_ref, wout_ref,
                  o_ref, conv_scr, h_scr):
    d = x_ref.shape[-1]
    w = lam_ref.shape[-1]
    tm = x_ref.shape[1]

    @pl.when(pl.program_id(1) == 0)
    def _():
        conv_scr[...] = jnp.zeros_like(conv_scr)
        h_scr[...] = jnp.zeros_like(h_scr)

    xt = x_ref[0]
    mod = mod_ref[0]
    h = _norm_mod(xt, g_ref[...], mod[:, 0:d], mod[:, d:2 * d]).astype(BF16)
    acc = jnp.zeros((tm, d), F32)
    for hd in range(w // LRU_BLOCK):
        lo = hd * LRU_BLOCK
        hi = lo + LRU_BLOCK
        y = jax.nn.gelu(_dot(h, win_ref[:, lo:hi]))
        ux = _dot(h, win_ref[:, w + lo:w + hi])
        prev_tail = conv_scr[:, lo:hi]
        cw = cw_ref[:, lo:hi]
        xc = ux * cw[CONV_WIDTH - 1:CONV_WIDTH]
        for k in range(1, CONV_WIDTH):
            xc = xc + _shift_rows(ux, prev_tail, k) * cw[CONV_WIDTH - 1 - k:CONV_WIDTH - k]
        xc = xc + cb_ref[:, lo:hi]
        conv_scr[:, lo:hi] = ux[tm - SUBLANES:]
        gates = _dot(xc.astype(BF16), wg_ref[hd])
        r = jax.nn.sigmoid(gates[:, :LRU_BLOCK] + bg_ref[0:1, lo:hi])
        i = jax.nn.sigmoid(gates[:, LRU_BLOCK:] + bg_ref[1:2, lo:hi])
        lam = lam_ref[:, lo:hi]
        softplus_neg = jnp.maximum(-lam, 0.0) + jnp.log1p(jnp.exp(-jnp.abs(lam)))
        log_a = (-LRU_C * r) * softplus_neg
        a = jnp.exp(log_a)
        mult = jnp.sqrt((1.0 + a * a) * jnp.tanh(-log_a))
        b = mult * (i * xc)
        hs = _lru_scan(a, b, h_scr[0:1, lo:hi])
        h_scr[0:1, lo:hi] = hs[tm - 1:tm]
        acc = acc + _dot((hs * y).astype(BF16), wout_ref[lo:hi, :])
    o_ref[0] = xt + mod[:, 2 * d:3 * d] * acc


def _rglru_layer(x, mod, gain, w_in, conv_w, conv_b, w_gates, b_gates, lam, w_out):
    b, s, d = x.shape
    w = lam.shape[-1]
    tm = TM_LRU
    return pl.pallas_call(
        _rglru_kernel,
        grid=(b, s // tm),
        in_specs=[pl.BlockSpec((1, tm, d), lambda i, t: (i, t, 0)),
                  pl.BlockSpec((1, 1, mod.shape[-1]), lambda i, t: (i, 0, 0)),
                  _const_spec((1, d)),
                  _const_spec(w_in.shape),
                  _const_spec(conv_w.shape),
                  _const_spec((1, w)),
                  _const_spec(w_gates.shape),
                  _const_spec(b_gates.shape),
                  _const_spec((1, w)),
                  _const_spec(w_out.shape)],
        out_specs=pl.BlockSpec((1, tm, d), lambda i, t: (i, t, 0)),
        out_shape=jax.ShapeDtypeStruct(x.shape, F32),
        scratch_shapes=[pltpu.VMEM((SUBLANES, w), F32), pltpu.VMEM((SUBLANES, w), F32)],
        compiler_params=_cparams(2),
        name="rglru",
    )(x, mod, gain.reshape(1, d), w_in, conv_w, conv_b.reshape(1, w), w_gates, b_gates,
      lam.reshape(1, w), w_out)


def _ffn_kernel(*refs, has_attn, final):
    x_ref, mod_ref, g_ref, wg_ref, wu_ref, wd_ref = refs[:6]
    rest = list(refs[6:])
    att_ref = wo_ref = fn_ref = None
    if has_attn:
        att_ref, wo_ref = rest[:2]
        rest = rest[2:]
    if final:
        fn_ref = rest[0]
        rest = rest[1:]
    o_ref, = rest
    d = x_ref.shape[-1]
    ff = wg_ref.shape[-1]
    tm = x_ref.shape[1]

    xt = x_ref[0]
    mod = mod_ref[0]
    if has_attn:
        xt = xt + mod[:, 2 * d:3 * d] * _dot(att_ref[0], wo_ref[...])
    h = _norm_mod(xt, g_ref[...], mod[:, 3 * d:4 * d], mod[:, 4 * d:5 * d]).astype(BF16)
    acc = jnp.zeros((tm, d), F32)
    for c in range(ff // FF_CHUNK):
        lo = c * FF_CHUNK
        hi = lo + FF_CHUNK
        gate = _dot(h, wg_ref[:, lo:hi])
        up = _dot(h, wu_ref[:, lo:hi])
        act = (gate * jax.nn.sigmoid(gate)) * up
        acc = acc + _dot(act.astype(BF16), wd_ref[lo:hi, :])
    y = xt + mod[:, 5 * d:6 * d] * acc
    if final:
        y = y * lax.rsqrt(jnp.mean(y * y, axis=-1, keepdims=True) + RMS_EPS) * fn_ref[...]
    o_ref[0] = y


def _ffn_layer(x, mod, gain, w_gate, w_up, w_down, attn=None, w_o=None, final_gain=None):
    b, s, d = x.shape
    tm = TM_FFN
    has_attn = attn is not None
    final = final_gain is not None
    in_specs = [pl.BlockSpec((1, tm, d), lambda i, t: (i, t, 0)),
                pl.BlockSpec((1, 1, mod.shape[-1]), lambda i, t: (i, 0, 0)),
                _const_spec((1, d)),
                _const_spec(w_gate.shape),
                _const_spec(w_up.shape),
                _const_spec(w_down.shape)]
    args = [x, mod, gain.reshape(1, d), w_gate, w_up, w_down]
    if has_attn:
        in_specs += [pl.BlockSpec((1, tm, d), lambda i, t: (i, t, 0)), _const_spec(w_o.shape)]
        args += [attn, w_o]
    if final:
        in_specs += [_const_spec((1, d))]
        args += [final_gain.reshape(1, d)]
    return pl.pallas_call(
        functools.partial(_ffn_kernel, has_attn=has_attn, final=final),
        grid=(b, s // tm),
        in_specs=in_specs,
        out_specs=pl.BlockSpec((1, tm, d), lambda i, t: (i, t, 0)),
        out_shape=jax.ShapeDtypeStruct(x.shape, F32),
        compiler_params=_cparams(2),
        name="ffn_attn_final" if has_attn else "ffn",
    )(*args)


def _qkv_kernel(x_ref, kvmod_ref, mod_ref, kvg_ref, qg_ref, wk_ref, wvt_ref, wq_ref,
                q_ref, k_ref, vt_ref, mask_ref, km_scr):
    d = x_ref.shape[-1]
    tm = x_ref.shape[1]
    nb = mask_ref.shape[3]
    j = pl.program_id(1)

    @pl.when(j == 0)
    def _():
        km_scr[...] = jnp.zeros_like(km_scr)

    xt = x_ref[0]
    inv = lax.rsqrt(jnp.mean(xt * xt, axis=-1, keepdims=True) + RMS_EPS)
    xn = xt * inv
    kvmod = kvmod_ref[0]
    mod = mod_ref[0]
    h_kv = ((xn * kvg_ref[...]) * (1.0 + kvmod[:, d:2 * d]) + kvmod[:, 0:d]).astype(BF16)
    h_q = ((xn * qg_ref[...]) * (1.0 + mod[:, d:2 * d]) + mod[:, 0:d]).astype(BF16)
    k = _dot(h_kv, wk_ref[...])
    vt = _dot_nt(wvt_ref[...], h_kv)
    q = _dot(h_q, wq_ref[...]) * (HEAD_DIM ** -0.5)
    q_ref[0] = q.astype(BF16)
    for hk in range(N_KV_HEADS):
        lo = hk * HEAD_DIM
        k_ref[0, hk, 0] = k[:, lo:lo + HEAD_DIM].astype(BF16)
        vt_ref[0, hk, 0] = vt[lo:lo + HEAD_DIM, :].astype(BF16)

    blk = lax.broadcasted_iota(jnp.int32, (nb, tm), 0).astype(F32)
    past = blk < j.astype(F32)
    for hq in range(N_HEADS):
        hk = hq // KV_GROUP
        km_h = km_scr[:, hk * HEAD_DIM:(hk + 1) * HEAD_DIM]
        gate = _dot_nt(km_h, q[:, hq * HEAD_DIM:(hq + 1) * HEAD_DIM], precision=HIGHEST)
        gate = jnp.where(past, gate, NEG_INF)
        mask = jnp.full((nb, tm), NEG_INF, F32)
        for _ in range(MOBA_TOPK):
            best = jnp.max(gate, axis=0, keepdims=True)
            first = jnp.min(jnp.where(gate == best, blk, nb), axis=0, keepdims=True)
            hit = blk == first
            mask = jnp.where(hit, 0.0, mask)
            gate = jnp.where(hit, -jnp.inf, gate)
        mask = jnp.where(past, mask, NEG_INF)
        lo = (hq % KV_GROUP) * tm
        mask_ref[0, hk, 0, :, lo:lo + tm] = mask

    km_scr[pl.ds(j, 1), :] = jnp.mean(k, axis=0, keepdims=True)


def _qkv_gate(x, kvmod, mod, kv_gain, q_gain, w_k, w_vt, w_q):
    b, s, d = x.shape
    tm = MOBA_BLOCK
    nb = s // tm
    dkv = N_KV_HEADS * HEAD_DIM
    dq = N_HEADS * HEAD_DIM
    out_shape = (jax.ShapeDtypeStruct((b, s, dq), BF16),
                 jax.ShapeDtypeStruct((b, N_KV_HEADS, nb, tm, HEAD_DIM), BF16),
                 jax.ShapeDtypeStruct((b, N_KV_HEADS, nb, HEAD_DIM, tm), BF16),
                 jax.ShapeDtypeStruct((b, N_KV_HEADS, nb, nb, KV_GROUP * tm), F32))
    out_specs = (pl.BlockSpec((1, tm, dq), lambda i, j: (i, j, 0)),
                 pl.BlockSpec((1, N_KV_HEADS, 1, tm, HEAD_DIM), lambda i, j: (i, 0, j, 0, 0)),
                 pl.BlockSpec((1, N_KV_HEADS, 1, HEAD_DIM, tm), lambda i, j: (i, 0, j, 0, 0)),
                 pl.BlockSpec((1, N_KV_HEADS, 1, nb, KV_GROUP * tm), lambda i, j: (i, 0, j, 0, 0)))
    return pl.pallas_call(
        _qkv_kernel,
        grid=(b, nb),
        in_specs=[pl.BlockSpec((1, tm, d), lambda i, j: (i, j, 0)),
                  pl.BlockSpec((1, 1, kvmod.shape[-1]), lambda i, j: (i, 0, 0)),
                  pl.BlockSpec((1, 1, mod.shape[-1]), lambda i, j: (i, 0, 0)),
                  _const_spec((1, d)),
                  _const_spec((1, d)),
                  _const_spec(w_k.shape),
                  _const_spec(w_vt.shape),
                  _const_spec(w_q.shape)],
        out_specs=out_specs,
        out_shape=out_shape,
        scratch_shapes=[pltpu.VMEM((nb, dkv), F32)],
        compiler_params=_cparams(2),
        name="qkv_gate",
    )(x, kvmod, mod, kv_gain.reshape(1, d), q_gain.reshape(1, d), w_k, w_vt, w_q)


def _attn_kernel(rb_ref, q_ref, k_ref, vt_ref, mask_ref, tab_ref, o_ref, m_scr, l_scr, acc_scr):
    g = pl.program_id(1)
    j = pl.program_id(2)
    tq = q_ref.shape[1]
    q2 = q_ref[0]
    qs = jnp.concatenate([q2[:, :HEAD_DIM], q2[:, HEAD_DIM:]], axis=0)

    def scores(n):
        return _dot_nt(k_ref[0, 0, n], qs)

    def tile_bias(lo):
        return jnp.concatenate([tab_ref[0, lo:lo + MOBA_BLOCK, :], tab_ref[1, lo:lo + MOBA_BLOCK, :]], axis=1)

    def update(s, n):
        m_old = m_scr[0:1]
        m_new = jnp.maximum(m_old, jnp.max(s, axis=0, keepdims=True))
        alpha = jnp.exp(m_old - m_new)
        p = jnp.exp(s - m_new)
        l_scr[0:1] = alpha * l_scr[0:1] + jnp.sum(p, axis=0, keepdims=True)
        acc_scr[...] = alpha * acc_scr[...] + _dot(vt_ref[0, 0, n], p.astype(BF16))
        m_scr[0:1] = m_new

    s = scores(j) + tile_bias(MOBA_BLOCK)
    m0 = jnp.max(s, axis=0, keepdims=True)
    p = jnp.exp(s - m0)
    m_scr[0:1] = m0
    l_scr[0:1] = jnp.sum(p, axis=0, keepdims=True)
    acc_scr[...] = _dot(vt_ref[0, 0, j], p.astype(BF16))

    @pl.when(j >= 1)
    def _():
        n = j - 1
        update(scores(n) + tile_bias(0) + mask_ref[0, 0, 0, pl.ds(n, 1), :], n)

    lane = lax.broadcasted_iota(jnp.int32, (1, KV_GROUP * tq), 1)
    far_bias = jnp.where(lane < tq, rb_ref[KV_GROUP * g, REL_BUCKETS - 1],
                         rb_ref[KV_GROUP * g + 1, REL_BUCKETS - 1])

    def far(n, carry):
        update(scores(n) + (mask_ref[0, 0, 0, pl.ds(n, 1), :] + far_bias), n)
        return carry

    lax.fori_loop(0, jnp.maximum(j - 1, 0), far, 0)

    o = acc_scr[...] * (1.0 / l_scr[0:1])
    o_ref[0] = jnp.concatenate([o[:, :tq].T, o[:, tq:].T], axis=1).astype(BF16)


def _moba_attn(rel_bias, q, k, vt, mask, table):
    b, s, dq = q.shape
    nb = k.shape[2]
    tq = MOBA_BLOCK
    gw = KV_GROUP * HEAD_DIM
    return pl.pallas_call(
        _attn_kernel,
        grid=(b, N_KV_HEADS, nb),
        in_specs=[pl.BlockSpec(memory_space=pltpu.SMEM),
                  pl.BlockSpec((1, tq, gw), lambda i, g, j: (i, j, g)),
                  pl.BlockSpec((1, 1, nb, tq, HEAD_DIM), lambda i, g, j: (i, g, 0, 0, 0)),
                  pl.BlockSpec((1, 1, nb, HEAD_DIM, tq), lambda i, g, j: (i, g, 0, 0, 0)),
                  pl.BlockSpec((1, 1, 1, nb, KV_GROUP * tq), lambda i, g, j: (i, g, j, 0, 0)),
                  pl.BlockSpec((KV_GROUP, 2 * MOBA_BLOCK, MOBA_BLOCK), lambda i, g, j: (g, 0, 0))],
        out_specs=pl.BlockSpec((1, tq, gw), lambda i, g, j: (i, j, g)),
        out_shape=jax.ShapeDtypeStruct((b, s, dq), BF16),
        scratch_shapes=[pltpu.VMEM((SUBLANES, KV_GROUP * tq), F32),
                        pltpu.VMEM((SUBLANES, KV_GROUP * tq), F32),
                        pltpu.VMEM((HEAD_DIM, KV_GROUP * tq), F32)],
        compiler_params=_cparams(3),
        name="moba_attn",
    )(rel_bias.astype(F32), q, k, vt, mask, table)


def kernel(x, c, mod_w, mod_b, norm_mix, norm_ffn, lru_w_in, lru_conv_w, lru_conv_b, lru_w_gates,
           lru_b_gates, lru_lambda, lru_w_out, kv_mod_w, kv_mod_b, kv_norm, w_kv, attn_w_q, attn_w_o,
           rel_bias, ffn_w_gate, ffn_w_up, ffn_w_down, final_norm):
    b, s, d = x.shape
    assert s % TM_FFN == 0 and s % MOBA_BLOCK == 0 and b <= SUBLANES
    dkv = N_KV_HEADS * HEAD_DIM

    c_pad = jnp.zeros((SUBLANES, d), F32).at[:b].set(c.astype(F32))
    mod = _adaln_mod(c_pad, mod_w, mod_b)[:, :b]
    kvmod = _adaln_mod(c_pad, kv_mod_w[None], kv_mod_b[None])[0, :b]
    mod0 = mod[0][:, None, :]
    mod1 = mod[1][:, None, :]
    kvmod = kvmod[:, None, :]

    x = _rglru_layer(x, mod0, norm_mix[0], lru_w_in[0].astype(BF16), lru_conv_w[0], lru_conv_b[0],
                     lru_w_gates[0].astype(BF16), lru_b_gates[0], lru_lambda[0], lru_w_out[0].astype(BF16))
    x = _ffn_layer(x, mod0, norm_ffn[0], ffn_w_gate[0].astype(BF16), ffn_w_up[0].astype(BF16),
                   ffn_w_down[0].astype(BF16))

    q, k, vt, mask = _qkv_gate(x, kvmod, mod1, kv_norm, norm_mix[1], w_kv[:, :dkv].astype(BF16),
                               w_kv[:, dkv:].T.astype(BF16), attn_w_q[0].astype(BF16))
    table = _t5_table(rel_bias)
    attn = _moba_attn(rel_bias, q, k, vt, mask, table)
    return _ffn_layer(x, mod1, norm_ffn[1], ffn_w_gate[1].astype(BF16), ffn_w_up[1].astype(BF16),
                      ffn_w_down[1].astype(BF16), attn=attn, w_o=attn_w_o[0].astype(BF16),
                      final_gain=final_norm)
```

```python
import functools
import math

import numpy as np
import jax
import jax.numpy as jnp
from jax import lax
from jax.experimental import pallas as pl
from jax.experimental.pallas import tpu as pltpu

F32 = jnp.float32
BF16 = jnp.bfloat16
HIGHEST = lax.Precision.HIGHEST

LRU_BLOCK = 256
CONV_WIDTH = 4
LRU_C = 8.0
N_HEADS = 8
N_KV_HEADS = 4
HEAD_DIM = 128
KV_GROUP = N_HEADS // N_KV_HEADS
MOBA_BLOCK = 256
MOBA_TOPK = 3
REL_BUCKETS = 32
REL_MAX_DIST = 128
RMS_EPS = 1e-6
NEG_INF = -1e30
LOG2E = math.log2(math.e)

SUBLANES = 8
LANES = 128
VMEM_LIMIT = 56 * 1024 * 1024

TM_LRU = 256
TM_FFN = 512
FF_CHUNK = 256
MOD_TN = 2048
FAR_GROUP = 4


def _cparams(n_axes):
    return pltpu.CompilerParams(dimension_semantics=("arbitrary",) * n_axes,
                                vmem_limit_bytes=VMEM_LIMIT)


def _const_spec(shape):
    nd = len(shape)
    return pl.BlockSpec(shape, lambda *_: (0,) * nd, pipeline_mode=pl.Buffered(1))


def _dot(a, b):
    return jnp.dot(a, b, preferred_element_type=F32)


def _dot_nt(a, b, precision=None):
    return lax.dot_general(a, b, (((1,), (1,)), ((), ())), preferred_element_type=F32,
                           precision=precision)


def _norm_mod(xt, gain, shift, scale):
    inv = lax.rsqrt(jnp.mean(xt * xt, axis=-1, keepdims=True) + RMS_EPS)
    return (xt * inv * gain) * (1.0 + scale) + shift


def _mod_kernel(c_ref, w_ref, b_ref, o_ref):
    c = c_ref[...]
    cs = c * jax.nn.sigmoid(c)
    o_ref[0] = jnp.dot(cs, w_ref[0], preferred_element_type=F32, precision=HIGHEST) + b_ref[0]


def _adaln_mod(c_pad, w, b):
    n_layers, d, n = w.shape
    tn = min(MOD_TN, n)
    return pl.pallas_call(
        _mod_kernel,
        grid=(n_layers, n // tn),
        in_specs=[pl.BlockSpec((SUBLANES, d), lambda l, j: (0, 0)),
                  pl.BlockSpec((1, d, tn), lambda l, j: (l, 0, j)),
                  pl.BlockSpec((1, 1, tn), lambda l, j: (l, 0, j))],
        out_specs=pl.BlockSpec((1, SUBLANES, tn), lambda l, j: (l, 0, j)),
        out_shape=jax.ShapeDtypeStruct((n_layers, SUBLANES, n), F32),
        compiler_params=_cparams(2),
        name="adaln_mod",
    )(c_pad, w, b.reshape(n_layers, 1, n))


def _t5_thresholds():
    max_exact = REL_BUCKETS // 2
    d = np.arange(0, 4 * REL_MAX_DIST)
    dd = np.maximum(d, 1).astype(np.float32)
    large = max_exact + (np.log(dd / max_exact) / math.log(REL_MAX_DIST / max_exact)
                         * (REL_BUCKETS - max_exact)).astype(np.int32)
    large = np.minimum(large, REL_BUCKETS - 1)
    bucket = np.where(d < max_exact, d, large)
    return [int(np.argmax(bucket >= b)) for b in range(REL_BUCKETS)]


def _t5_kernel(rb_ref, o_ref, *, thr):
    h = pl.program_id(0)
    shape = (2 * MOBA_BLOCK, MOBA_BLOCK)
    key = lax.broadcasted_iota(jnp.int32, shape, 0)
    qry = lax.broadcasted_iota(jnp.int32, shape, 1)
    dist = qry + MOBA_BLOCK - key
    val = jnp.full(shape, rb_ref[h, REL_BUCKETS - 1], F32)
    for b in range(REL_BUCKETS - 2, -1, -1):
        val = jnp.where(dist < thr[b + 1], rb_ref[h, b], val)
    o_ref[0] = jnp.where(dist < 0, NEG_INF, LOG2E * val)


def _t5_table(rel_bias):
    return pl.pallas_call(
        functools.partial(_t5_kernel, thr=_t5_thresholds()),
        grid=(N_HEADS,),
        in_specs=[pl.BlockSpec(memory_space=pltpu.SMEM)],
        out_specs=pl.BlockSpec((1, 2 * MOBA_BLOCK, MOBA_BLOCK), lambda h: (h, 0, 0)),
        out_shape=jax.ShapeDtypeStruct((N_HEADS, 2 * MOBA_BLOCK, MOBA_BLOCK), F32),
        compiler_params=_cparams(1),
        name="t5_table",
    )(rel_bias.astype(F32))


def _shift_rows(cur, prev_tail, k):
    rolled = pltpu.roll(cur, k, 0)
    fix = pltpu.roll(prev_tail, k, 0)
    row = lax.broadcasted_iota(jnp.int32, prev_tail.shape, 0)
    top = jnp.where(row < k, fix, rolled[:SUBLANES])
    return jnp.concatenate([top, rolled[SUBLANES:]], axis=0)


def _lru_scan(a, b, h0):
    t, c = a.shape
    row = lax.broadcasted_iota(jnp.int32, a.shape, 0)
    s = 1
    while s < t:
        if s < SUBLANES:
            a_sh = jnp.where(row < s, 1.0, pltpu.roll(a, s, 0))
            b_sh = jnp.where(row < s, 0.0, pltpu.roll(b, s, 0))
        else:
            a_sh = jnp.concatenate([jnp.ones((s, c), F32), a[:t - s]], axis=0)
            b_sh = jnp.concatenate([jnp.zeros((s, c), F32), b[:t - s]], axis=0)
        b = a * b_sh + b
        a = a * a_sh
        s *= 2
    return a * h0 + b


def _rglru_kernel(x_ref, mod_ref, g_ref, win_ref, cw_ref, cb_ref, wg_ref, bg_ref, lam_ref, wout_ref,
                  o_ref, conv_scr, h_scr):
    d = x_ref.shape[-1]
    w = lam_ref.shape[-1]
    tm = x_ref.shape[1]

    @pl.when(pl.program_id(1) == 0)
    def _():
        conv_scr[...] = jnp.zeros_like(conv_scr)
        h_scr[...] = jnp.zeros_like(h_scr)

    xt = x_ref[0]
    mod = mod_ref[0]
    h = _norm_mod(xt, g_ref[...], mod[:, 0:d], mod[:, d:2 * d]).astype(BF16)
    acc = jnp.zeros((tm, d), F32)
    for hd in range(w // LRU_BLOCK):
        lo = hd * LRU_BLOCK
        hi = lo + LRU_BLOCK
        y = jax.nn.gelu(_dot(h, win_ref[:, lo:hi]))
        ux = _dot(h, win_ref[:, w + lo:w + hi])
        prev_tail = conv_scr[:, lo:hi]
        cw = cw_ref[:, lo:hi]
        xc = ux * cw[CONV_WIDTH - 1:CONV_WIDTH]
        for k in range(1, CONV_WIDTH):
            xc = xc + _shift_rows(ux, prev_tail, k) * cw[CONV_WIDTH - 1 - k:CONV_WIDTH - k]
        xc = xc + cb_ref[:, lo:hi]
        conv_scr[:, lo:hi] = ux[tm - SUBLANES:]
        gates = _dot(xc.astype(BF16), wg_ref[hd])
        r = jax.nn.sigmoid(gates[:, :LRU_BLOCK] + bg_ref[0:1, lo:hi])
        i = jax.nn.sigmoid(gates[:, LRU_BLOCK:] + bg_ref[1:2, lo:hi])
        lam = lam_ref[:, lo:hi]
        softplus_neg = jnp.maximum(-lam, 0.0) + jnp.log1p(jnp.exp(-jnp.abs(lam)))
        log_a = (-LRU_C * r) * softplus_neg
        a = jnp.exp(log_a)
        mult = jnp.sqrt((1.0 + a * a) * jnp.tanh(-log_a))
        b = mult * (i * xc)
        hs = _lru_scan(a, b, h_scr[0:1, lo:hi])
        h_scr[0:1, lo:hi] = hs[tm - 1:tm]
        acc = acc + _dot((hs * y).astype(BF16), wout_ref[lo:hi, :])
    o_ref[0] = xt + mod[:, 2 * d:3 * d] * acc


def _rglru_layer(x, mod, gain, w_in, conv_w, conv_b, w_gates, b_gates, lam, w_out):
    b, s, d = x.shape
    w = lam.shape[-1]
    tm = TM_LRU
    return pl.pallas_call(
        _rglru_kernel,
        grid=(b, s // tm),
        in_specs=[pl.BlockSpec((1, tm, d), lambda i, t: (i, t, 0)),
                  pl.BlockSpec((1, 1, mod.shape[-1]), lambda i, t: (i, 0, 0)),
                  _const_spec((1, d)),
                  _const_spec(w_in.shape),
                  _const_spec(conv_w.shape),
                  _const_spec((1, w)),
                  _const_spec(w_gates.shape),
                  _const_spec(b_gates.shape),
                  _const_spec((1, w)),
                  _const_spec(w_out.shape)],
        out_specs=pl.BlockSpec((1, tm, d), lambda i, t: (i, t, 0)),
        out_shape=jax.ShapeDtypeStruct(x.shape, F32),
        scratch_shapes=[pltpu.VMEM((SUBLANES, w), F32), pltpu.VMEM((SUBLANES, w), F32)],
        compiler_params=_cparams(2),
        name="rglru",
    )(x, mod, gain.reshape(1, d), w_in, conv_w, conv_b.reshape(1, w), w_gates, b_gates,
      lam.reshape(1, w), w_out)


def _ffn_kernel(*refs, has_attn, final):
    x_ref, mod_ref, g_ref, wg_ref, wu_ref, wd_ref = refs[:6]
    rest = list(refs[6:])
    att_ref = wo_ref = fn_ref = None
    if has_attn:
        att_ref, wo_ref = rest[:2]
        rest = rest[2:]
    if final:
        fn_ref = rest[0]
        rest = rest[1:]
    o_ref, = rest
    d = x_ref.shape[-1]
    ff = wg_ref.shape[-1]
    tm = x_ref.shape[1]

    xt = x_ref[0]
    mod = mod_ref[0]
    if has_attn:
        xt = xt + mod[:, 2 * d:3 * d] * _dot(att_ref[0], wo_ref[...])
    h = _norm_mod(xt, g_ref[...], mod[:, 3 * d:4 * d], mod[:, 4 * d:5 * d]).astype(BF16)
    acc = jnp.zeros((tm, d), F32)
    for c in range(ff // FF_CHUNK):
        lo = c * FF_CHUNK
        hi = lo + FF_CHUNK
        gate = _dot(h, wg_ref[:, lo:hi])
        up = _dot(h, wu_ref[:, lo:hi])
        act = (gate * jax.nn.sigmoid(gate)) * up
        acc = acc + _dot(act.astype(BF16), wd_ref[lo:hi, :])
    y = xt + mod[:, 5 * d:6 * d] * acc
    if final:
        y = y * lax.rsqrt(jnp.mean(y * y, axis=-1, keepdims=True) + RMS_EPS) * fn_ref[...]
    o_ref[0] = y


def _ffn_layer(x, mod, gain, w_gate, w_up, w_down, attn=None, w_o=None, final_gain=None):
    b, s, d = x.shape
    tm = TM_FFN
    has_attn = attn is not None
    final = final_gain is not None
    in_specs = [pl.BlockSpec((1, tm, d), lambda i, t: (i, t, 0)),
                pl.BlockSpec((1, 1, mod.shape[-1]), lambda i, t: (i, 0, 0)),
                _const_spec((1, d)),
                _const_spec(w_gate.shape),
                _const_spec(w_up.shape),
                _const_spec(w_down.shape)]
    args = [x, mod, gain.reshape(1, d), w_gate, w_up, w_down]
    if has_attn:
        in_specs += [pl.BlockSpec((1, tm, d), lambda i, t: (i, t, 0)), _const_spec(w_o.shape)]
        args += [attn, w_o]
    if final:
        in_specs += [_const_spec((1, d))]
        args += [final_gain.reshape(1, d)]
    return pl.pallas_call(
        functools.partial(_ffn_kernel, has_attn=has_attn, final=final),
        grid=(b, s // tm),
        in_specs=in_specs,
        out_specs=pl.BlockSpec((1, tm, d), lambda i, t: (i, t, 0)),
        out_shape=jax.ShapeDtypeStruct(x.shape, F32),
        compiler_params=_cparams(2),
        name="ffn_attn_final" if has_attn else "ffn",
    )(*args)


def _qkv_kernel(x_ref, kvmod_ref, mod_ref, kvg_ref, qg_ref, wk_ref, wvt_ref, wq_ref,
                q_ref, k_ref, vt_ref, mask_ref, km_scr):
    d = x_ref.shape[-1]
    tm = x_ref.shape[1]
    nb = mask_ref.shape[3]
    j = pl.program_id(1)

    @pl.when(j == 0)
    def _():
        km_scr[...] = jnp.zeros_like(km_scr)

    xt = x_ref[0]
    inv = lax.rsqrt(jnp.mean(xt * xt, axis=-1, keepdims=True) + RMS_EPS)
    xn = xt * inv
    kvmod = kvmod_ref[0]
    mod = mod_ref[0]
    h_kv = ((xn * kvg_ref[...]) * (1.0 + kvmod[:, d:2 * d]) + kvmod[:, 0:d]).astype(BF16)
    h_q = ((xn * qg_ref[...]) * (1.0 + mod[:, d:2 * d]) + mod[:, 0:d]).astype(BF16)
    k = _dot(h_kv, wk_ref[...])
    vt = _dot_nt(wvt_ref[...], h_kv)
    q = _dot(h_q, wq_ref[...]) * (HEAD_DIM ** -0.5 * LOG2E)
    q_ref[0] = q.astype(BF16)
    for hk in range(N_KV_HEADS):
        lo = hk * HEAD_DIM
        k_ref[0, hk, 0] = k[:, lo:lo + HEAD_DIM].astype(BF16)
        vt_ref[0, hk, 0] = vt[lo:lo + HEAD_DIM, :].astype(BF16)

    blk = lax.broadcasted_iota(jnp.int32, (nb, tm), 0).astype(F32)
    past = blk < j.astype(F32)
    for hq in range(N_HEADS):
        hk = hq // KV_GROUP
        km_h = km_scr[:, hk * HEAD_DIM:(hk + 1) * HEAD_DIM]
        gate = _dot_nt(km_h, q[:, hq * HEAD_DIM:(hq + 1) * HEAD_DIM], precision=HIGHEST)
        gate = jnp.where(past, gate, NEG_INF)
        mask = jnp.full((nb, tm), NEG_INF, F32)
        for _ in range(MOBA_TOPK):
            best = jnp.max(gate, axis=0, keepdims=True)
            first = jnp.min(jnp.where(gate == best, blk, nb), axis=0, keepdims=True)
            hit = blk == first
            mask = jnp.where(hit, 0.0, mask)
            gate = jnp.where(hit, -jnp.inf, gate)
        mask = jnp.where(past, mask, NEG_INF)
        lo = (hq % KV_GROUP) * tm
        mask_ref[0, hk, 0, :, lo:lo + tm] = mask

    km_scr[pl.ds(j, 1), :] = jnp.mean(k, axis=0, keepdims=True)


def _qkv_gate(x, kvmod, mod, kv_gain, q_gain, w_k, w_vt, w_q):
    b, s, d = x.shape
    tm = MOBA_BLOCK
    nb = s // tm
    dkv = N_KV_HEADS * HEAD_DIM
    dq = N_HEADS * HEAD_DIM
    out_shape = (jax.ShapeDtypeStruct((b, s, dq), BF16),
                 jax.ShapeDtypeStruct((b, N_KV_HEADS, nb, tm, HEAD_DIM), BF16),
                 jax.ShapeDtypeStruct((b, N_KV_HEADS, nb, HEAD_DIM, tm), BF16),
                 jax.ShapeDtypeStruct((b, N_KV_HEADS, nb, nb, KV_GROUP * tm), F32))
    out_specs = (pl.BlockSpec((1, tm, dq), lambda i, j: (i, j, 0)),
                 pl.BlockSpec((1, N_KV_HEADS, 1, tm, HEAD_DIM), lambda i, j: (i, 0, j, 0, 0)),
                 pl.BlockSpec((1, N_KV_HEADS, 1, HEAD_DIM, tm), lambda i, j: (i, 0, j, 0, 0)),
                 pl.BlockSpec((1, N_KV_HEADS, 1, nb, KV_GROUP * tm), lambda i, j: (i, 0, j, 0, 0)))
    return pl.pallas_call(
        _qkv_kernel,
        grid=(b, nb),
        in_specs=[pl.BlockSpec((1, tm, d), lambda i, j: (i, j, 0)),
                  pl.BlockSpec((1, 1, kvmod.shape[-1]), lambda i, j: (i, 0, 0)),
                  pl.BlockSpec((1, 1, mod.shape[-1]), lambda i, j: (i, 0, 0)),
                  _const_spec((1, d)),
                  _const_spec((1, d)),
                  _const_spec(w_k.shape),
                  _const_spec(w_vt.shape),
                  _const_spec(w_q.shape)],
        out_specs=out_specs,
        out_shape=out_shape,
        scratch_shapes=[pltpu.VMEM((nb, dkv), F32)],
        compiler_params=_cparams(2),
        name="qkv_gate",
    )(x, kvmod, mod, kv_gain.reshape(1, d), q_gain.reshape(1, d), w_k, w_vt, w_q)


def _attn_kernel(rb_ref, q_ref, k_ref, vt_ref, mask_ref, tab_ref, o_ref, m_scr, l_scr, acc_scr):
    g = pl.program_id(1)
    j = pl.program_id(2)
    nb = k_ref.shape[2]
    tq = q_ref.shape[1]
    q2 = q_ref[0]
    qs = jnp.concatenate([q2[:, :HEAD_DIM], q2[:, HEAD_DIM:]], axis=0)

    def scores(n):
        return _dot_nt(k_ref[0, 0, n], qs)

    def tile_bias(lo):
        return jnp.concatenate([tab_ref[0, lo:lo + MOBA_BLOCK, :], tab_ref[1, lo:lo + MOBA_BLOCK, :]], axis=1)

    def mask_row(n):
        return mask_ref[0, 0, 0, pl.ds(n, 1), :]

    def col_max(s):
        return jnp.max(s, axis=0, keepdims=True)

    def col_sum(s):
        return jnp.sum(s, axis=0, keepdims=True)

    jp = jnp.maximum(j - 1, 0)
    s_own = scores(j) + tile_bias(MOBA_BLOCK)
    s_prev = scores(jp) + tile_bias(0) + jnp.where(j >= 1, mask_row(jp), NEG_INF)
    m0 = jnp.maximum(col_max(s_own), col_max(s_prev))
    p_own = jnp.exp2(s_own - m0)
    p_prev = jnp.exp2(s_prev - m0)
    m_scr[0:1] = m0
    l_scr[0:1] = col_sum(p_own) + col_sum(p_prev)
    acc_scr[...] = _dot(vt_ref[0, 0, j], p_own.astype(BF16)) + _dot(vt_ref[0, 0, jp], p_prev.astype(BF16))

    lane = lax.broadcasted_iota(jnp.int32, (1, KV_GROUP * tq), 1)
    far_bias = LOG2E * jnp.where(lane < tq, rb_ref[KV_GROUP * g, REL_BUCKETS - 1],
                                 rb_ref[KV_GROUP * g + 1, REL_BUCKETS - 1])
    n_far = jnp.maximum(j - 1, 0)

    def far(gi, carry):
        m_old = m_scr[0:1]
        blocks, raw, rows = [], [], []
        for u in range(FAR_GROUP):
            n = gi * FAR_GROUP + u
            nc = jnp.minimum(n, nb - 1)
            blocks.append(nc)
            raw.append(scores(nc))
            rows.append(jnp.where(n < n_far, mask_row(nc) + far_bias, NEG_INF))
        m_new = m_old
        for u in range(FAR_GROUP):
            m_new = jnp.maximum(m_new, col_max(raw[u]) + rows[u])
        alpha = jnp.exp2(m_old - m_new)
        l_new = alpha * l_scr[0:1]
        pv = None
        for u in range(FAR_GROUP):
            p = jnp.exp2(raw[u] + (rows[u] - m_new))
            l_new = l_new + col_sum(p)
            d = _dot(vt_ref[0, 0, blocks[u]], p.astype(BF16))
            pv = d if pv is None else pv + d
        acc_scr[...] = alpha * acc_scr[...] + pv
        l_scr[0:1] = l_new
        m_scr[0:1] = m_new
        return carry

    lax.fori_loop(0, (n_far + FAR_GROUP - 1) // FAR_GROUP, far, 0)

    o = acc_scr[...] * (1.0 / l_scr[0:1])
    o_ref[0] = jnp.concatenate([o[:, :tq].T, o[:, tq:].T], axis=1).astype(BF16)


def _moba_attn(rel_bias, q, k, vt, mask, table):
    b, s, dq = q.shape
    nb = k.shape[2]
    tq = MOBA_BLOCK
    gw = KV_GROUP * HEAD_DIM
    return pl.pallas_call(
        _attn_kernel,
        grid=(b, N_KV_HEADS, nb),
        in_specs=[pl.BlockSpec(memory_space=pltpu.SMEM),
                  pl.BlockSpec((1, tq, gw), lambda i, g, j: (i, j, g)),
                  pl.BlockSpec((1, 1, nb, tq, HEAD_DIM), lambda i, g, j: (i, g, 0, 0, 0)),
                  pl.BlockSpec((1, 1, nb, HEAD_DIM, tq), lambda i, g, j: (i, g, 0, 0, 0)),
                  pl.BlockSpec((1, 1, 1, nb, KV_GROUP * tq), lambda i, g, j: (i, g, j, 0, 0)),
                  pl.BlockSpec((KV_GROUP, 2 * MOBA_BLOCK, MOBA_BLOCK), lambda i, g, j: (g, 0, 0))],
        out_specs=pl.BlockSpec((1, tq, gw), lambda i, g, j: (i, j, g)),
        out_shape=jax.ShapeDtypeStruct((b, s, dq), BF16),
        scratch_shapes=[pltpu.VMEM((SUBLANES, KV_GROUP * tq), F32),
                        pltpu.VMEM((SUBLANES, KV_GROUP * tq), F32),
                        pltpu.VMEM((HEAD_DIM, KV_GROUP * tq), F32)],
        compiler_params=_cparams(3),
        name="moba_attn",
    )(rel_bias.astype(F32), q, k, vt, mask, table)


def kernel(x, c, mod_w, mod_b, norm_mix, norm_ffn, lru_w_in, lru_conv_w, lru_conv_b, lru_w_gates,
           lru_b_gates, lru_lambda, lru_w_out, kv_mod_w, kv_mod_b, kv_norm, w_kv, attn_w_q, attn_w_o,
           rel_bias, ffn_w_gate, ffn_w_up, ffn_w_down, final_norm):
    b, s, d = x.shape
    assert s % TM_FFN == 0 and s % MOBA_BLOCK == 0 and b <= SUBLANES
    dkv = N_KV_HEADS * HEAD_DIM

    c_pad = jnp.zeros((SUBLANES, d), F32).at[:b].set(c.astype(F32))
    mod = _adaln_mod(c_pad, mod_w, mod_b)[:, :b]
    kvmod = _adaln_mod(c_pad, kv_mod_w[None], kv_mod_b[None])[0, :b]
    mod0 = mod[0][:, None, :]
    mod1 = mod[1][:, None, :]
    kvmod = kvmod[:, None, :]

    x = _rglru_layer(x, mod0, norm_mix[0], lru_w_in[0].astype(BF16), lru_conv_w[0], lru_conv_b[0],
                     lru_w_gates[0].astype(BF16), lru_b_gates[0], lru_lambda[0], lru_w_out[0].astype(BF16))
    x = _ffn_layer(x, mod0, norm_ffn[0], ffn_w_gate[0].astype(BF16), ffn_w_up[0].astype(BF16),
                   ffn_w_down[0].astype(BF16))

    q, k, vt, mask = _qkv_gate(x, kvmod, mod1, kv_norm, norm_mix[1], w_kv[:, :dkv].astype(BF16),
                               w_kv[:, dkv:].T.astype(BF16), attn_w_q[0].astype(BF16))
    table = _t5_table(rel_bias)
    attn = _moba_attn(rel_bias, q, k, vt, mask, table)
    return _ffn_layer(x, mod1, norm_ffn[1], ffn_w_gate[1].astype(BF16), ffn_w_up[1].astype(BF16),
                      ffn_w_down[1].astype(BF16), attn=attn, w_o=attn_w_o[0].astype(BF16),
                      final_gain=final_norm)
```

```python
import functools
import math

import numpy as np
import jax
import jax.numpy as jnp
from jax import lax
from jax.experimental import pallas as pl
from jax.experimental.pallas import tpu as pltpu

F32 = jnp.float32
BF16 = jnp.bfloat16
HIGHEST = lax.Precision.HIGHEST

LRU_BLOCK = 256
CONV_WIDTH = 4
LRU_C = 8.0
N_HEADS = 8
N_KV_HEADS = 4
HEAD_DIM = 128
KV_GROUP = N_HEADS // N_KV_HEADS
MOBA_BLOCK = 256
MOBA_TOPK = 3
REL_BUCKETS = 32
REL_MAX_DIST = 128
RMS_EPS = 1e-6
NEG_INF = -1e30
LOG2E = math.log2(math.e)

SUBLANES = 8
LANES = 128
VMEM_LIMIT = 56 * 1024 * 1024

TM_LRU = 256
TM_FFN = 512
FF_CHUNK = 256
MOD_TN = 2048
GROUP = 4
NEAR_FAR = GROUP - 2


def _cparams(n_axes):
    return pltpu.CompilerParams(dimension_semantics=("arbitrary",) * n_axes,
                                vmem_limit_bytes=VMEM_LIMIT)


def _const_spec(shape):
    nd = len(shape)
    return pl.BlockSpec(shape, lambda *_: (0,) * nd, pipeline_mode=pl.Buffered(1))


def _dot(a, b):
    return jnp.dot(a, b, preferred_element_type=F32)


def _dot_nt(a, b, precision=None):
    return lax.dot_general(a, b, (((1,), (1,)), ((), ())), preferred_element_type=F32,
                           precision=precision)


def _norm_mod(xt, gain, shift, scale):
    inv = lax.rsqrt(jnp.mean(xt * xt, axis=-1, keepdims=True) + RMS_EPS)
    return (xt * inv * gain) * (1.0 + scale) + shift


def _mod_kernel(c_ref, w_ref, b_ref, o_ref):
    c = c_ref[...]
    cs = c * jax.nn.sigmoid(c)
    o_ref[0] = jnp.dot(cs, w_ref[0], preferred_element_type=F32, precision=HIGHEST) + b_ref[0]


def _adaln_mod(c_pad, w, b):
    n_layers, d, n = w.shape
    tn = min(MOD_TN, n)
    return pl.pallas_call(
        _mod_kernel,
        grid=(n_layers, n // tn),
        in_specs=[pl.BlockSpec((SUBLANES, d), lambda l, j: (0, 0)),
                  pl.BlockSpec((1, d, tn), lambda l, j: (l, 0, j)),
                  pl.BlockSpec((1, 1, tn), lambda l, j: (l, 0, j))],
        out_specs=pl.BlockSpec((1, SUBLANES, tn), lambda l, j: (l, 0, j)),
        out_shape=jax.ShapeDtypeStruct((n_layers, SUBLANES, n), F32),
        compiler_params=_cparams(2),
        name="adaln_mod",
    )(c_pad, w, b.reshape(n_layers, 1, n))


def _t5_thresholds():
    max_exact = REL_BUCKETS // 2
    d = np.arange(0, 4 * REL_MAX_DIST)
    dd = np.maximum(d, 1).astype(np.float32)
    large = max_exact + (np.log(dd / max_exact) / math.log(REL_MAX_DIST / max_exact)
                         * (REL_BUCKETS - max_exact)).astype(np.int32)
    large = np.minimum(large, REL_BUCKETS - 1)
    bucket = np.where(d < max_exact, d, large)
    return [int(np.argmax(bucket >= b)) for b in range(REL_BUCKETS)]


def _t5_kernel(rb_ref, o_ref, *, thr):
    h = pl.program_id(0)
    shape = (2 * MOBA_BLOCK, MOBA_BLOCK)
    key = lax.broadcasted_iota(jnp.int32, shape, 0)
    qry = lax.broadcasted_iota(jnp.int32, shape, 1)
    dist = qry + MOBA_BLOCK - key
    val = jnp.full(shape, rb_ref[h, REL_BUCKETS - 1], F32)
    for b in range(REL_BUCKETS - 2, -1, -1):
        val = jnp.where(dist < thr[b + 1], rb_ref[h, b], val)
    o_ref[0] = jnp.where(dist < 0, NEG_INF, LOG2E * val)


def _t5_table(rel_bias):
    return pl.pallas_call(
        functools.partial(_t5_kernel, thr=_t5_thresholds()),
        grid=(N_HEADS,),
        in_specs=[pl.BlockSpec(memory_space=pltpu.SMEM)],
        out_specs=pl.BlockSpec((1, 2 * MOBA_BLOCK, MOBA_BLOCK), lambda h: (h, 0, 0)),
        out_shape=jax.ShapeDtypeStruct((N_HEADS, 2 * MOBA_BLOCK, MOBA_BLOCK), F32),
        compiler_params=_cparams(1),
        name="t5_table",
    )(rel_bias.astype(F32))


def _shift_rows(cur, prev_tail, k):
    rolled = pltpu.roll(cur, k, 0)
    fix = pltpu.roll(prev_tail, k, 0)
    row = lax.broadcasted_iota(jnp.int32, prev_tail.shape, 0)
    top = jnp.where(row < k, fix, rolled[:SUBLANES])
    return jnp.concatenate([top, rolled[SUBLANES:]], axis=0)


def _lru_scan(a, b, h0):
    t, c = a.shape
    row = lax.broadcasted_iota(jnp.int32, a.shape, 0)
    s = 1
    while s < t:
        if s < SUBLANES:
            a_sh = jnp.where(row < s, 1.0, pltpu.roll(a, s, 0))
            b_sh = jnp.where(row < s, 0.0, pltpu.roll(b, s, 0))
        else:
            a_sh = jnp.concatenate([jnp.ones((s, c), F32), a[:t - s]], axis=0)
            b_sh = jnp.concatenate([jnp.zeros((s, c), F32), b[:t - s]], axis=0)
        b = a * b_sh + b
        a = a * a_sh
        s *= 2
    return a * h0 + b


def _rglru_kernel(x_ref, mod_ref, g_ref, win_ref, cw_ref, cb_ref, wg_ref, bg_ref, lam_ref, wout_ref,
                  o_ref, conv_scr, h_scr):
    d = x_ref.shape[-1]
    w = lam_ref.shape[-1]
    tm = x_ref.shape[1]

    @pl.when(pl.program_id(1) == 0)
    def _():
        conv_scr[...] = jnp.zeros_like(conv_scr)
        h_scr[...] = jnp.zeros_like(h_scr)

    xt = x_ref[0]
    mod = mod_ref[0]
    h = _norm_mod(xt, g_ref[...], mod[:, 0:d], mod[:, d:2 * d]).astype(BF16)
    acc = jnp.zeros((tm, d), F32)
    for hd in range(w // LRU_BLOCK):
        lo = hd * LRU_BLOCK
        hi = lo + LRU_BLOCK
        y = jax.nn.gelu(_dot(h, win_ref[:, lo:hi]))
        ux = _dot(h, win_ref[:, w + lo:w + hi])
        prev_tail = conv_scr[:, lo:hi]
        cw = cw_ref[:, lo:hi]
        xc = ux * cw[CONV_WIDTH - 1:CONV_WIDTH]
        for k in range(1, CONV_WIDTH):
            xc = xc + _shift_rows(ux, prev_tail, k) * cw[CONV_WIDTH - 1 - k:CONV_WIDTH - k]
        xc = xc + cb_ref[:, lo:hi]
        conv_scr[:, lo:hi] = ux[tm - SUBLANES:]
        gates = _dot(xc.astype(BF16), wg_ref[hd])
        r = jax.nn.sigmoid(gates[:, :LRU_BLOCK] + bg_ref[0:1, lo:hi])
        i = jax.nn.sigmoid(gates[:, LRU_BLOCK:] + bg_ref[1:2, lo:hi])
        lam = lam_ref[:, lo:hi]
        softplus_neg = jnp.maximum(-lam, 0.0) + jnp.log1p(jnp.exp(-jnp.abs(lam)))
        log_a = (-LRU_C * r) * softplus_neg
        a = jnp.exp(log_a)
        mult = jnp.sqrt((1.0 + a * a) * jnp.tanh(-log_a))
        b = mult * (i * xc)
        hs = _lru_scan(a, b, h_scr[0:1, lo:hi])
        h_scr[0:1, lo:hi] = hs[tm - 1:tm]
        acc = acc + _dot((hs * y).astype(BF16), wout_ref[lo:hi, :])
    o_ref[0] = xt + mod[:, 2 * d:3 * d] * acc


def _rglru_layer(x, mod, gain, w_in, conv_w, conv_b, w_gates, b_gates, lam, w_out):
    b, s, d = x.shape
    w = lam.shape[-1]
    tm = TM_LRU
    return pl.pallas_call(
        _rglru_kernel,
        grid=(b, s // tm),
        in_specs=[pl.BlockSpec((1, tm, d), lambda i, t: (i, t, 0)),
                  pl.BlockSpec((1, 1, mod.shape[-1]), lambda i, t: (i, 0, 0)),
                  _const_spec((1, d)),
                  _const_spec(w_in.shape),
                  _const_spec(conv_w.shape),
                  _const_spec((1, w)),
                  _const_spec(w_gates.shape),
                  _const_spec(b_gates.shape),
                  _const_spec((1, w)),
                  _const_spec(w_out.shape)],
        out_specs=pl.BlockSpec((1, tm, d), lambda i, t: (i, t, 0)),
        out_shape=jax.ShapeDtypeStruct(x.shape, F32),
        scratch_shapes=[pltpu.VMEM((SUBLANES, w), F32), pltpu.VMEM((SUBLANES, w), F32)],
        compiler_params=_cparams(2),
        name="rglru",
    )(x, mod, gain.reshape(1, d), w_in, conv_w, conv_b.reshape(1, w), w_gates, b_gates,
      lam.reshape(1, w), w_out)


def _ffn_kernel(*refs, has_attn, final):
    x_ref, mod_ref, g_ref, wg_ref, wu_ref, wd_ref = refs[:6]
    rest = list(refs[6:])
    att_ref = wo_ref = fn_ref = None
    if has_attn:
        att_ref, wo_ref = rest[:2]
        rest = rest[2:]
    if final:
        fn_ref = rest[0]
        rest = rest[1:]
    o_ref, = rest
    d = x_ref.shape[-1]
    ff = wg_ref.shape[-1]
    tm = x_ref.shape[1]

    xt = x_ref[0]
    mod = mod_ref[0]
    if has_attn:
        xt = xt + mod[:, 2 * d:3 * d] * _dot(att_ref[0], wo_ref[...])
    h = _norm_mod(xt, g_ref[...], mod[:, 3 * d:4 * d], mod[:, 4 * d:5 * d]).astype(BF16)
    acc = jnp.zeros((tm, d), F32)
    for c in range(ff // FF_CHUNK):
        lo = c * FF_CHUNK
        hi = lo + FF_CHUNK
        gate = _dot(h, wg_ref[:, lo:hi])
        up = _dot(h, wu_ref[:, lo:hi])
        act = (gate * jax.nn.sigmoid(gate)) * up
        acc = acc + _dot(act.astype(BF16), wd_ref[lo:hi, :])
    y = xt + mod[:, 5 * d:6 * d] * acc
    if final:
        y = y * lax.rsqrt(jnp.mean(y * y, axis=-1, keepdims=True) + RMS_EPS) * fn_ref[...]
    o_ref[0] = y


def _ffn_layer(x, mod, gain, w_gate, w_up, w_down, attn=None, w_o=None, final_gain=None):
    b, s, d = x.shape
    tm = TM_FFN
    has_attn = attn is not None
    final = final_gain is not None
    in_specs = [pl.BlockSpec((1, tm, d), lambda i, t: (i, t, 0)),
                pl.BlockSpec((1, 1, mod.shape[-1]), lambda i, t: (i, 0, 0)),
                _const_spec((1, d)),
                _const_spec(w_gate.shape),
                _const_spec(w_up.shape),
                _const_spec(w_down.shape)]
    args = [x, mod, gain.reshape(1, d), w_gate, w_up, w_down]
    if has_attn:
        in_specs += [pl.BlockSpec((1, tm, d), lambda i, t: (i, t, 0)), _const_spec(w_o.shape)]
        args += [attn, w_o]
    if final:
        in_specs += [_const_spec((1, d))]
        args += [final_gain.reshape(1, d)]
    return pl.pallas_call(
        functools.partial(_ffn_kernel, has_attn=has_attn, final=final),
        grid=(b, s // tm),
        in_specs=in_specs,
        out_specs=pl.BlockSpec((1, tm, d), lambda i, t: (i, t, 0)),
        out_shape=jax.ShapeDtypeStruct(x.shape, F32),
        compiler_params=_cparams(2),
        name="ffn_attn_final" if has_attn else "ffn",
    )(*args)


def _qkv_kernel(x_ref, kvmod_ref, mod_ref, kvg_ref, qg_ref, wk_ref, wvt_ref, wq_ref,
                q_ref, k_ref, vt_ref, mask_ref, km_scr):
    d = x_ref.shape[-1]
    tm = x_ref.shape[1]
    nb = mask_ref.shape[3]
    j = pl.program_id(1)

    @pl.when(j == 0)
    def _():
        km_scr[...] = jnp.zeros_like(km_scr)

    xt = x_ref[0]
    inv = lax.rsqrt(jnp.mean(xt * xt, axis=-1, keepdims=True) + RMS_EPS)
    xn = xt * inv
    kvmod = kvmod_ref[0]
    mod = mod_ref[0]
    h_kv = ((xn * kvg_ref[...]) * (1.0 + kvmod[:, d:2 * d]) + kvmod[:, 0:d]).astype(BF16)
    h_q = ((xn * qg_ref[...]) * (1.0 + mod[:, d:2 * d]) + mod[:, 0:d]).astype(BF16)
    k = _dot(h_kv, wk_ref[...])
    vt = _dot_nt(wvt_ref[...], h_kv)
    q = _dot(h_q, wq_ref[...]) * (HEAD_DIM ** -0.5 * LOG2E)
    q_ref[0] = q.astype(BF16)
    for hk in range(N_KV_HEADS):
        lo = hk * HEAD_DIM
        k_ref[0, hk, 0] = k[:, lo:lo + HEAD_DIM].astype(BF16)
        vt_ref[0, hk, 0] = vt[lo:lo + HEAD_DIM, :].astype(BF16)

    blk = lax.broadcasted_iota(jnp.int32, (nb, tm), 0).astype(F32)
    past = blk < j.astype(F32)
    for hq in range(N_HEADS):
        hk = hq // KV_GROUP
        km_h = km_scr[:, hk * HEAD_DIM:(hk + 1) * HEAD_DIM]
        gate = _dot_nt(km_h, q[:, hq * HEAD_DIM:(hq + 1) * HEAD_DIM], precision=HIGHEST)
        gate = jnp.where(past, gate, NEG_INF)
        mask = jnp.full((nb, tm), NEG_INF, F32)
        for _ in range(MOBA_TOPK):
            best = jnp.max(gate, axis=0, keepdims=True)
            first = jnp.min(jnp.where(gate == best, blk, nb), axis=0, keepdims=True)
            hit = blk == first
            mask = jnp.where(hit, 0.0, mask)
            gate = jnp.where(hit, -jnp.inf, gate)
        mask = jnp.where(past, mask, NEG_INF)
        lo = (hq % KV_GROUP) * tm
        mask_ref[0, hk, 0, :, lo:lo + tm] = mask

    km_scr[pl.ds(j, 1), :] = jnp.mean(k, axis=0, keepdims=True)


def _qkv_gate(x, kvmod, mod, kv_gain, q_gain, w_k, w_vt, w_q):
    b, s, d = x.shape
    tm = MOBA_BLOCK
    nb = s // tm
    dkv = N_KV_HEADS * HEAD_DIM
    dq = N_HEADS * HEAD_DIM
    out_shape = (jax.ShapeDtypeStruct((b, s, dq), BF16),
                 jax.ShapeDtypeStruct((b, N_KV_HEADS, nb, tm, HEAD_DIM), BF16),
                 jax.ShapeDtypeStruct((b, N_KV_HEADS, nb, HEAD_DIM, tm), BF16),
                 jax.ShapeDtypeStruct((b, N_KV_HEADS, nb, nb, KV_GROUP * tm), F32))
    out_specs = (pl.BlockSpec((1, tm, dq), lambda i, j: (i, j, 0)),
                 pl.BlockSpec((1, N_KV_HEADS, 1, tm, HEAD_DIM), lambda i, j: (i, 0, j, 0, 0)),
                 pl.BlockSpec((1, N_KV_HEADS, 1, HEAD_DIM, tm), lambda i, j: (i, 0, j, 0, 0)),
                 pl.BlockSpec((1, N_KV_HEADS, 1, nb, KV_GROUP * tm), lambda i, j: (i, 0, j, 0, 0)))
    return pl.pallas_call(
        _qkv_kernel,
        grid=(b, nb),
        in_specs=[pl.BlockSpec((1, tm, d), lambda i, j: (i, j, 0)),
                  pl.BlockSpec((1, 1, kvmod.shape[-1]), lambda i, j: (i, 0, 0)),
                  pl.BlockSpec((1, 1, mod.shape[-1]), lambda i, j: (i, 0, 0)),
                  _const_spec((1, d)),
                  _const_spec((1, d)),
                  _const_spec(w_k.shape),
                  _const_spec(w_vt.shape),
                  _const_spec(w_q.shape)],
        out_specs=out_specs,
        out_shape=out_shape,
        scratch_shapes=[pltpu.VMEM((nb, dkv), F32)],
        compiler_params=_cparams(2),
        name="qkv_gate",
    )(x, kvmod, mod, kv_gain.reshape(1, d), q_gain.reshape(1, d), w_k, w_vt, w_q)


def _attn_kernel(rb_ref, q_ref, k_ref, vt_ref, mask_ref, tab_ref, o_ref, m_scr, l_scr, acc_scr,
                 sa_scr, sb_scr, cma_scr, cmb_scr, rowa_scr, rowb_scr):
    g = pl.program_id(1)
    j = pl.program_id(2)
    nb = k_ref.shape[2]
    tq = q_ref.shape[1]
    q2 = q_ref[0]
    qs = jnp.concatenate([q2[:, :HEAD_DIM], q2[:, HEAD_DIM:]], axis=0)

    def scores(n):
        return _dot_nt(k_ref[0, 0, n], qs)

    def tile_bias(lo):
        return jnp.concatenate([tab_ref[0, lo:lo + MOBA_BLOCK, :], tab_ref[1, lo:lo + MOBA_BLOCK, :]], axis=1)

    def mask_row(n):
        return mask_ref[0, 0, 0, pl.ds(n, 1), :]

    def col_max(s):
        return jnp.max(s, axis=0, keepdims=True)

    def col_sum(s):
        return jnp.sum(s, axis=0, keepdims=True)

    lane = lax.broadcasted_iota(jnp.int32, (1, KV_GROUP * tq), 1)
    far_bias = LOG2E * jnp.where(lane < tq, rb_ref[KV_GROUP * g, REL_BUCKETS - 1],
                                 rb_ref[KV_GROUP * g + 1, REL_BUCKETS - 1])

    def far_row(n, valid):
        return jnp.where(valid, mask_row(n) + far_bias, NEG_INF)

    n_far = jnp.maximum(j - 1 - NEAR_FAR, 0)
    n_groups = 1 + (n_far + GROUP - 1) // GROUP
    jp = jnp.maximum(j - 1, 0)
    head_blocks = [j, jp] + [jnp.maximum(j - 2 - u, 0) for u in range(NEAR_FAR)]

    def group_blocks(k):
        return [jnp.where(k == 0, head_blocks[u], jnp.minimum((k - 1) * GROUP + u, nb - 1))
                for u in range(GROUP)]

    def put(buf, u, s, row):
        s_buf, cm_buf, row_buf = buf
        s_buf[u] = s
        cm_buf[u:u + 1] = col_max(s)
        row_buf[u:u + 1] = row

    def stage1_head(buf):
        put(buf, 0, scores(j) + tile_bias(MOBA_BLOCK), jnp.zeros((1, KV_GROUP * tq), F32))
        put(buf, 1, scores(jp) + tile_bias(0), jnp.where(j >= 1, mask_row(jp), NEG_INF))
        for u in range(NEAR_FAR):
            n = head_blocks[2 + u]
            put(buf, 2 + u, scores(n), far_row(n, j - 2 - u >= 0))

    def step(k, buf, nxt):
        s_buf, cm_buf, row_buf = buf
        blocks = group_blocks(k)
        m_old = m_scr[0:1]
        rows = [row_buf[u:u + 1] for u in range(GROUP)]
        m_new = m_old
        for u in range(GROUP):
            m_new = jnp.maximum(m_new, cm_buf[u:u + 1] + rows[u])
        alpha = jnp.exp2(m_old - m_new)
        l_new = alpha * l_scr[0:1]
        pv = None
        for u in range(GROUP):
            n = k * GROUP + u
            nc = jnp.minimum(n, nb - 1)
            put(nxt, u, scores(nc), far_row(nc, n < n_far))
            p = jnp.exp2(s_buf[u] + (rows[u] - m_new))
            l_new = l_new + col_sum(p)
            d = _dot(vt_ref[0, 0, blocks[u]], p.astype(BF16))
            pv = d if pv is None else pv + d
        acc_scr[...] = alpha * acc_scr[...] + pv
        l_scr[0:1] = l_new
        m_scr[0:1] = m_new

    buf_a = (sa_scr, cma_scr, rowa_scr)
    buf_b = (sb_scr, cmb_scr, rowb_scr)
    m_scr[...] = jnp.full_like(m_scr, NEG_INF)
    l_scr[...] = jnp.zeros_like(l_scr)
    acc_scr[...] = jnp.zeros_like(acc_scr)
    stage1_head(buf_a)

    def pair(t, carry):
        k = 2 * t
        step(k, buf_a, buf_b)

        @pl.when(k + 1 < n_groups)
        def _():
            step(k + 1, buf_b, buf_a)

        return carry

    lax.fori_loop(0, (n_groups + 1) // 2, pair, 0)

    o = acc_scr[...] * (1.0 / l_scr[0:1])
    o_ref[0] = jnp.concatenate([o[:, :tq].T, o[:, tq:].T], axis=1).astype(BF16)


def _moba_attn(rel_bias, q, k, vt, mask, table):
    b, s, dq = q.shape
    nb = k.shape[2]
    tq = MOBA_BLOCK
    gw = KV_GROUP * HEAD_DIM
    return pl.pallas_call(
        _attn_kernel,
        grid=(b, N_KV_HEADS, nb),
        in_specs=[pl.BlockSpec(memory_space=pltpu.SMEM),
                  pl.BlockSpec((1, tq, gw), lambda i, g, j: (i, j, g)),
                  pl.BlockSpec((1, 1, nb, tq, HEAD_DIM), lambda i, g, j: (i, g, 0, 0, 0)),
                  pl.BlockSpec((1, 1, nb, HEAD_DIM, tq), lambda i, g, j: (i, g, 0, 0, 0)),
                  pl.BlockSpec((1, 1, 1, nb, KV_GROUP * tq), lambda i, g, j: (i, g, j, 0, 0)),
                  pl.BlockSpec((KV_GROUP, 2 * MOBA_BLOCK, MOBA_BLOCK), lambda i, g, j: (g, 0, 0))],
        out_specs=pl.BlockSpec((1, tq, gw), lambda i, g, j: (i, j, g)),
        out_shape=jax.ShapeDtypeStruct((b, s, dq), BF16),
        scratch_shapes=[pltpu.VMEM((SUBLANES, KV_GROUP * tq), F32),
                        pltpu.VMEM((SUBLANES, KV_GROUP * tq), F32),
                        pltpu.VMEM((HEAD_DIM, KV_GROUP * tq), F32),
                        pltpu.VMEM((GROUP, MOBA_BLOCK, KV_GROUP * tq), F32),
                        pltpu.VMEM((GROUP, MOBA_BLOCK, KV_GROUP * tq), F32),
                        pltpu.VMEM((SUBLANES, KV_GROUP * tq), F32),
                        pltpu.VMEM((SUBLANES, KV_GROUP * tq), F32),
                        pltpu.VMEM((SUBLANES, KV_GROUP * tq), F32),
                        pltpu.VMEM((SUBLANES, KV_GROUP * tq), F32)],
        compiler_params=_cparams(3),
        name="moba_attn",
    )(rel_bias.astype(F32), q, k, vt, mask, table)


def kernel(x, c, mod_w, mod_b, norm_mix, norm_ffn, lru_w_in, lru_conv_w, lru_conv_b, lru_w_gates,
           lru_b_gates, lru_lambda, lru_w_out, kv_mod_w, kv_mod_b, kv_norm, w_kv, attn_w_q, attn_w_o,
           rel_bias, ffn_w_gate, ffn_w_up, ffn_w_down, final_norm):
    b, s, d = x.shape
    assert s % TM_FFN == 0 and s % MOBA_BLOCK == 0 and b <= SUBLANES
    dkv = N_KV_HEADS * HEAD_DIM

    c_pad = jnp.zeros((SUBLANES, d), F32).at[:b].set(c.astype(F32))
    mod = _adaln_mod(c_pad, mod_w, mod_b)[:, :b]
    kvmod = _adaln_mod(c_pad, kv_mod_w[None], kv_mod_b[None])[0, :b]
    mod0 = mod[0][:, None, :]
    mod1 = mod[1][:, None, :]
    kvmod = kvmod[:, None, :]

    x = _rglru_layer(x, mod0, norm_mix[0], lru_w_in[0].astype(BF16), lru_conv_w[0], lru_conv_b[0],
                     lru_w_gates[0].astype(BF16), lru_b_gates[0], lru_lambda[0], lru_w_out[0].astype(BF16))
    x = _ffn_layer(x, mod0, norm_ffn[0], ffn_w_gate[0].astype(BF16), ffn_w_up[0].astype(BF16),
                   ffn_w_down[0].astype(BF16))

    q, k, vt, mask = _qkv_gate(x, kvmod, mod1, kv_norm, norm_mix[1], w_kv[:, :dkv].astype(BF16),
                               w_kv[:, dkv:].T.astype(BF16), attn_w_q[0].astype(BF16))
    table = _t5_table(rel_bias)
    attn = _moba_attn(rel_bias, q, k, vt, mask, table)
    return _ffn_layer(x, mod1, norm_ffn[1], ffn_w_gate[1].astype(BF16), ffn_w_up[1].astype(BF16),
                      ffn_w_down[1].astype(BF16), attn=attn, w_o=attn_w_o[0].astype(BF16),
                      final_gain=final_norm)
```

```python
import functools
import math

import numpy as np
import jax
import jax.numpy as jnp
from jax import lax
from jax.experimental import pallas as pl
from jax.experimental.pallas import tpu as pltpu

F32 = jnp.float32
BF16 = jnp.bfloat16
HIGHEST = lax.Precision.HIGHEST

LRU_BLOCK = 256
CONV_WIDTH = 4
LRU_C = 8.0
N_HEADS = 8
N_KV_HEADS = 4
HEAD_DIM = 128
KV_GROUP = N_HEADS // N_KV_HEADS
MOBA_BLOCK = 256
MOBA_TOPK = 3
REL_BUCKETS = 32
REL_MAX_DIST = 128
RMS_EPS = 1e-6
NEG_INF = -1e30
LOG2E = math.log2(math.e)
GELU_C1 = 2.0 * math.sqrt(2.0 / math.pi)
GELU_C2 = GELU_C1 * 0.044715

SUBLANES = 8
LANES = 128
VMEM_LIMIT = 56 * 1024 * 1024

TM_LRU = 256
TM_FFN = 512
FF_CHUNK = 256
MOD_TN = 2048
GROUP = 4
NEAR_FAR = GROUP - 2


def _cparams(n_axes):
    return pltpu.CompilerParams(dimension_semantics=("arbitrary",) * n_axes,
                                vmem_limit_bytes=VMEM_LIMIT)


def _const_spec(shape):
    nd = len(shape)
    return pl.BlockSpec(shape, lambda *_: (0,) * nd, pipeline_mode=pl.Buffered(1))


def _dot(a, b):
    return jnp.dot(a, b, preferred_element_type=F32)


def _dot_nt(a, b, precision=None):
    return lax.dot_general(a, b, (((1,), (1,)), ((), ())), preferred_element_type=F32,
                           precision=precision)


def _norm_mod(xt, gain, shift, scale):
    inv = lax.rsqrt(jnp.mean(xt * xt, axis=-1, keepdims=True) + RMS_EPS)
    return (xt * inv) * (gain * (1.0 + scale)) + shift


def _mod_kernel(c_ref, w_ref, b_ref, o_ref):
    c = c_ref[...]
    cs = c * jax.nn.sigmoid(c)
    o_ref[0] = jnp.dot(cs, w_ref[0], preferred_element_type=F32, precision=HIGHEST) + b_ref[0]


def _adaln_mod(c_pad, w, b):
    n_layers, d, n = w.shape
    tn = min(MOD_TN, n)
    return pl.pallas_call(
        _mod_kernel,
        grid=(n_layers, n // tn),
        in_specs=[pl.BlockSpec((SUBLANES, d), lambda l, j: (0, 0)),
                  pl.BlockSpec((1, d, tn), lambda l, j: (l, 0, j)),
                  pl.BlockSpec((1, 1, tn), lambda l, j: (l, 0, j))],
        out_specs=pl.BlockSpec((1, SUBLANES, tn), lambda l, j: (l, 0, j)),
        out_shape=jax.ShapeDtypeStruct((n_layers, SUBLANES, n), F32),
        compiler_params=_cparams(2),
        name="adaln_mod",
    )(c_pad, w, b.reshape(n_layers, 1, n))


def _t5_thresholds():
    max_exact = REL_BUCKETS // 2
    d = np.arange(0, 4 * REL_MAX_DIST)
    dd = np.maximum(d, 1).astype(np.float32)
    large = max_exact + (np.log(dd / max_exact) / math.log(REL_MAX_DIST / max_exact)
                         * (REL_BUCKETS - max_exact)).astype(np.int32)
    large = np.minimum(large, REL_BUCKETS - 1)
    bucket = np.where(d < max_exact, d, large)
    return [int(np.argmax(bucket >= b)) for b in range(REL_BUCKETS)]


def _t5_kernel(rb_ref, o_ref, *, thr):
    h = pl.program_id(0)
    shape = (2 * MOBA_BLOCK, MOBA_BLOCK)
    key = lax.broadcasted_iota(jnp.int32, shape, 0)
    qry = lax.broadcasted_iota(jnp.int32, shape, 1)
    dist = qry + MOBA_BLOCK - key
    val = jnp.full(shape, rb_ref[h, REL_BUCKETS - 1], F32)
    for b in range(REL_BUCKETS - 2, -1, -1):
        val = jnp.where(dist < thr[b + 1], rb_ref[h, b], val)
    o_ref[0] = jnp.where(dist < 0, NEG_INF, LOG2E * val)


def _t5_table(rel_bias):
    return pl.pallas_call(
        functools.partial(_t5_kernel, thr=_t5_thresholds()),
        grid=(N_HEADS,),
        in_specs=[pl.BlockSpec(memory_space=pltpu.SMEM)],
        out_specs=pl.BlockSpec((1, 2 * MOBA_BLOCK, MOBA_BLOCK), lambda h: (h, 0, 0)),
        out_shape=jax.ShapeDtypeStruct((N_HEADS, 2 * MOBA_BLOCK, MOBA_BLOCK), F32),
        compiler_params=_cparams(1),
        name="t5_table",
    )(rel_bias.astype(F32))


def _time_permutation(tm):
    seg = tm // SUBLANES
    row = np.arange(tm)
    perm = np.zeros((tm, tm), np.float32)
    perm[row, (row % SUBLANES) * seg + row // SUBLANES] = 1.0
    return jnp.asarray(perm, BF16), jnp.asarray(perm.T, BF16)


def _vrow(v, g):
    return v[g * SUBLANES:(g + 1) * SUBLANES]


def _sublane_scan(a, b, h0):
    sub = lax.broadcasted_iota(jnp.int32, a.shape, 0)
    s = 1
    while s < SUBLANES:
        a_sh = jnp.where(sub < s, 1.0, pltpu.roll(a, s, 0))
        b_sh = jnp.where(sub < s, 0.0, pltpu.roll(b, s, 0))
        b = a * b_sh + b
        a = a * a_sh
        s *= 2
    return a * h0 + b


def _rglru_kernel(x_ref, mod_ref, g_ref, perm_ref, permt_ref, win_ref, cw_ref, cb_ref, wg_ref, bg_ref,
                  lam_ref, wout_ref, o_ref, conv_scr, h_scr):
    nbatch = x_ref.shape[0]
    d = x_ref.shape[-1]
    w = lam_ref.shape[-1]
    tm = x_ref.shape[1]
    seg = tm // SUBLANES
    halo = CONV_WIDTH - 1

    @pl.when(pl.program_id(0) == 0)
    def _():
        conv_scr[...] = jnp.zeros_like(conv_scr)
        h_scr[...] = jnp.zeros_like(h_scr)

    sub = lax.broadcasted_iota(jnp.int32, (SUBLANES, LRU_BLOCK), 0)
    blocks = [(hd * LRU_BLOCK, (hd + 1) * LRU_BLOCK) for hd in range(w // LRU_BLOCK)]
    streams = [(bi, hd) for bi in range(nbatch) for hd in range(len(blocks))]
    hp, uy, ux, xc, gates, hy = {}, {}, {}, {}, {}, {}

    def normalize(bi):
        mod = mod_ref[bi]
        h = _norm_mod(x_ref[bi], g_ref[...], mod[:, 0:d], mod[:, d:2 * d]).astype(BF16)
        hp[bi] = _dot(perm_ref[...], h).astype(BF16)

    def in_proj(bi, hd):
        lo, hi = blocks[hd]
        uy[bi, hd] = _dot(hp[bi], win_ref[:, lo:hi])
        ux[bi, hd] = _dot(hp[bi], win_ref[:, w + lo:w + hi])

    def conv(bi, hd):
        lo, hi = blocks[hd]
        u = ux[bi, hd]
        tops = []
        for i in range(halo):
            cur = pltpu.roll(_vrow(u, seg - halo + i), 1, 0)
            prv = pltpu.roll(conv_scr[bi, i, :, lo:hi], 1, 0)
            tops.append(jnp.where(sub == 0, prv, cur))
            conv_scr[bi, i, :, lo:hi] = _vrow(u, seg - halo + i)
        cw = cw_ref[:, lo:hi]
        v = u * cw[CONV_WIDTH - 1:CONV_WIDTH]
        for k in range(1, CONV_WIDTH):
            shifted = jnp.concatenate(tops[halo - k:] + [u[:tm - k * SUBLANES]], axis=0)
            v = v + shifted * cw[CONV_WIDTH - 1 - k:CONV_WIDTH - k]
        xc[bi, hd] = v + cb_ref[:, lo:hi]

    def gate_proj(bi, hd):
        gates[bi, hd] = _dot(xc[bi, hd].astype(BF16), wg_ref[hd])

    def recur(bi, hd):
        lo, hi = blocks[hd]
        u, v, gt = uy[bi, hd], xc[bi, hd], gates[bi, hd]
        y = u * jax.nn.sigmoid(u * (GELU_C1 + GELU_C2 * (u * u)))
        r = jax.nn.sigmoid(gt[:, :LRU_BLOCK] + bg_ref[0:1, lo:hi])
        i_gate = jax.nn.sigmoid(gt[:, LRU_BLOCK:] + bg_ref[1:2, lo:hi])
        lam = lam_ref[:, lo:hi]
        softplus_neg = jnp.maximum(-lam, 0.0) + jnp.log1p(jnp.exp(-jnp.abs(lam)))
        neg_log_a = r * (LRU_C * softplus_neg)
        a = jnp.exp(-neg_log_a)
        one_minus_a2 = (1.0 + a * a) * jnp.tanh(neg_log_a)
        mult = jnp.where(one_minus_a2 == 0.0, 0.0, one_minus_a2 * lax.rsqrt(one_minus_a2))
        b = mult * (i_gate * v)
        loc = [_vrow(b, 0)]
        prod = [_vrow(a, 0)]
        for g in range(1, seg):
            ag = _vrow(a, g)
            loc.append(ag * loc[-1] + _vrow(b, g))
            prod.append(ag * prod[-1])
        h_in = h_scr[bi, 0:1, lo:hi]
        seg_end = _sublane_scan(prod[-1], loc[-1], h_in)
        seg_in = jnp.where(sub == 0, h_in, pltpu.roll(seg_end, 1, 0))
        h_scr[bi, 0:1, lo:hi] = seg_end[SUBLANES - 1:SUBLANES]
        hs = jnp.concatenate([loc[g] + prod[g] * seg_in for g in range(seg)], axis=0)
        hy[bi, hd] = (hs * y).astype(BF16)

    def out_proj(bi):
        hy_seg = jnp.concatenate([hy[bi, hd] for hd in range(len(blocks))], axis=1)
        hy_time = _dot(permt_ref[...], hy_seg).astype(BF16)
        o_ref[bi] = x_ref[bi] + mod_ref[bi][:, 2 * d:3 * d] * _dot(hy_time, wout_ref[...])

    stages = (in_proj, conv, gate_proj, recur)
    normalize(0)
    for slot in range(len(streams) + len(stages) - 1):
        for depth, stage in enumerate(stages):
            i = slot - depth
            if 0 <= i < len(streams):
                bi, hd = streams[i]
                if stage is in_proj and hd == len(blocks) - 2 and bi + 1 < nbatch:
                    normalize(bi + 1)
                stage(bi, hd)
                if stage is recur and hd == len(blocks) - 1:
                    out_proj(bi)


def _rglru_layer(x, mod, gain, w_in, conv_w, conv_b, w_gates, b_gates, lam, w_out):
    b, s, d = x.shape
    w = lam.shape[-1]
    tm = TM_LRU
    perm, perm_t = _time_permutation(tm)
    return pl.pallas_call(
        _rglru_kernel,
        grid=(s // tm,),
        in_specs=[pl.BlockSpec((b, tm, d), lambda t: (0, t, 0)),
                  _const_spec(mod.shape),
                  _const_spec((1, d)),
                  _const_spec((tm, tm)),
                  _const_spec((tm, tm)),
                  _const_spec(w_in.shape),
                  _const_spec(conv_w.shape),
                  _const_spec((1, w)),
                  _const_spec(w_gates.shape),
                  _const_spec(b_gates.shape),
                  _const_spec((1, w)),
                  _const_spec(w_out.shape)],
        out_specs=pl.BlockSpec((b, tm, d), lambda t: (0, t, 0)),
        out_shape=jax.ShapeDtypeStruct(x.shape, F32),
        scratch_shapes=[pltpu.VMEM((b, CONV_WIDTH - 1, SUBLANES, w), F32),
                        pltpu.VMEM((b, SUBLANES, w), F32)],
        compiler_params=_cparams(1),
        name="rglru",
    )(x, mod, gain.reshape(1, d), perm, perm_t, w_in, conv_w, conv_b.reshape(1, w), w_gates, b_gates,
      lam.reshape(1, w), w_out)


def _ffn_kernel(*refs, has_attn, final):
    x_ref, mod_ref, g_ref, wg_ref, wu_ref, wd_ref = refs[:6]
    rest = list(refs[6:])
    att_ref = wo_ref = fn_ref = None
    if has_attn:
        att_ref, wo_ref = rest[:2]
        rest = rest[2:]
    if final:
        fn_ref = rest[0]
        rest = rest[1:]
    o_ref, = rest
    d = x_ref.shape[-1]
    ff = wg_ref.shape[-1]
    tm = x_ref.shape[1]

    xt = x_ref[0]
    mod = mod_ref[0]
    if has_attn:
        xt = xt + mod[:, 2 * d:3 * d] * _dot(att_ref[0], wo_ref[...])
    h = _norm_mod(xt, g_ref[...], mod[:, 3 * d:4 * d], mod[:, 4 * d:5 * d]).astype(BF16)
    acc = jnp.zeros((tm, d), F32)
    for c in range(ff // FF_CHUNK):
        lo = c * FF_CHUNK
        hi = lo + FF_CHUNK
        gate = _dot(h, wg_ref[:, lo:hi])
        up = _dot(h, wu_ref[:, lo:hi])
        act = (gate * jax.nn.sigmoid(gate)) * up
        acc = acc + _dot(act.astype(BF16), wd_ref[lo:hi, :])
    y = xt + mod[:, 5 * d:6 * d] * acc
    if final:
        y = y * lax.rsqrt(jnp.mean(y * y, axis=-1, keepdims=True) + RMS_EPS) * fn_ref[...]
    o_ref[0] = y


def _ffn_layer(x, mod, gain, w_gate, w_up, w_down, attn=None, w_o=None, final_gain=None):
    b, s, d = x.shape
    tm = TM_FFN
    has_attn = attn is not None
    final = final_gain is not None
    in_specs = [pl.BlockSpec((1, tm, d), lambda i, t: (i, t, 0)),
                pl.BlockSpec((1, 1, mod.shape[-1]), lambda i, t: (i, 0, 0)),
                _const_spec((1, d)),
                _const_spec(w_gate.shape),
                _const_spec(w_up.shape),
                _const_spec(w_down.shape)]
    args = [x, mod, gain.reshape(1, d), w_gate, w_up, w_down]
    if has_attn:
        in_specs += [pl.BlockSpec((1, tm, d), lambda i, t: (i, t, 0)), _const_spec(w_o.shape)]
        args += [attn, w_o]
    if final:
        in_specs += [_const_spec((1, d))]
        args += [final_gain.reshape(1, d)]
    return pl.pallas_call(
        functools.partial(_ffn_kernel, has_attn=has_attn, final=final),
        grid=(b, s // tm),
        in_specs=in_specs,
        out_specs=pl.BlockSpec((1, tm, d), lambda i, t: (i, t, 0)),
        out_shape=jax.ShapeDtypeStruct(x.shape, F32),
        compiler_params=_cparams(2),
        name="ffn_attn_final" if has_attn else "ffn",
    )(*args)


def _qkv_kernel(x_ref, kvmod_ref, mod_ref, kvg_ref, qg_ref, wk_ref, wvt_ref, wq_ref,
                q_ref, k_ref, vt_ref, mask_ref, km_scr):
    d = x_ref.shape[-1]
    tm = x_ref.shape[1]
    nb = mask_ref.shape[3]
    j = pl.program_id(1)

    @pl.when(j == 0)
    def _():
        km_scr[...] = jnp.zeros_like(km_scr)

    xt = x_ref[0]
    inv = lax.rsqrt(jnp.mean(xt * xt, axis=-1, keepdims=True) + RMS_EPS)
    xn = xt * inv
    kvmod = kvmod_ref[0]
    mod = mod_ref[0]
    h_kv = ((xn * kvg_ref[...]) * (1.0 + kvmod[:, d:2 * d]) + kvmod[:, 0:d]).astype(BF16)
    h_q = ((xn * qg_ref[...]) * (1.0 + mod[:, d:2 * d]) + mod[:, 0:d]).astype(BF16)
    k = _dot(h_kv, wk_ref[...])
    vt = _dot_nt(wvt_ref[...], h_kv)
    q = _dot(h_q, wq_ref[...]) * (HEAD_DIM ** -0.5 * LOG2E)
    q_ref[0] = q.astype(BF16)
    for hk in range(N_KV_HEADS):
        lo = hk * HEAD_DIM
        k_ref[0, hk, 0] = k[:, lo:lo + HEAD_DIM].astype(BF16)
        vt_ref[0, hk, 0] = vt[lo:lo + HEAD_DIM, :].astype(BF16)

    blk = lax.broadcasted_iota(jnp.int32, (nb, tm), 0).astype(F32)
    past = blk < j.astype(F32)
    for hq in range(N_HEADS):
        hk = hq // KV_GROUP
        km_h = km_scr[:, hk * HEAD_DIM:(hk + 1) * HEAD_DIM]
        gate = _dot_nt(km_h, q[:, hq * HEAD_DIM:(hq + 1) * HEAD_DIM], precision=HIGHEST)
        gate = jnp.where(past, gate, NEG_INF)
        mask = jnp.full((nb, tm), NEG_INF, F32)
        for _ in range(MOBA_TOPK):
            best = jnp.max(gate, axis=0, keepdims=True)
            first = jnp.min(jnp.where(gate == best, blk, nb), axis=0, keepdims=True)
            hit = blk == first
            mask = jnp.where(hit, 0.0, mask)
            gate = jnp.where(hit, -jnp.inf, gate)
        mask = jnp.where(past, mask, NEG_INF)
        lo = (hq % KV_GROUP) * tm
        mask_ref[0, hk, 0, :, lo:lo + tm] = mask

    km_scr[pl.ds(j, 1), :] = jnp.mean(k, axis=0, keepdims=True)


def _qkv_gate(x, kvmod, mod, kv_gain, q_gain, w_k, w_vt, w_q):
    b, s, d = x.shape
    tm = MOBA_BLOCK
    nb = s // tm
    dkv = N_KV_HEADS * HEAD_DIM
    dq = N_HEADS * HEAD_DIM
    out_shape = (jax.ShapeDtypeStruct((b, s, dq), BF16),
                 jax.ShapeDtypeStruct((b, N_KV_HEADS, nb, tm, HEAD_DIM), BF16),
                 jax.ShapeDtypeStruct((b, N_KV_HEADS, nb, HEAD_DIM, tm), BF16),
                 jax.ShapeDtypeStruct((b, N_KV_HEADS, nb, nb, KV_GROUP * tm), F32))
    out_specs = (pl.BlockSpec((1, tm, dq), lambda i, j: (i, j, 0)),
                 pl.BlockSpec((1, N_KV_HEADS, 1, tm, HEAD_DIM), lambda i, j: (i, 0, j, 0, 0)),
                 pl.BlockSpec((1, N_KV_HEADS, 1, HEAD_DIM, tm), lambda i, j: (i, 0, j, 0, 0)),
                 pl.BlockSpec((1, N_KV_HEADS, 1, nb, KV_GROUP * tm), lambda i, j: (i, 0, j, 0, 0)))
    return pl.pallas_call(
        _qkv_kernel,
        grid=(b, nb),
        in_specs=[pl.BlockSpec((1, tm, d), lambda i, j: (i, j, 0)),
                  pl.BlockSpec((1, 1, kvmod.shape[-1]), lambda i, j: (i, 0, 0)),
                  pl.BlockSpec((1, 1, mod.shape[-1]), lambda i, j: (i, 0, 0)),
                  _const_spec((1, d)),
                  _const_spec((1, d)),
                  _const_spec(w_k.shape),
                  _const_spec(w_vt.shape),
                  _const_spec(w_q.shape)],
        out_specs=out_specs,
        out_shape=out_shape,
        scratch_shapes=[pltpu.VMEM((nb, dkv), F32)],
        compiler_params=_cparams(2),
        name="qkv_gate",
    )(x, kvmod, mod, kv_gain.reshape(1, d), q_gain.reshape(1, d), w_k, w_vt, w_q)


def _attn_kernel(rb_ref, q_ref, k_ref, vt_ref, mask_ref, tab_ref, o_ref, m_scr, l_scr, acc_scr,
                 sa_scr, sb_scr, cma_scr, cmb_scr, rowa_scr, rowb_scr):
    g = pl.program_id(1)
    j = pl.program_id(2)
    nb = k_ref.shape[2]
    tq = q_ref.shape[1]
    q2 = q_ref[0]
    qs = jnp.concatenate([q2[:, :HEAD_DIM], q2[:, HEAD_DIM:]], axis=0)

    def scores(n):
        return _dot_nt(k_ref[0, 0, n], qs)

    def tile_bias(lo):
        return jnp.concatenate([tab_ref[0, lo:lo + MOBA_BLOCK, :], tab_ref[1, lo:lo + MOBA_BLOCK, :]], axis=1)

    def mask_row(n):
        return mask_ref[0, 0, 0, pl.ds(n, 1), :]

    def col_max(s):
        return jnp.max(s, axis=0, keepdims=True)

    def col_sum(s):
        return jnp.sum(s, axis=0, keepdims=True)

    lane = lax.broadcasted_iota(jnp.int32, (1, KV_GROUP * tq), 1)
    far_bias = LOG2E * jnp.where(lane < tq, rb_ref[KV_GROUP * g, REL_BUCKETS - 1],
                                 rb_ref[KV_GROUP * g + 1, REL_BUCKETS - 1])

    def far_row(n, valid):
        return jnp.where(valid, mask_row(n) + far_bias, NEG_INF)

    n_far = jnp.maximum(j - 1 - NEAR_FAR, 0)
    n_groups = 1 + (n_far + GROUP - 1) // GROUP
    jp = jnp.maximum(j - 1, 0)
    head_blocks = [j, jp] + [jnp.maximum(j - 2 - u, 0) for u in range(NEAR_FAR)]

    def group_blocks(k):
        return [jnp.where(k == 0, head_blocks[u], jnp.minimum((k - 1) * GROUP + u, nb - 1))
                for u in range(GROUP)]

    def put(buf, u, s, row):
        s_buf, cm_buf, row_buf = buf
        s_buf[u] = s
        cm_buf[u:u + 1] = col_max(s)
        row_buf[u:u + 1] = row

    def stage1_head(buf):
        put(buf, 0, scores(j) + tile_bias(MOBA_BLOCK), jnp.zeros((1, KV_GROUP * tq), F32))
        put(buf, 1, scores(jp) + tile_bias(0), jnp.where(j >= 1, mask_row(jp), NEG_INF))
        for u in range(NEAR_FAR):
            n = head_blocks[2 + u]
            put(buf, 2 + u, scores(n), far_row(n, j - 2 - u >= 0))

    def step(k, buf, nxt):
        s_buf, cm_buf, row_buf = buf
        blocks = group_blocks(k)
        m_old = m_scr[0:1]
        rows = [row_buf[u:u + 1] for u in range(GROUP)]
        m_new = m_old
        for u in range(GROUP):
            m_new = jnp.maximum(m_new, cm_buf[u:u + 1] + rows[u])
        alpha = jnp.exp2(m_old - m_new)
        l_new = alpha * l_scr[0:1]
        pv = None
        for u in range(GROUP):
            n = k * GROUP + u
            nc = jnp.minimum(n, nb - 1)
            put(nxt, u, scores(nc), far_row(nc, n < n_far))
            p = jnp.exp2(s_buf[u] + (rows[u] - m_new))
            l_new = l_new + col_sum(p)
            d = _dot(vt_ref[0, 0, blocks[u]], p.astype(BF16))
            pv = d if pv is None else pv + d
        acc_scr[...] = alpha * acc_scr[...] + pv
        l_scr[0:1] = l_new
        m_scr[0:1] = m_new

    buf_a = (sa_scr, cma_scr, rowa_scr)
    buf_b = (sb_scr, cmb_scr, rowb_scr)
    m_scr[...] = jnp.full_like(m_scr, NEG_INF)
    l_scr[...] = jnp.zeros_like(l_scr)
    acc_scr[...] = jnp.zeros_like(acc_scr)
    stage1_head(buf_a)

    def pair(t, carry):
        k = 2 * t
        step(k, buf_a, buf_b)

        @pl.when(k + 1 < n_groups)
        def _():
            step(k + 1, buf_b, buf_a)

        return carry

    lax.fori_loop(0, (n_groups + 1) // 2, pair, 0)

    o = acc_scr[...] * (1.0 / l_scr[0:1])
    o_ref[0] = jnp.concatenate([o[:, :tq].T, o[:, tq:].T], axis=1).astype(BF16)


def _moba_attn(rel_bias, q, k, vt, mask, table):
    b, s, dq = q.shape
    nb = k.shape[2]
    tq = MOBA_BLOCK
    gw = KV_GROUP * HEAD_DIM
    return pl.pallas_call(
        _attn_kernel,
        grid=(b, N_KV_HEADS, nb),
        in_specs=[pl.BlockSpec(memory_space=pltpu.SMEM),
                  pl.BlockSpec((1, tq, gw), lambda i, g, j: (i, j, g)),
                  pl.BlockSpec((1, 1, nb, tq, HEAD_DIM), lambda i, g, j: (i, g, 0, 0, 0)),
                  pl.BlockSpec((1, 1, nb, HEAD_DIM, tq), lambda i, g, j: (i, g, 0, 0, 0)),
                  pl.BlockSpec((1, 1, 1, nb, KV_GROUP * tq), lambda i, g, j: (i, g, j, 0, 0)),
                  pl.BlockSpec((KV_GROUP, 2 * MOBA_BLOCK, MOBA_BLOCK), lambda i, g, j: (g, 0, 0))],
        out_specs=pl.BlockSpec((1, tq, gw), lambda i, g, j: (i, j, g)),
        out_shape=jax.ShapeDtypeStruct((b, s, dq), BF16),
        scratch_shapes=[pltpu.VMEM((SUBLANES, KV_GROUP * tq), F32),
                        pltpu.VMEM((SUBLANES, KV_GROUP * tq), F32),
                        pltpu.VMEM((HEAD_DIM, KV_GROUP * tq), F32),
                        pltpu.VMEM((GROUP, MOBA_BLOCK, KV_GROUP * tq), F32),
                        pltpu.VMEM((GROUP, MOBA_BLOCK, KV_GROUP * tq), F32),
                        pltpu.VMEM((SUBLANES, KV_GROUP * tq), F32),
                        pltpu.VMEM((SUBLANES, KV_GROUP * tq), F32),
                        pltpu.VMEM((SUBLANES, KV_GROUP * tq), F32),
                        pltpu.VMEM((SUBLANES, KV_GROUP * tq), F32)],
        compiler_params=_cparams(3),
        name="moba_attn",
    )(rel_bias.astype(F32), q, k, vt, mask, table)


def kernel(x, c, mod_w, mod_b, norm_mix, norm_ffn, lru_w_in, lru_conv_w, lru_conv_b, lru_w_gates,
           lru_b_gates, lru_lambda, lru_w_out, kv_mod_w, kv_mod_b, kv_norm, w_kv, attn_w_q, attn_w_o,
           rel_bias, ffn_w_gate, ffn_w_up, ffn_w_down, final_norm):
    b, s, d = x.shape
    assert s % TM_FFN == 0 and s % MOBA_BLOCK == 0 and b <= SUBLANES
    dkv = N_KV_HEADS * HEAD_DIM

    c_pad = jnp.zeros((SUBLANES, d), F32).at[:b].set(c.astype(F32))
    mod = _adaln_mod(c_pad, mod_w, mod_b)[:, :b]
    kvmod = _adaln_mod(c_pad, kv_mod_w[None], kv_mod_b[None])[0, :b]
    mod0 = mod[0][:, None, :]
    mod1 = mod[1][:, None, :]
    kvmod = kvmod[:, None, :]

    x = _rglru_layer(x, mod0, norm_mix[0], lru_w_in[0].astype(BF16), lru_conv_w[0], lru_conv_b[0],
                     lru_w_gates[0].astype(BF16), lru_b_gates[0], lru_lambda[0], lru_w_out[0].astype(BF16))
    x = _ffn_layer(x, mod0, norm_ffn[0], ffn_w_gate[0].astype(BF16), ffn_w_up[0].astype(BF16),
                   ffn_w_down[0].astype(BF16))

    q, k, vt, mask = _qkv_gate(x, kvmod, mod1, kv_norm, norm_mix[1], w_kv[:, :dkv].astype(BF16),
                               w_kv[:, dkv:].T.astype(BF16), attn_w_q[0].astype(BF16))
    table = _t5_table(rel_bias)
    attn = _moba_attn(rel_bias, q, k, vt, mask, table)
    return _ffn_layer(x, mod1, norm_ffn[1], ffn_w_gate[1].astype(BF16), ffn_w_up[1].astype(BF16),
                      ffn_w_down[1].astype(BF16), attn=attn, w_o=attn_w_o[0].astype(BF16),
                      final_gain=final_norm)
```

```python
import functools
import math

import numpy as np
import jax
import jax.numpy as jnp
from jax import lax
from jax.experimental import pallas as pl
from jax.experimental.pallas import tpu as pltpu

F32 = jnp.float32
BF16 = jnp.bfloat16
HIGHEST = lax.Precision.HIGHEST

LRU_BLOCK = 256
CONV_WIDTH = 4
LRU_C = 8.0
N_HEADS = 8
N_KV_HEADS = 4
HEAD_DIM = 128
KV_GROUP = N_HEADS // N_KV_HEADS
MOBA_BLOCK = 256
MOBA_TOPK = 3
REL_BUCKETS = 32
REL_MAX_DIST = 128
RMS_EPS = 1e-6
NEG_INF = -1e30
LOG2E = math.log2(math.e)
GELU_C1 = 2.0 * math.sqrt(2.0 / math.pi)
GELU_C2 = GELU_C1 * 0.044715

SUBLANES = 8
LANES = 128
VMEM_LIMIT = 56 * 1024 * 1024

TM_LRU = 256
TM_FFN = 512
FF_CHUNK = 256
MOD_TN = 2048
GROUP = 4
NEAR_FAR = GROUP - 2


def _cparams(n_axes):
    return pltpu.CompilerParams(dimension_semantics=("arbitrary",) * n_axes,
                                vmem_limit_bytes=VMEM_LIMIT)


def _const_spec(shape):
    nd = len(shape)
    return pl.BlockSpec(shape, lambda *_: (0,) * nd, pipeline_mode=pl.Buffered(1))


def _dot(a, b):
    return jnp.dot(a, b, preferred_element_type=F32)


def _dot_nt(a, b, precision=None):
    return lax.dot_general(a, b, (((1,), (1,)), ((), ())), preferred_element_type=F32,
                           precision=precision)


def _norm_mod(xt, gain, shift, scale):
    inv = lax.rsqrt(jnp.mean(xt * xt, axis=-1, keepdims=True) + RMS_EPS)
    return (xt * inv) * (gain * (1.0 + scale)) + shift


def _mod_kernel(c_ref, w_ref, b_ref, o_ref):
    c = c_ref[...]
    cs = c * jax.nn.sigmoid(c)
    o_ref[0] = jnp.dot(cs, w_ref[0], preferred_element_type=F32, precision=HIGHEST) + b_ref[0]


def _adaln_mod(c_pad, w, b):
    n_layers, d, n = w.shape
    tn = min(MOD_TN, n)
    return pl.pallas_call(
        _mod_kernel,
        grid=(n_layers, n // tn),
        in_specs=[pl.BlockSpec((SUBLANES, d), lambda l, j: (0, 0)),
                  pl.BlockSpec((1, d, tn), lambda l, j: (l, 0, j)),
                  pl.BlockSpec((1, 1, tn), lambda l, j: (l, 0, j))],
        out_specs=pl.BlockSpec((1, SUBLANES, tn), lambda l, j: (l, 0, j)),
        out_shape=jax.ShapeDtypeStruct((n_layers, SUBLANES, n), F32),
        compiler_params=_cparams(2),
        name="adaln_mod",
    )(c_pad, w, b.reshape(n_layers, 1, n))


def _t5_thresholds():
    max_exact = REL_BUCKETS // 2
    d = np.arange(0, 4 * REL_MAX_DIST)
    dd = np.maximum(d, 1).astype(np.float32)
    large = max_exact + (np.log(dd / max_exact) / math.log(REL_MAX_DIST / max_exact)
                         * (REL_BUCKETS - max_exact)).astype(np.int32)
    large = np.minimum(large, REL_BUCKETS - 1)
    bucket = np.where(d < max_exact, d, large)
    return [int(np.argmax(bucket >= b)) for b in range(REL_BUCKETS)]


def _t5_kernel(rb_ref, o_ref, *, thr):
    h = pl.program_id(0)
    shape = (2 * MOBA_BLOCK, MOBA_BLOCK)
    key = lax.broadcasted_iota(jnp.int32, shape, 0)
    qry = lax.broadcasted_iota(jnp.int32, shape, 1)
    dist = qry + MOBA_BLOCK - key
    val = jnp.full(shape, rb_ref[h, REL_BUCKETS - 1], F32)
    for b in range(REL_BUCKETS - 2, -1, -1):
        val = jnp.where(dist < thr[b + 1], rb_ref[h, b], val)
    o_ref[0] = jnp.where(dist < 0, NEG_INF, LOG2E * val)


def _t5_table(rel_bias):
    return pl.pallas_call(
        functools.partial(_t5_kernel, thr=_t5_thresholds()),
        grid=(N_HEADS,),
        in_specs=[pl.BlockSpec(memory_space=pltpu.SMEM)],
        out_specs=pl.BlockSpec((1, 2 * MOBA_BLOCK, MOBA_BLOCK), lambda h: (h, 0, 0)),
        out_shape=jax.ShapeDtypeStruct((N_HEADS, 2 * MOBA_BLOCK, MOBA_BLOCK), F32),
        compiler_params=_cparams(1),
        name="t5_table",
    )(rel_bias.astype(F32))


def _time_permutation(tm):
    seg = tm // SUBLANES
    row = np.arange(tm)
    perm = np.zeros((tm, tm), np.float32)
    perm[row, (row % SUBLANES) * seg + row // SUBLANES] = 1.0
    return jnp.asarray(perm, BF16), jnp.asarray(perm.T, BF16)


def _vrow(v, g):
    return v[g * SUBLANES:(g + 1) * SUBLANES]


def _sublane_scan(a, b, h0):
    sub = lax.broadcasted_iota(jnp.int32, a.shape, 0)
    s = 1
    while s < SUBLANES:
        a_sh = jnp.where(sub < s, 1.0, pltpu.roll(a, s, 0))
        b_sh = jnp.where(sub < s, 0.0, pltpu.roll(b, s, 0))
        b = a * b_sh + b
        a = a * a_sh
        s *= 2
    return a * h0 + b


def _rglru_kernel(x_ref, mod_ref, g_ref, perm_ref, permt_ref, win_ref, cw_ref, cb_ref, wg_ref, bg_ref,
                  lam_ref, wout_ref, o_ref, conv_scr, h_scr):
    nbatch = x_ref.shape[0]
    d = x_ref.shape[-1]
    w = lam_ref.shape[-1]
    tm = x_ref.shape[1]
    seg = tm // SUBLANES
    halo = CONV_WIDTH - 1

    @pl.when(pl.program_id(0) == 0)
    def _():
        conv_scr[...] = jnp.zeros_like(conv_scr)
        h_scr[...] = jnp.zeros_like(h_scr)

    sub = lax.broadcasted_iota(jnp.int32, (SUBLANES, LRU_BLOCK), 0)
    blocks = [(hd * LRU_BLOCK, (hd + 1) * LRU_BLOCK) for hd in range(w // LRU_BLOCK)]
    streams = [(bi, hd) for bi in range(nbatch) for hd in range(len(blocks))]
    hp, uy, ux, xc, gates, hy = {}, {}, {}, {}, {}, {}

    def normalize(bi):
        mod = mod_ref[bi]
        h = _norm_mod(x_ref[bi], g_ref[...], mod[:, 0:d], mod[:, d:2 * d]).astype(BF16)
        hp[bi] = _dot(perm_ref[...], h).astype(BF16)

    def in_proj(bi, hd):
        lo, hi = blocks[hd]
        uy[bi, hd] = _dot(hp[bi], win_ref[:, lo:hi])
        ux[bi, hd] = _dot(hp[bi], win_ref[:, w + lo:w + hi])

    def conv(bi, hd):
        lo, hi = blocks[hd]
        u = ux[bi, hd]
        tops = []
        for i in range(halo):
            cur = pltpu.roll(_vrow(u, seg - halo + i), 1, 0)
            prv = pltpu.roll(conv_scr[bi, i, :, lo:hi], 1, 0)
            tops.append(jnp.where(sub == 0, prv, cur))
            conv_scr[bi, i, :, lo:hi] = _vrow(u, seg - halo + i)
        cw = cw_ref[:, lo:hi]
        v = u * cw[CONV_WIDTH - 1:CONV_WIDTH]
        for k in range(1, CONV_WIDTH):
            shifted = jnp.concatenate(tops[halo - k:] + [u[:tm - k * SUBLANES]], axis=0)
            v = v + shifted * cw[CONV_WIDTH - 1 - k:CONV_WIDTH - k]
        xc[bi, hd] = v + cb_ref[:, lo:hi]

    def gate_proj(bi, hd):
        gates[bi, hd] = _dot(xc[bi, hd].astype(BF16), wg_ref[hd])

    def recur(bi, hd):
        lo, hi = blocks[hd]
        u, v, gt = uy[bi, hd], xc[bi, hd], gates[bi, hd]
        y = u * jax.nn.sigmoid(u * (GELU_C1 + GELU_C2 * (u * u)))
        r = jax.nn.sigmoid(gt[:, :LRU_BLOCK] + bg_ref[0:1, lo:hi])
        i_gate = jax.nn.sigmoid(gt[:, LRU_BLOCK:] + bg_ref[1:2, lo:hi])
        lam = lam_ref[:, lo:hi]
        softplus_neg = jnp.maximum(-lam, 0.0) + jnp.log1p(jnp.exp(-jnp.abs(lam)))
        neg_log_a = r * (LRU_C * softplus_neg)
        a = jnp.exp(-neg_log_a)
        one_minus_a2 = (1.0 + a * a) * jnp.tanh(neg_log_a)
        mult = jnp.where(one_minus_a2 == 0.0, 0.0, one_minus_a2 * lax.rsqrt(one_minus_a2))
        b = mult * (i_gate * v)
        loc = [_vrow(b, 0)]
        prod = [_vrow(a, 0)]
        for g in range(1, seg):
            ag = _vrow(a, g)
            loc.append(ag * loc[-1] + _vrow(b, g))
            prod.append(ag * prod[-1])
        h_in = h_scr[bi, 0:1, lo:hi]
        seg_end = _sublane_scan(prod[-1], loc[-1], h_in)
        seg_in = jnp.where(sub == 0, h_in, pltpu.roll(seg_end, 1, 0))
        h_scr[bi, 0:1, lo:hi] = seg_end[SUBLANES - 1:SUBLANES]
        hs = jnp.concatenate([loc[g] + prod[g] * seg_in for g in range(seg)], axis=0)
        hy[bi, hd] = (hs * y).astype(BF16)

    def out_proj(bi):
        hy_seg = jnp.concatenate([hy[bi, hd] for hd in range(len(blocks))], axis=1)
        hy_time = _dot(permt_ref[...], hy_seg).astype(BF16)
        o_ref[bi] = x_ref[bi] + mod_ref[bi][:, 2 * d:3 * d] * _dot(hy_time, wout_ref[...])

    stages = (in_proj, conv, gate_proj, recur)
    normalize(0)
    for slot in range(len(streams) + len(stages) - 1):
        for depth, stage in enumerate(stages):
            i = slot - depth
            if 0 <= i < len(streams):
                bi, hd = streams[i]
                if stage is in_proj and hd == len(blocks) - 2 and bi + 1 < nbatch:
                    normalize(bi + 1)
                stage(bi, hd)
                if stage is recur and hd == len(blocks) - 1:
                    out_proj(bi)


def _rglru_layer(x, mod, gain, w_in, conv_w, conv_b, w_gates, b_gates, lam, w_out):
    b, s, d = x.shape
    w = lam.shape[-1]
    tm = TM_LRU
    perm, perm_t = _time_permutation(tm)
    return pl.pallas_call(
        _rglru_kernel,
        grid=(s // tm,),
        in_specs=[pl.BlockSpec((b, tm, d), lambda t: (0, t, 0)),
                  _const_spec(mod.shape),
                  _const_spec((1, d)),
                  _const_spec((tm, tm)),
                  _const_spec((tm, tm)),
                  _const_spec(w_in.shape),
                  _const_spec(conv_w.shape),
                  _const_spec((1, w)),
                  _const_spec(w_gates.shape),
                  _const_spec(b_gates.shape),
                  _const_spec((1, w)),
                  _const_spec(w_out.shape)],
        out_specs=pl.BlockSpec((b, tm, d), lambda t: (0, t, 0)),
        out_shape=jax.ShapeDtypeStruct(x.shape, F32),
        scratch_shapes=[pltpu.VMEM((b, CONV_WIDTH - 1, SUBLANES, w), F32),
                        pltpu.VMEM((b, SUBLANES, w), F32)],
        compiler_params=_cparams(1),
        name="rglru",
    )(x, mod, gain.reshape(1, d), perm, perm_t, w_in, conv_w, conv_b.reshape(1, w), w_gates, b_gates,
      lam.reshape(1, w), w_out)


def _ffn_kernel(*refs, has_attn, final):
    x_ref, mod_ref, g_ref, wg_ref, wu_ref, wd_ref = refs[:6]
    rest = list(refs[6:])
    att_ref = wo_ref = fn_ref = None
    if has_attn:
        att_ref, wo_ref = rest[:2]
        rest = rest[2:]
    if final:
        fn_ref = rest[0]
        rest = rest[1:]
    o_ref, = rest
    d = x_ref.shape[-1]
    ff = wg_ref.shape[-1]
    tm = x_ref.shape[1]

    xt = x_ref[0]
    mod = mod_ref[0]
    if has_attn:
        xt = xt + mod[:, 2 * d:3 * d] * _dot(att_ref[0], wo_ref[...])
    h = _norm_mod(xt, g_ref[...], mod[:, 3 * d:4 * d], mod[:, 4 * d:5 * d]).astype(BF16)
    acc = jnp.zeros((tm, d), F32)
    for c in range(ff // FF_CHUNK):
        lo = c * FF_CHUNK
        hi = lo + FF_CHUNK
        gate = _dot(h, wg_ref[:, lo:hi])
        up = _dot(h, wu_ref[:, lo:hi])
        act = (gate * jax.nn.sigmoid(gate)) * up
        acc = acc + _dot(act.astype(BF16), wd_ref[lo:hi, :])
    y = xt + mod[:, 5 * d:6 * d] * acc
    if final:
        y = y * lax.rsqrt(jnp.mean(y * y, axis=-1, keepdims=True) + RMS_EPS) * fn_ref[...]
    o_ref[0] = y


def _ffn_layer(x, mod, gain, w_gate, w_up, w_down, attn=None, w_o=None, final_gain=None):
    b, s, d = x.shape
    tm = TM_FFN
    has_attn = attn is not None
    final = final_gain is not None
    in_specs = [pl.BlockSpec((1, tm, d), lambda i, t: (i, t, 0)),
                pl.BlockSpec((1, 1, mod.shape[-1]), lambda i, t: (i, 0, 0)),
                _const_spec((1, d)),
                _const_spec(w_gate.shape),
                _const_spec(w_up.shape),
                _const_spec(w_down.shape)]
    args = [x, mod, gain.reshape(1, d), w_gate, w_up, w_down]
    if has_attn:
        in_specs += [pl.BlockSpec((1, tm, d), lambda i, t: (i, t, 0)), _const_spec(w_o.shape)]
        args += [attn, w_o]
    if final:
        in_specs += [_const_spec((1, d))]
        args += [final_gain.reshape(1, d)]
    return pl.pallas_call(
        functools.partial(_ffn_kernel, has_attn=has_attn, final=final),
        grid=(b, s // tm),
        in_specs=in_specs,
        out_specs=pl.BlockSpec((1, tm, d), lambda i, t: (i, t, 0)),
        out_shape=jax.ShapeDtypeStruct(x.shape, F32),
        compiler_params=_cparams(2),
        name="ffn_attn_final" if has_attn else "ffn",
    )(*args)


def _qkv_kernel(x_ref, kvmod_ref, mod_ref, kvg_ref, qg_ref, wk_ref, wvt_ref, wq_ref,
                q_ref, k_ref, vt_ref, mask_ref, km_scr):
    d = x_ref.shape[-1]
    tm = x_ref.shape[1]
    nb = mask_ref.shape[3]
    j = pl.program_id(1)

    @pl.when(j == 0)
    def _():
        km_scr[...] = jnp.zeros_like(km_scr)

    xt = x_ref[0]
    inv = lax.rsqrt(jnp.mean(xt * xt, axis=-1, keepdims=True) + RMS_EPS)
    xn = xt * inv
    kvmod = kvmod_ref[0]
    mod = mod_ref[0]
    blk = lax.broadcasted_iota(jnp.int32, (nb, tm), 0).astype(F32)
    past = blk < j.astype(F32)

    def select_blocks(hq, q_h):
        hk = hq // KV_GROUP
        km_h = km_scr[:, hk * HEAD_DIM:(hk + 1) * HEAD_DIM]
        gate = _dot_nt(km_h, q_h, precision=HIGHEST)
        gate = jnp.where(past, gate, NEG_INF)
        mask = jnp.full((nb, tm), NEG_INF, F32)
        for _ in range(MOBA_TOPK):
            best = jnp.max(gate, axis=0, keepdims=True)
            first = jnp.min(jnp.where(gate == best, blk, nb), axis=0, keepdims=True)
            hit = blk == first
            mask = jnp.where(hit, 0.0, mask)
            gate = jnp.where(hit, -jnp.inf, gate)
        mask = jnp.where(past, mask, NEG_INF)
        lo = (hq % KV_GROUP) * tm
        mask_ref[0, hk, 0, :, lo:lo + tm] = mask

    half = N_HEADS // 2
    dq_half = half * HEAD_DIM
    q_scale = HEAD_DIM ** -0.5 * LOG2E
    h_q = (xn * (qg_ref[...] * (1.0 + mod[:, d:2 * d])) + mod[:, 0:d]).astype(BF16)
    q_lo = _dot(h_q, wq_ref[:, :dq_half]) * q_scale
    h_kv = (xn * (kvg_ref[...] * (1.0 + kvmod[:, d:2 * d])) + kvmod[:, 0:d]).astype(BF16)
    q_hi = _dot(h_q, wq_ref[:, dq_half:]) * q_scale
    q_ref[0, :, :dq_half] = q_lo.astype(BF16)
    for hq in range(half):
        select_blocks(hq, q_lo[:, hq * HEAD_DIM:(hq + 1) * HEAD_DIM])
    k = _dot(h_kv, wk_ref[...])
    q_ref[0, :, dq_half:] = q_hi.astype(BF16)
    for hq in range(half, N_HEADS):
        select_blocks(hq, q_hi[:, (hq - half) * HEAD_DIM:(hq - half + 1) * HEAD_DIM])
    vt = _dot_nt(wvt_ref[...], h_kv)
    for hk in range(N_KV_HEADS):
        lo = hk * HEAD_DIM
        k_ref[0, hk, 0] = k[:, lo:lo + HEAD_DIM].astype(BF16)
        vt_ref[0, hk, 0] = vt[lo:lo + HEAD_DIM, :].astype(BF16)

    km_scr[pl.ds(j, 1), :] = jnp.mean(k, axis=0, keepdims=True)


def _qkv_gate(x, kvmod, mod, kv_gain, q_gain, w_k, w_vt, w_q):
    b, s, d = x.shape
    tm = MOBA_BLOCK
    nb = s // tm
    dkv = N_KV_HEADS * HEAD_DIM
    dq = N_HEADS * HEAD_DIM
    out_shape = (jax.ShapeDtypeStruct((b, s, dq), BF16),
                 jax.ShapeDtypeStruct((b, N_KV_HEADS, nb, tm, HEAD_DIM), BF16),
                 jax.ShapeDtypeStruct((b, N_KV_HEADS, nb, HEAD_DIM, tm), BF16),
                 jax.ShapeDtypeStruct((b, N_KV_HEADS, nb, nb, KV_GROUP * tm), F32))
    out_specs = (pl.BlockSpec((1, tm, dq), lambda i, j: (i, j, 0)),
                 pl.BlockSpec((1, N_KV_HEADS, 1, tm, HEAD_DIM), lambda i, j: (i, 0, j, 0, 0)),
                 pl.BlockSpec((1, N_KV_HEADS, 1, HEAD_DIM, tm), lambda i, j: (i, 0, j, 0, 0)),
                 pl.BlockSpec((1, N_KV_HEADS, 1, nb, KV_GROUP * tm), lambda i, j: (i, 0, j, 0, 0)))
    return pl.pallas_call(
        _qkv_kernel,
        grid=(b, nb),
        in_specs=[pl.BlockSpec((1, tm, d), lambda i, j: (i, j, 0)),
                  pl.BlockSpec((1, 1, kvmod.shape[-1]), lambda i, j: (i, 0, 0)),
                  pl.BlockSpec((1, 1, mod.shape[-1]), lambda i, j: (i, 0, 0)),
                  _const_spec((1, d)),
                  _const_spec((1, d)),
                  _const_spec(w_k.shape),
                  _const_spec(w_vt.shape),
                  _const_spec(w_q.shape)],
        out_specs=out_specs,
        out_shape=out_shape,
        scratch_shapes=[pltpu.VMEM((nb, dkv), F32)],
        compiler_params=_cparams(2),
        name="qkv_gate",
    )(x, kvmod, mod, kv_gain.reshape(1, d), q_gain.reshape(1, d), w_k, w_vt, w_q)


def _attn_kernel(rb_ref, q_ref, qn_ref, k_ref, vt_ref, mask_ref, maskn_ref, tab_ref, o_ref, m_scr, l_scr,
                 acc_scr, sa_scr, sb_scr, cma_scr, cmb_scr, rowa_scr, rowb_scr, phase_scr):
    g = pl.program_id(1)
    j = pl.program_id(2)
    nb = k_ref.shape[2]
    tq = q_ref.shape[1]

    def stack_heads(q2):
        return jnp.concatenate([q2[:, :HEAD_DIM], q2[:, HEAD_DIM:]], axis=0)

    qs = stack_heads(q_ref[0])

    def scores(n, qx=qs):
        return _dot_nt(k_ref[0, 0, n], qx)

    def tile_bias(lo):
        return jnp.concatenate([tab_ref[0, lo:lo + MOBA_BLOCK, :], tab_ref[1, lo:lo + MOBA_BLOCK, :]], axis=1)

    def mask_row(n, mref=mask_ref):
        return mref[0, 0, 0, pl.ds(n, 1), :]

    def col_max(s):
        return jnp.max(s, axis=0, keepdims=True)

    def col_sum(s):
        return jnp.sum(s, axis=0, keepdims=True)

    lane = lax.broadcasted_iota(jnp.int32, (1, KV_GROUP * tq), 1)
    far_bias = LOG2E * jnp.where(lane < tq, rb_ref[KV_GROUP * g, REL_BUCKETS - 1],
                                 rb_ref[KV_GROUP * g + 1, REL_BUCKETS - 1])

    def far_row(n, valid, mref=mask_ref):
        return jnp.where(valid, mask_row(n, mref) + far_bias, NEG_INF)

    n_far = jnp.maximum(j - 1 - NEAR_FAR, 0)
    n_groups = 1 + (n_far + GROUP - 1) // GROUP
    jp = jnp.maximum(j - 1, 0)
    head_blocks = [j, jp] + [jnp.maximum(j - 2 - u, 0) for u in range(NEAR_FAR)]

    def group_blocks(k):
        return [jnp.where(k == 0, head_blocks[u], jnp.minimum((k - 1) * GROUP + u, nb - 1))
                for u in range(GROUP)]

    def put(buf, u, s, row):
        s_buf, cm_buf, row_buf = buf
        s_buf[u] = s
        cm_buf[u:u + 1] = col_max(s)
        row_buf[u:u + 1] = row

    def head_slot(u, jq, qx, mref):
        if u == 0:
            return scores(jq, qx) + tile_bias(MOBA_BLOCK), jnp.zeros((1, KV_GROUP * tq), F32)
        if u == 1:
            jqp = jnp.maximum(jq - 1, 0)
            return scores(jqp, qx) + tile_bias(0), jnp.where(jq >= 1, mask_row(jqp, mref), NEG_INF)
        n = jnp.maximum(jq - u, 0)
        return scores(n, qx), far_row(n, jq - u >= 0, mref)

    j_next = jnp.minimum(j + 1, nb - 1)

    def step(k, buf, nxt, last):
        s_buf, cm_buf, row_buf = buf
        blocks = group_blocks(k)
        m_old = m_scr[0:1]
        rows = [row_buf[u:u + 1] for u in range(GROUP)]
        m_new = m_old
        for u in range(GROUP):
            m_new = jnp.maximum(m_new, cm_buf[u:u + 1] + rows[u])
        alpha = jnp.exp2(m_old - m_new)
        l_new = alpha * l_scr[0:1]
        pv = None
        qs_next = stack_heads(qn_ref[0]) if last else None
        for u in range(GROUP):
            if last:
                put(nxt, u, *head_slot(u, j_next, qs_next, maskn_ref))
            else:
                n = k * GROUP + u
                nc = jnp.minimum(n, nb - 1)
                put(nxt, u, scores(nc), far_row(nc, n < n_far))
            p = jnp.exp2(s_buf[u] + (rows[u] - m_new))
            l_new = l_new + col_sum(p)
            d = _dot(vt_ref[0, 0, blocks[u]], p.astype(BF16))
            pv = d if pv is None else pv + d
        acc_scr[...] = alpha * acc_scr[...] + pv
        l_scr[0:1] = l_new
        m_scr[0:1] = m_new

    buf_a = (sa_scr, cma_scr, rowa_scr)
    buf_b = (sb_scr, cmb_scr, rowb_scr)
    m_scr[...] = jnp.full_like(m_scr, NEG_INF)
    l_scr[...] = jnp.zeros_like(l_scr)
    acc_scr[...] = jnp.zeros_like(acc_scr)

    @pl.when(j == 0)
    def _():
        for u in range(GROUP):
            put(buf_a, u, *head_slot(u, j, qs, mask_ref))
        phase_scr[0] = 0

    phase = phase_scr[0]

    def run(k, buf, nxt):
        @pl.when((k >= 0) & (k + 1 < n_groups))
        def _():
            step(k, buf, nxt, last=False)

        @pl.when((k >= 0) & (k + 1 == n_groups))
        def _():
            step(k, buf, nxt, last=True)

    def pair(t, carry):
        run(2 * t - phase, buf_a, buf_b)
        run(2 * t + 1 - phase, buf_b, buf_a)
        return carry

    lax.fori_loop(0, (n_groups + phase + 1) // 2, pair, 0)
    phase_scr[0] = (n_groups + phase) % 2

    o = acc_scr[...] * (1.0 / l_scr[0:1])
    o_ref[0] = jnp.concatenate([o[:, :tq].T, o[:, tq:].T], axis=1).astype(BF16)


def _moba_attn(rel_bias, q, k, vt, mask, table):
    b, s, dq = q.shape
    nb = k.shape[2]
    tq = MOBA_BLOCK
    gw = KV_GROUP * HEAD_DIM
    return pl.pallas_call(
        _attn_kernel,
        grid=(b, N_KV_HEADS, nb),
        in_specs=[pl.BlockSpec(memory_space=pltpu.SMEM),
                  pl.BlockSpec((1, tq, gw), lambda i, g, j: (i, j, g)),
                  pl.BlockSpec((1, tq, gw), lambda i, g, j: (i, jnp.minimum(j + 1, nb - 1), g)),
                  pl.BlockSpec((1, 1, nb, tq, HEAD_DIM), lambda i, g, j: (i, g, 0, 0, 0)),
                  pl.BlockSpec((1, 1, nb, HEAD_DIM, tq), lambda i, g, j: (i, g, 0, 0, 0)),
                  pl.BlockSpec((1, 1, 1, nb, KV_GROUP * tq), lambda i, g, j: (i, g, j, 0, 0)),
                  pl.BlockSpec((1, 1, 1, nb, KV_GROUP * tq),
                               lambda i, g, j: (i, g, jnp.minimum(j + 1, nb - 1), 0, 0)),
                  pl.BlockSpec((KV_GROUP, 2 * MOBA_BLOCK, MOBA_BLOCK), lambda i, g, j: (g, 0, 0))],
        out_specs=pl.BlockSpec((1, tq, gw), lambda i, g, j: (i, j, g)),
        out_shape=jax.ShapeDtypeStruct((b, s, dq), BF16),
        scratch_shapes=[pltpu.VMEM((SUBLANES, KV_GROUP * tq), F32),
                        pltpu.VMEM((SUBLANES, KV_GROUP * tq), F32),
                        pltpu.VMEM((HEAD_DIM, KV_GROUP * tq), F32),
                        pltpu.VMEM((GROUP, MOBA_BLOCK, KV_GROUP * tq), F32),
                        pltpu.VMEM((GROUP, MOBA_BLOCK, KV_GROUP * tq), F32),
                        pltpu.VMEM((SUBLANES, KV_GROUP * tq), F32),
                        pltpu.VMEM((SUBLANES, KV_GROUP * tq), F32),
                        pltpu.VMEM((SUBLANES, KV_GROUP * tq), F32),
                        pltpu.VMEM((SUBLANES, KV_GROUP * tq), F32),
                        pltpu.SMEM((1,), jnp.int32)],
        compiler_params=_cparams(3),
        name="moba_attn",
    )(rel_bias.astype(F32), q, q, k, vt, mask, mask, table)


def kernel(x, c, mod_w, mod_b, norm_mix, norm_ffn, lru_w_in, lru_conv_w, lru_conv_b, lru_w_gates,
           lru_b_gates, lru_lambda, lru_w_out, kv_mod_w, kv_mod_b, kv_norm, w_kv, attn_w_q, attn_w_o,
           rel_bias, ffn_w_gate, ffn_w_up, ffn_w_down, final_norm):
    b, s, d = x.shape
    assert s % TM_FFN == 0 and s % MOBA_BLOCK == 0 and b <= SUBLANES
    dkv = N_KV_HEADS * HEAD_DIM

    c_pad = jnp.zeros((SUBLANES, d), F32).at[:b].set(c.astype(F32))
    mod = _adaln_mod(c_pad, mod_w, mod_b)[:, :b]
    kvmod = _adaln_mod(c_pad, kv_mod_w[None], kv_mod_b[None])[0, :b]
    mod0 = mod[0][:, None, :]
    mod1 = mod[1][:, None, :]
    kvmod = kvmod[:, None, :]

    x = _rglru_layer(x, mod0, norm_mix[0], lru_w_in[0].astype(BF16), lru_conv_w[0], lru_conv_b[0],
                     lru_w_gates[0].astype(BF16), lru_b_gates[0], lru_lambda[0], lru_w_out[0].astype(BF16))
    x = _ffn_layer(x, mod0, norm_ffn[0], ffn_w_gate[0].astype(BF16), ffn_w_up[0].astype(BF16),
                   ffn_w_down[0].astype(BF16))

    q, k, vt, mask = _qkv_gate(x, kvmod, mod1, kv_norm, norm_mix[1], w_kv[:, :dkv].astype(BF16),
                               w_kv[:, dkv:].T.astype(BF16), attn_w_q[0].astype(BF16))
    table = _t5_table(rel_bias)
    attn = _moba_attn(rel_bias, q, k, vt, mask, table)
    return _ffn_layer(x, mod1, norm_ffn[1], ffn_w_gate[1].astype(BF16), ffn_w_up[1].astype(BF16),
                      ffn_w_down[1].astype(BF16), attn=attn, w_o=attn_w_o[0].astype(BF16),
                      final_gain=final_norm)
```

```python
import functools
import math

import numpy as np
import jax
import jax.numpy as jnp
from jax import lax
from jax.experimental import pallas as pl
from jax.experimental.pallas import tpu as pltpu

F32 = jnp.float32
BF16 = jnp.bfloat16
HIGHEST = lax.Precision.HIGHEST

LRU_BLOCK = 256
CONV_WIDTH = 4
LRU_C = 8.0
N_HEADS = 8
N_KV_HEADS = 4
HEAD_DIM = 128
KV_GROUP = N_HEADS // N_KV_HEADS
MOBA_BLOCK = 256
MOBA_TOPK = 3
REL_BUCKETS = 32
REL_MAX_DIST = 128
RMS_EPS = 1e-6
NEG_INF = -1e30
LOG2E = math.log2(math.e)
GELU_C1 = 2.0 * math.sqrt(2.0 / math.pi)
GELU_C2 = GELU_C1 * 0.044715

SUBLANES = 8
LANES = 128
VMEM_LIMIT = 56 * 1024 * 1024

TM_LRU = 256
TM_FFN = 512
FF_CHUNK = 256
MOD_TN = 2048
GROUP = 4
NEAR_FAR = GROUP - 2


def _cparams(n_axes):
    return pltpu.CompilerParams(dimension_semantics=("arbitrary",) * n_axes,
                                vmem_limit_bytes=VMEM_LIMIT)


def _const_spec(shape):
    nd = len(shape)
    return pl.BlockSpec(shape, lambda *_: (0,) * nd, pipeline_mode=pl.Buffered(1))


def _dot(a, b):
    return jnp.dot(a, b, preferred_element_type=F32)


def _dot_nt(a, b, precision=None):
    return lax.dot_general(a, b, (((1,), (1,)), ((), ())), preferred_element_type=F32,
                           precision=precision)


def _norm_mod(xt, gain, shift, scale):
    inv = lax.rsqrt(jnp.mean(xt * xt, axis=-1, keepdims=True) + RMS_EPS)
    return (xt * inv) * (gain * (1.0 + scale)) + shift


def _mod_kernel(c_ref, w_ref, b_ref, o_ref):
    c = c_ref[...]
    cs = c * jax.nn.sigmoid(c)
    o_ref[0] = jnp.dot(cs, w_ref[0], preferred_element_type=F32, precision=HIGHEST) + b_ref[0]


def _adaln_mod(c_pad, w, b):
    n_layers, d, n = w.shape
    tn = min(MOD_TN, n)
    return pl.pallas_call(
        _mod_kernel,
        grid=(n_layers, n // tn),
        in_specs=[pl.BlockSpec((SUBLANES, d), lambda l, j: (0, 0)),
                  pl.BlockSpec((1, d, tn), lambda l, j: (l, 0, j)),
                  pl.BlockSpec((1, 1, tn), lambda l, j: (l, 0, j))],
        out_specs=pl.BlockSpec((1, SUBLANES, tn), lambda l, j: (l, 0, j)),
        out_shape=jax.ShapeDtypeStruct((n_layers, SUBLANES, n), F32),
        compiler_params=_cparams(2),
        name="adaln_mod",
    )(c_pad, w, b.reshape(n_layers, 1, n))


def _t5_thresholds():
    max_exact = REL_BUCKETS // 2
    d = np.arange(0, 4 * REL_MAX_DIST)
    dd = np.maximum(d, 1).astype(np.float32)
    large = max_exact + (np.log(dd / max_exact) / math.log(REL_MAX_DIST / max_exact)
                         * (REL_BUCKETS - max_exact)).astype(np.int32)
    large = np.minimum(large, REL_BUCKETS - 1)
    bucket = np.where(d < max_exact, d, large)
    return [int(np.argmax(bucket >= b)) for b in range(REL_BUCKETS)]


def _t5_kernel(rb_ref, o_ref, *, thr):
    h = pl.program_id(0)
    shape = (2 * MOBA_BLOCK, MOBA_BLOCK)
    key = lax.broadcasted_iota(jnp.int32, shape, 0)
    qry = lax.broadcasted_iota(jnp.int32, shape, 1)
    dist = qry + MOBA_BLOCK - key
    val = jnp.full(shape, rb_ref[h, REL_BUCKETS - 1], F32)
    for b in range(REL_BUCKETS - 2, -1, -1):
        val = jnp.where(dist < thr[b + 1], rb_ref[h, b], val)
    o_ref[0] = jnp.where(dist < 0, NEG_INF, LOG2E * val)


def _t5_table(rel_bias):
    return pl.pallas_call(
        functools.partial(_t5_kernel, thr=_t5_thresholds()),
        grid=(N_HEADS,),
        in_specs=[pl.BlockSpec(memory_space=pltpu.SMEM)],
        out_specs=pl.BlockSpec((1, 2 * MOBA_BLOCK, MOBA_BLOCK), lambda h: (h, 0, 0)),
        out_shape=jax.ShapeDtypeStruct((N_HEADS, 2 * MOBA_BLOCK, MOBA_BLOCK), F32),
        compiler_params=_cparams(1),
        name="t5_table",
    )(rel_bias.astype(F32))


def _time_permutation(tm):
    seg = tm // SUBLANES
    row = np.arange(tm)
    perm = np.zeros((tm, tm), np.float32)
    perm[row, (row % SUBLANES) * seg + row // SUBLANES] = 1.0
    return jnp.asarray(perm, BF16), jnp.asarray(perm.T, BF16)


def _vrow(v, g):
    return v[g * SUBLANES:(g + 1) * SUBLANES]


def _sublane_scan(a, b, h0):
    sub = lax.broadcasted_iota(jnp.int32, a.shape, 0)
    s = 1
    while s < SUBLANES:
        a_sh = jnp.where(sub < s, 1.0, pltpu.roll(a, s, 0))
        b_sh = jnp.where(sub < s, 0.0, pltpu.roll(b, s, 0))
        b = a * b_sh + b
        a = a * a_sh
        s *= 2
    return a * h0 + b


def _rglru_kernel(x_ref, mod_ref, g_ref, perm_ref, permt_ref, win_ref, cw_ref, cb_ref, wg_ref, bg_ref,
                  lam_ref, wout_ref, o_ref, conv_scr, h_scr):
    nbatch = x_ref.shape[0]
    d = x_ref.shape[-1]
    w = lam_ref.shape[-1]
    tm = x_ref.shape[1]
    seg = tm // SUBLANES
    halo = CONV_WIDTH - 1

    @pl.when(pl.program_id(0) == 0)
    def _():
        conv_scr[...] = jnp.zeros_like(conv_scr)
        h_scr[...] = jnp.zeros_like(h_scr)

    sub = lax.broadcasted_iota(jnp.int32, (SUBLANES, LRU_BLOCK), 0)
    blocks = [(hd * LRU_BLOCK, (hd + 1) * LRU_BLOCK) for hd in range(w // LRU_BLOCK)]
    streams = [(bi, hd) for bi in range(nbatch) for hd in range(len(blocks))]
    hp, uy, ux, xc, gates, hy = {}, {}, {}, {}, {}, {}

    def normalize(bi):
        mod = mod_ref[bi]
        h = _norm_mod(x_ref[bi], g_ref[...], mod[:, 0:d], mod[:, d:2 * d]).astype(BF16)
        hp[bi] = _dot(perm_ref[...], h).astype(BF16)

    def in_proj(bi, hd):
        lo, hi = blocks[hd]
        uy[bi, hd] = _dot(hp[bi], win_ref[:, lo:hi])
        ux[bi, hd] = _dot(hp[bi], win_ref[:, w + lo:w + hi])

    def conv(bi, hd):
        lo, hi = blocks[hd]
        u = ux[bi, hd]
        tops = []
        for i in range(halo):
            cur = pltpu.roll(_vrow(u, seg - halo + i), 1, 0)
            prv = pltpu.roll(conv_scr[bi, i, :, lo:hi], 1, 0)
            tops.append(jnp.where(sub == 0, prv, cur))
            conv_scr[bi, i, :, lo:hi] = _vrow(u, seg - halo + i)
        cw = cw_ref[:, lo:hi]
        v = u * cw[CONV_WIDTH - 1:CONV_WIDTH]
        for k in range(1, CONV_WIDTH):
            shifted = jnp.concatenate(tops[halo - k:] + [u[:tm - k * SUBLANES]], axis=0)
            v = v + shifted * cw[CONV_WIDTH - 1 - k:CONV_WIDTH - k]
        xc[bi, hd] = v + cb_ref[:, lo:hi]

    def gate_proj(bi, hd):
        gates[bi, hd] = _dot(xc[bi, hd].astype(BF16), wg_ref[hd])

    def recur(bi, hd):
        lo, hi = blocks[hd]
        u, v, gt = uy[bi, hd], xc[bi, hd], gates[bi, hd]
        y = u * jax.nn.sigmoid(u * (GELU_C1 + GELU_C2 * (u * u)))
        r = jax.nn.sigmoid(gt[:, :LRU_BLOCK] + bg_ref[0:1, lo:hi])
        i_gate = jax.nn.sigmoid(gt[:, LRU_BLOCK:] + bg_ref[1:2, lo:hi])
        lam = lam_ref[:, lo:hi]
        softplus_neg = jnp.maximum(-lam, 0.0) + jnp.log1p(jnp.exp(-jnp.abs(lam)))
        neg_log_a = r * (LRU_C * softplus_neg)
        a = jnp.exp(-neg_log_a)
        one_minus_a2 = (1.0 + a * a) * jnp.tanh(neg_log_a)
        mult = jnp.where(one_minus_a2 == 0.0, 0.0, one_minus_a2 * lax.rsqrt(one_minus_a2))
        b = mult * (i_gate * v)
        loc = [_vrow(b, 0)]
        prod = [_vrow(a, 0)]
        for g in range(1, seg):
            ag = _vrow(a, g)
            loc.append(ag * loc[-1] + _vrow(b, g))
            prod.append(ag * prod[-1])
        h_in = h_scr[bi, 0:1, lo:hi]
        seg_end = _sublane_scan(prod[-1], loc[-1], h_in)
        seg_in = jnp.where(sub == 0, h_in, pltpu.roll(seg_end, 1, 0))
        h_scr[bi, 0:1, lo:hi] = seg_end[SUBLANES - 1:SUBLANES]
        hs = jnp.concatenate([loc[g] + prod[g] * seg_in for g in range(seg)], axis=0)
        hy[bi, hd] = (hs * y).astype(BF16)

    def out_proj(bi):
        hy_seg = jnp.concatenate([hy[bi, hd] for hd in range(len(blocks))], axis=1)
        hy_time = _dot(permt_ref[...], hy_seg).astype(BF16)
        o_ref[bi] = x_ref[bi] + mod_ref[bi][:, 2 * d:3 * d] * _dot(hy_time, wout_ref[...])

    stages = (in_proj, conv, gate_proj, recur)
    normalize(0)
    for slot in range(len(streams) + len(stages) - 1):
        for depth, stage in enumerate(stages):
            i = slot - depth
            if 0 <= i < len(streams):
                bi, hd = streams[i]
                if stage is in_proj and hd == len(blocks) - 2 and bi + 1 < nbatch:
                    normalize(bi + 1)
                stage(bi, hd)
                if stage is recur and hd == len(blocks) - 1:
                    out_proj(bi)


def _rglru_layer(x, mod, gain, w_in, conv_w, conv_b, w_gates, b_gates, lam, w_out):
    b, s, d = x.shape
    w = lam.shape[-1]
    tm = TM_LRU
    perm, perm_t = _time_permutation(tm)
    return pl.pallas_call(
        _rglru_kernel,
        grid=(s // tm,),
        in_specs=[pl.BlockSpec((b, tm, d), lambda t: (0, t, 0)),
                  _const_spec(mod.shape),
                  _const_spec((1, d)),
                  _const_spec((tm, tm)),
                  _const_spec((tm, tm)),
                  _const_spec(w_in.shape),
                  _const_spec(conv_w.shape),
                  _const_spec((1, w)),
                  _const_spec(w_gates.shape),
                  _const_spec(b_gates.shape),
                  _const_spec((1, w)),
                  _const_spec(w_out.shape)],
        out_specs=pl.BlockSpec((b, tm, d), lambda t: (0, t, 0)),
        out_shape=jax.ShapeDtypeStruct(x.shape, F32),
        scratch_shapes=[pltpu.VMEM((b, CONV_WIDTH - 1, SUBLANES, w), F32),
                        pltpu.VMEM((b, SUBLANES, w), F32)],
        compiler_params=_cparams(1),
        name="rglru",
    )(x, mod, gain.reshape(1, d), perm, perm_t, w_in, conv_w, conv_b.reshape(1, w), w_gates, b_gates,
      lam.reshape(1, w), w_out)


def _ffn_kernel(*refs, has_attn, final):
    x_ref, mod_ref, g_ref, wg_ref, wu_ref, wd_ref = refs[:6]
    rest = list(refs[6:])
    att_ref = wo_ref = fn_ref = None
    if has_attn:
        att_ref, wo_ref = rest[:2]
        rest = rest[2:]
    if final:
        fn_ref = rest[0]
        rest = rest[1:]
    o_ref, = rest
    d = x_ref.shape[-1]
    ff = wg_ref.shape[-1]
    tm = x_ref.shape[1]

    xt = x_ref[0]
    mod = mod_ref[0]
    if has_attn:
        xt = xt + mod[:, 2 * d:3 * d] * _dot(att_ref[0], wo_ref[...])
    h = _norm_mod(xt, g_ref[...], mod[:, 3 * d:4 * d], mod[:, 4 * d:5 * d]).astype(BF16)
    acc = jnp.zeros((tm, d), F32)
    for c in range(ff // FF_CHUNK):
        lo = c * FF_CHUNK
        hi = lo + FF_CHUNK
        gate = _dot(h, wg_ref[:, lo:hi])
        up = _dot(h, wu_ref[:, lo:hi])
        act = (gate * jax.nn.sigmoid(gate)) * up
        acc = acc + _dot(act.astype(BF16), wd_ref[lo:hi, :])
    y = xt + mod[:, 5 * d:6 * d] * acc
    if final:
        y = y * lax.rsqrt(jnp.mean(y * y, axis=-1, keepdims=True) + RMS_EPS) * fn_ref[...]
    o_ref[0] = y


def _ffn_layer(x, mod, gain, w_gate, w_up, w_down, attn=None, w_o=None, final_gain=None):
    b, s, d = x.shape
    tm = TM_FFN
    has_attn = attn is not None
    final = final_gain is not None
    in_specs = [pl.BlockSpec((1, tm, d), lambda i, t: (i, t, 0)),
                pl.BlockSpec((1, 1, mod.shape[-1]), lambda i, t: (i, 0, 0)),
                _const_spec((1, d)),
                _const_spec(w_gate.shape),
                _const_spec(w_up.shape),
                _const_spec(w_down.shape)]
    args = [x, mod, gain.reshape(1, d), w_gate, w_up, w_down]
    if has_attn:
        in_specs += [pl.BlockSpec((1, tm, d), lambda i, t: (i, t, 0)), _const_spec(w_o.shape)]
        args += [attn, w_o]
    if final:
        in_specs += [_const_spec((1, d))]
        args += [final_gain.reshape(1, d)]
    return pl.pallas_call(
        functools.partial(_ffn_kernel, has_attn=has_attn, final=final),
        grid=(b, s // tm),
        in_specs=in_specs,
        out_specs=pl.BlockSpec((1, tm, d), lambda i, t: (i, t, 0)),
        out_shape=jax.ShapeDtypeStruct(x.shape, F32),
        compiler_params=_cparams(2),
        name="ffn_attn_final" if has_attn else "ffn",
    )(*args)


def _qkv_kernel(x_ref, xn_ref, kvmod_ref, mod_ref, kvg_ref, qg_ref, wk_ref, wvt_ref, wq_ref,
                q_ref, k_ref, vt_ref, mask_ref, km_scr, hqa_scr, hkva_scr, hqb_scr, hkvb_scr):
    nbatch = x_ref.shape[0]
    d = x_ref.shape[-1]
    tm = MOBA_BLOCK
    nb = mask_ref.shape[3]
    step = pl.program_id(0)
    buf_a = (hqa_scr, hkva_scr)
    buf_b = (hqb_scr, hkvb_scr)
    blk = lax.broadcasted_iota(jnp.int32, (nb, tm), 0).astype(F32)
    q_scale = HEAD_DIM ** -0.5 * LOG2E
    pair_w = KV_GROUP * HEAD_DIM

    def normalize(src_ref, row0, bufs):
        for bi in range(nbatch):
            xt = src_ref[bi, row0:row0 + tm, :]
            xn = xt * lax.rsqrt(jnp.mean(xt * xt, axis=-1, keepdims=True) + RMS_EPS)
            mod = mod_ref[bi]
            kvmod = kvmod_ref[bi]
            rows = pl.ds(bi * tm, tm)
            bufs[0][rows, :] = (xn * (qg_ref[...] * (1.0 + mod[:, d:2 * d])) + mod[:, 0:d]).astype(BF16)
            bufs[1][rows, :] = (xn * (kvg_ref[...] * (1.0 + kvmod[:, d:2 * d])) + kvmod[:, 0:d]).astype(BF16)

    def process(half, cur, nxt_src, nxt_row0, nxt):
        j = 2 * step + half
        past = blk < j.astype(F32)
        out_rows = slice(half * tm, (half + 1) * tm)
        q_pair = {}
        h_q = cur[0][...]
        h_kv = cur[1][...]
        normalize(nxt_src, nxt_row0, nxt)

        def q_proj(hk):
            qp = _dot(h_q, wq_ref[:, hk * pair_w:(hk + 1) * pair_w]) * q_scale
            for bi in range(nbatch):
                q_ref[bi, out_rows, hk * pair_w:(hk + 1) * pair_w] = qp[bi * tm:(bi + 1) * tm].astype(BF16)
            q_pair[hk] = qp

        def select_blocks(bi, hq):
            hk, sub = divmod(hq, KV_GROUP)
            q_h = q_pair[hk][bi * tm:(bi + 1) * tm, sub * HEAD_DIM:(sub + 1) * HEAD_DIM]
            km_h = km_scr[bi, :, hk * HEAD_DIM:(hk + 1) * HEAD_DIM]
            gate = _dot_nt(km_h.astype(BF16), q_h.astype(BF16))
            gate = jnp.where(past, gate, NEG_INF)
            mask = jnp.full((nb, tm), NEG_INF, F32)
            for _ in range(MOBA_TOPK):
                best = jnp.max(gate, axis=0, keepdims=True)
                first = jnp.min(jnp.where(gate == best, blk, nb), axis=0, keepdims=True)
                hit = blk == first
                mask = jnp.where(hit, 0.0, mask)
                gate = jnp.where(hit, -jnp.inf, gate)
            mask = jnp.where(past, mask, NEG_INF)
            mask_ref[bi, hk, half, :, sub * tm:(sub + 1) * tm] = mask

        def k_proj():
            k = _dot(h_kv, wk_ref[...])
            for bi in range(nbatch):
                kb = k[bi * tm:(bi + 1) * tm]
                for hk in range(N_KV_HEADS):
                    k_ref[bi, hk, half] = kb[:, hk * HEAD_DIM:(hk + 1) * HEAD_DIM].astype(BF16)
            return k

        def v_proj():
            vt = _dot_nt(wvt_ref[...], h_kv)
            for bi in range(nbatch):
                for hk in range(N_KV_HEADS):
                    vt_ref[bi, hk, half] = vt[hk * HEAD_DIM:(hk + 1) * HEAD_DIM,
                                              bi * tm:(bi + 1) * tm].astype(BF16)

        k = None
        q_proj(0)
        for hk in range(N_KV_HEADS):
            if hk + 1 < N_KV_HEADS:
                q_proj(hk + 1)
            if hk == N_KV_HEADS - 2:
                k = k_proj()
            if hk == N_KV_HEADS - 1:
                v_proj()
            for bi in range(nbatch):
                for sub in range(KV_GROUP):
                    select_blocks(bi, hk * KV_GROUP + sub)
        for bi in range(nbatch):
            km_scr[bi, pl.ds(j, 1), :] = jnp.mean(k[bi * tm:(bi + 1) * tm], axis=0, keepdims=True)

    @pl.when(step == 0)
    def _():
        km_scr[...] = jnp.zeros_like(km_scr)
        normalize(x_ref, 0, buf_a)

    process(0, buf_a, x_ref, tm, buf_b)
    process(1, buf_b, xn_ref, 0, buf_a)


def _qkv_gate(x, kvmod, mod, kv_gain, q_gain, w_k, w_vt, w_q):
    b, s, d = x.shape
    tm = MOBA_BLOCK
    nb = s // tm
    dkv = N_KV_HEADS * HEAD_DIM
    dq = N_HEADS * HEAD_DIM
    out_shape = (jax.ShapeDtypeStruct((b, s, dq), BF16),
                 jax.ShapeDtypeStruct((b, N_KV_HEADS, nb, tm, HEAD_DIM), BF16),
                 jax.ShapeDtypeStruct((b, N_KV_HEADS, nb, HEAD_DIM, tm), BF16),
                 jax.ShapeDtypeStruct((b, N_KV_HEADS, nb, nb, KV_GROUP * tm), F32))
    out_specs = (pl.BlockSpec((b, 2 * tm, dq), lambda t: (0, t, 0)),
                 pl.BlockSpec((b, N_KV_HEADS, 2, tm, HEAD_DIM), lambda t: (0, 0, t, 0, 0)),
                 pl.BlockSpec((b, N_KV_HEADS, 2, HEAD_DIM, tm), lambda t: (0, 0, t, 0, 0)),
                 pl.BlockSpec((b, N_KV_HEADS, 2, nb, KV_GROUP * tm), lambda t: (0, 0, t, 0, 0)))
    return pl.pallas_call(
        _qkv_kernel,
        grid=(nb // 2,),
        in_specs=[pl.BlockSpec((b, 2 * tm, d), lambda t: (0, t, 0)),
                  pl.BlockSpec((b, tm, d), lambda t: (0, jnp.minimum(2 * t + 2, nb - 1), 0)),
                  _const_spec(kvmod.shape),
                  _const_spec(mod.shape),
                  _const_spec((1, d)),
                  _const_spec((1, d)),
                  _const_spec(w_k.shape),
                  _const_spec(w_vt.shape),
                  _const_spec(w_q.shape)],
        out_specs=out_specs,
        out_shape=out_shape,
        scratch_shapes=[pltpu.VMEM((b, nb, dkv), F32),
                        pltpu.VMEM((b * tm, d), BF16),
                        pltpu.VMEM((b * tm, d), BF16),
                        pltpu.VMEM((b * tm, d), BF16),
                        pltpu.VMEM((b * tm, d), BF16)],
        compiler_params=_cparams(1),
        name="qkv_gate",
    )(x, x, kvmod, mod, kv_gain.reshape(1, d), q_gain.reshape(1, d), w_k, w_vt, w_q)


def _attn_kernel(rb_ref, q_ref, qn_ref, k_ref, vt_ref, mask_ref, maskn_ref, tab_ref, o_ref, m_scr, l_scr,
                 acc_scr, sa_scr, sb_scr, cma_scr, cmb_scr, rowa_scr, rowb_scr, phase_scr):
    g = pl.program_id(1)
    j = pl.program_id(2)
    nb = k_ref.shape[2]
    tq = q_ref.shape[1]

    def stack_heads(q2):
        return jnp.concatenate([q2[:, :HEAD_DIM], q2[:, HEAD_DIM:]], axis=0)

    qs = stack_heads(q_ref[0])

    def scores(n, qx=qs):
        return _dot_nt(k_ref[0, 0, n], qx)

    def tile_bias(lo):
        return jnp.concatenate([tab_ref[0, lo:lo + MOBA_BLOCK, :], tab_ref[1, lo:lo + MOBA_BLOCK, :]], axis=1)

    def mask_row(n, mref=mask_ref):
        return mref[0, 0, 0, pl.ds(n, 1), :]

    def col_max(s):
        return jnp.max(s, axis=0, keepdims=True)

    def col_sum(s):
        return jnp.sum(s, axis=0, keepdims=True)

    lane = lax.broadcasted_iota(jnp.int32, (1, KV_GROUP * tq), 1)
    far_bias = LOG2E * jnp.where(lane < tq, rb_ref[KV_GROUP * g, REL_BUCKETS - 1],
                                 rb_ref[KV_GROUP * g + 1, REL_BUCKETS - 1])

    def far_row(n, valid, mref=mask_ref):
        return jnp.where(valid, mask_row(n, mref) + far_bias, NEG_INF)

    n_far = jnp.maximum(j - 1 - NEAR_FAR, 0)
    n_groups = 1 + (n_far + GROUP - 1) // GROUP
    jp = jnp.maximum(j - 1, 0)
    head_blocks = [j, jp] + [jnp.maximum(j - 2 - u, 0) for u in range(NEAR_FAR)]

    def group_blocks(k):
        return [jnp.where(k == 0, head_blocks[u], jnp.minimum((k - 1) * GROUP + u, nb - 1))
                for u in range(GROUP)]

    def put(buf, u, s, row):
        s_buf, cm_buf, row_buf = buf
        s_buf[u] = s
        cm_buf[u:u + 1] = col_max(s)
        row_buf[u:u + 1] = row

    def head_slot(u, jq, qx, mref):
        if u == 0:
            return scores(jq, qx) + tile_bias(MOBA_BLOCK), jnp.zeros((1, KV_GROUP * tq), F32)
        if u == 1:
            jqp = jnp.maximum(jq - 1, 0)
            return scores(jqp, qx) + tile_bias(0), jnp.where(jq >= 1, mask_row(jqp, mref), NEG_INF)
        n = jnp.maximum(jq - u, 0)
        return scores(n, qx), far_row(n, jq - u >= 0, mref)

    j_next = jnp.minimum(j + 1, nb - 1)

    def step(k, buf, nxt, last):
        s_buf, cm_buf, row_buf = buf
        blocks = group_blocks(k)
        m_old = m_scr[0:1]
        rows = [row_buf[u:u + 1] for u in range(GROUP)]
        m_new = m_old
        for u in range(GROUP):
            m_new = jnp.maximum(m_new, cm_buf[u:u + 1] + rows[u])
        alpha = jnp.exp2(m_old - m_new)
        l_new = alpha * l_scr[0:1]
        pv = None
        qs_next = stack_heads(qn_ref[0]) if last else None
        for u in range(GROUP):
            if last:
                put(nxt, u, *head_slot(u, j_next, qs_next, maskn_ref))
            else:
                n = k * GROUP + u
                nc = jnp.minimum(n, nb - 1)
                put(nxt, u, scores(nc), far_row(nc, n < n_far))
            p = jnp.exp2(s_buf[u] + (rows[u] - m_new))
            l_new = l_new + col_sum(p)
            d = _dot(vt_ref[0, 0, blocks[u]], p.astype(BF16))
            pv = d if pv is None else pv + d
        acc_scr[...] = alpha * acc_scr[...] + pv
        l_scr[0:1] = l_new
        m_scr[0:1] = m_new

    buf_a = (sa_scr, cma_scr, rowa_scr)
    buf_b = (sb_scr, cmb_scr, rowb_scr)
    m_scr[...] = jnp.full_like(m_scr, NEG_INF)
    l_scr[...] = jnp.zeros_like(l_scr)
    acc_scr[...] = jnp.zeros_like(acc_scr)

    @pl.when(j == 0)
    def _():
        for u in range(GROUP):
            put(buf_a, u, *head_slot(u, j, qs, mask_ref))
        phase_scr[0] = 0

    phase = phase_scr[0]

    def run(k, buf, nxt):
        @pl.when((k >= 0) & (k + 1 < n_groups))
        def _():
            step(k, buf, nxt, last=False)

        @pl.when((k >= 0) & (k + 1 == n_groups))
        def _():
            step(k, buf, nxt, last=True)

    def pair(t, carry):
        run(2 * t - phase, buf_a, buf_b)
        run(2 * t + 1 - phase, buf_b, buf_a)
        return carry

    lax.fori_loop(0, (n_groups + phase + 1) // 2, pair, 0)
    phase_scr[0] = (n_groups + phase) % 2

    o = acc_scr[...] * (1.0 / l_scr[0:1])
    o_ref[0] = jnp.concatenate([o[:, :tq].T, o[:, tq:].T], axis=1).astype(BF16)


def _moba_attn(rel_bias, q, k, vt, mask, table):
    b, s, dq = q.shape
    nb = k.shape[2]
    tq = MOBA_BLOCK
    gw = KV_GROUP * HEAD_DIM
    return pl.pallas_call(
        _attn_kernel,
        grid=(b, N_KV_HEADS, nb),
        in_specs=[pl.BlockSpec(memory_space=pltpu.SMEM),
                  pl.BlockSpec((1, tq, gw), lambda i, g, j: (i, j, g)),
                  pl.BlockSpec((1, tq, gw), lambda i, g, j: (i, jnp.minimum(j + 1, nb - 1), g)),
                  pl.BlockSpec((1, 1, nb, tq, HEAD_DIM), lambda i, g, j: (i, g, 0, 0, 0)),
                  pl.BlockSpec((1, 1, nb, HEAD_DIM, tq), lambda i, g, j: (i, g, 0, 0, 0)),
                  pl.BlockSpec((1, 1, 1, nb, KV_GROUP * tq), lambda i, g, j: (i, g, j, 0, 0)),
                  pl.BlockSpec((1, 1, 1, nb, KV_GROUP * tq),
                               lambda i, g, j: (i, g, jnp.minimum(j + 1, nb - 1), 0, 0)),
                  pl.BlockSpec((KV_GROUP, 2 * MOBA_BLOCK, MOBA_BLOCK), lambda i, g, j: (g, 0, 0))],
        out_specs=pl.BlockSpec((1, tq, gw), lambda i, g, j: (i, j, g)),
        out_shape=jax.ShapeDtypeStruct((b, s, dq), BF16),
        scratch_shapes=[pltpu.VMEM((SUBLANES, KV_GROUP * tq), F32),
                        pltpu.VMEM((SUBLANES, KV_GROUP * tq), F32),
                        pltpu.VMEM((HEAD_DIM, KV_GROUP * tq), F32),
                        pltpu.VMEM((GROUP, MOBA_BLOCK, KV_GROUP * tq), F32),
                        pltpu.VMEM((GROUP, MOBA_BLOCK, KV_GROUP * tq), F32),
                        pltpu.VMEM((SUBLANES, KV_GROUP * tq), F32),
                        pltpu.VMEM((SUBLANES, KV_GROUP * tq), F32),
                        pltpu.VMEM((SUBLANES, KV_GROUP * tq), F32),
                        pltpu.VMEM((SUBLANES, KV_GROUP * tq), F32),
                        pltpu.SMEM((1,), jnp.int32)],
        compiler_params=_cparams(3),
        name="moba_attn",
    )(rel_bias.astype(F32), q, q, k, vt, mask, mask, table)


def kernel(x, c, mod_w, mod_b, norm_mix, norm_ffn, lru_w_in, lru_conv_w, lru_conv_b, lru_w_gates,
           lru_b_gates, lru_lambda, lru_w_out, kv_mod_w, kv_mod_b, kv_norm, w_kv, attn_w_q, attn_w_o,
           rel_bias, ffn_w_gate, ffn_w_up, ffn_w_down, final_norm):
    b, s, d = x.shape
    assert s % TM_FFN == 0 and s % MOBA_BLOCK == 0 and b <= SUBLANES
    dkv = N_KV_HEADS * HEAD_DIM

    c_pad = jnp.zeros((SUBLANES, d), F32).at[:b].set(c.astype(F32))
    mod = _adaln_mod(c_pad, mod_w, mod_b)[:, :b]
    kvmod = _adaln_mod(c_pad, kv_mod_w[None], kv_mod_b[None])[0, :b]
    mod0 = mod[0][:, None, :]
    mod1 = mod[1][:, None, :]
    kvmod = kvmod[:, None, :]

    x = _rglru_layer(x, mod0, norm_mix[0], lru_w_in[0].astype(BF16), lru_conv_w[0], lru_conv_b[0],
                     lru_w_gates[0].astype(BF16), lru_b_gates[0], lru_lambda[0], lru_w_out[0].astype(BF16))
    x = _ffn_layer(x, mod0, norm_ffn[0], ffn_w_gate[0].astype(BF16), ffn_w_up[0].astype(BF16),
                   ffn_w_down[0].astype(BF16))

    q, k, vt, mask = _qkv_gate(x, kvmod, mod1, kv_norm, norm_mix[1], w_kv[:, :dkv].astype(BF16),
                               w_kv[:, dkv:].T.astype(BF16), attn_w_q[0].astype(BF16))
    table = _t5_table(rel_bias)
    attn = _moba_attn(rel_bias, q, k, vt, mask, table)
    return _ffn_layer(x, mod1, norm_ffn[1], ffn_w_gate[1].astype(BF16), ffn_w_up[1].astype(BF16),
                      ffn_w_down[1].astype(BF16), attn=attn, w_o=attn_w_o[0].astype(BF16),
                      final_gain=final_norm)
```

```python
import functools
import math

import numpy as np
import jax
import jax.numpy as jnp
from jax import lax
from jax.experimental import pallas as pl
from jax.experimental.pallas import tpu as pltpu

F32 = jnp.float32
BF16 = jnp.bfloat16
HIGHEST = lax.Precision.HIGHEST

LRU_BLOCK = 256
CONV_WIDTH = 4
LRU_C = 8.0
N_HEADS = 8
N_KV_HEADS = 4
HEAD_DIM = 128
KV_GROUP = N_HEADS // N_KV_HEADS
MOBA_BLOCK = 256
MOBA_TOPK = 3
REL_BUCKETS = 32
REL_MAX_DIST = 128
RMS_EPS = 1e-6
NEG_INF = -1e30
LOG2E = math.log2(math.e)
GELU_C1 = 2.0 * math.sqrt(2.0 / math.pi)
GELU_C2 = GELU_C1 * 0.044715

SUBLANES = 8
LANES = 128
VMEM_LIMIT = 56 * 1024 * 1024

TM_LRU = 256
TM_FFN = 512
FF_CHUNK = 256
MOD_TN = 2048
GROUP = 4
NEAR_FAR = GROUP - 2
ATTN_STREAMS = 4


def _cparams(n_axes):
    return pltpu.CompilerParams(dimension_semantics=("arbitrary",) * n_axes,
                                vmem_limit_bytes=VMEM_LIMIT)


def _const_spec(shape):
    nd = len(shape)
    return pl.BlockSpec(shape, lambda *_: (0,) * nd, pipeline_mode=pl.Buffered(1))


def _dot(a, b):
    return jnp.dot(a, b, preferred_element_type=F32)


def _dot_nt(a, b, precision=None):
    return lax.dot_general(a, b, (((1,), (1,)), ((), ())), preferred_element_type=F32,
                           precision=precision)


def _norm_mod(xt, gain, shift, scale):
    inv = lax.rsqrt(jnp.mean(xt * xt, axis=-1, keepdims=True) + RMS_EPS)
    return (xt * inv) * (gain * (1.0 + scale)) + shift


def _mod_kernel(c_ref, w_ref, b_ref, o_ref):
    c = c_ref[...]
    cs = c * jax.nn.sigmoid(c)
    o_ref[0] = jnp.dot(cs, w_ref[0], preferred_element_type=F32, precision=HIGHEST) + b_ref[0]


def _adaln_mod(c_pad, w, b):
    n_layers, d, n = w.shape
    tn = min(MOD_TN, n)
    return pl.pallas_call(
        _mod_kernel,
        grid=(n_layers, n // tn),
        in_specs=[pl.BlockSpec((SUBLANES, d), lambda l, j: (0, 0)),
                  pl.BlockSpec((1, d, tn), lambda l, j: (l, 0, j)),
                  pl.BlockSpec((1, 1, tn), lambda l, j: (l, 0, j))],
        out_specs=pl.BlockSpec((1, SUBLANES, tn), lambda l, j: (l, 0, j)),
        out_shape=jax.ShapeDtypeStruct((n_layers, SUBLANES, n), F32),
        compiler_params=_cparams(2),
        name="adaln_mod",
    )(c_pad, w, b.reshape(n_layers, 1, n))


def _t5_thresholds():
    max_exact = REL_BUCKETS // 2
    d = np.arange(0, 4 * REL_MAX_DIST)
    dd = np.maximum(d, 1).astype(np.float32)
    large = max_exact + (np.log(dd / max_exact) / math.log(REL_MAX_DIST / max_exact)
                         * (REL_BUCKETS - max_exact)).astype(np.int32)
    large = np.minimum(large, REL_BUCKETS - 1)
    bucket = np.where(d < max_exact, d, large)
    return [int(np.argmax(bucket >= b)) for b in range(REL_BUCKETS)]


def _t5_kernel(rb_ref, o_ref, *, thr):
    h = pl.program_id(0)
    shape = (2 * MOBA_BLOCK, MOBA_BLOCK)
    key = lax.broadcasted_iota(jnp.int32, shape, 0)
    qry = lax.broadcasted_iota(jnp.int32, shape, 1)
    dist = qry + MOBA_BLOCK - key
    val = jnp.full(shape, rb_ref[h, REL_BUCKETS - 1], F32)
    for b in range(REL_BUCKETS - 2, -1, -1):
        val = jnp.where(dist < thr[b + 1], rb_ref[h, b], val)
    o_ref[0] = jnp.where(dist < 0, NEG_INF, LOG2E * val)


def _t5_table(rel_bias):
    return pl.pallas_call(
        functools.partial(_t5_kernel, thr=_t5_thresholds()),
        grid=(N_HEADS,),
        in_specs=[pl.BlockSpec(memory_space=pltpu.SMEM)],
        out_specs=pl.BlockSpec((1, 2 * MOBA_BLOCK, MOBA_BLOCK), lambda h: (h, 0, 0)),
        out_shape=jax.ShapeDtypeStruct((N_HEADS, 2 * MOBA_BLOCK, MOBA_BLOCK), F32),
        compiler_params=_cparams(1),
        name="t5_table",
    )(rel_bias.astype(F32))


def _time_permutation(tm):
    seg = tm // SUBLANES
    row = np.arange(tm)
    perm = np.zeros((tm, tm), np.float32)
    perm[row, (row % SUBLANES) * seg + row // SUBLANES] = 1.0
    return jnp.asarray(perm, BF16), jnp.asarray(perm.T, BF16)


def _vrow(v, g):
    return v[g * SUBLANES:(g + 1) * SUBLANES]


def _sublane_scan(a, b, h0):
    sub = lax.broadcasted_iota(jnp.int32, a.shape, 0)
    s = 1
    while s < SUBLANES:
        a_sh = jnp.where(sub < s, 1.0, pltpu.roll(a, s, 0))
        b_sh = jnp.where(sub < s, 0.0, pltpu.roll(b, s, 0))
        b = a * b_sh + b
        a = a * a_sh
        s *= 2
    return a * h0 + b


def _rglru_kernel(x_ref, mod_ref, g_ref, perm_ref, permt_ref, win_ref, cw_ref, cb_ref, wg_ref, bg_ref,
                  lam_ref, wout_ref, o_ref, conv_scr, h_scr):
    nbatch = x_ref.shape[0]
    d = x_ref.shape[-1]
    w = lam_ref.shape[-1]
    tm = x_ref.shape[1]
    seg = tm // SUBLANES
    halo = CONV_WIDTH - 1

    @pl.when(pl.program_id(0) == 0)
    def _():
        conv_scr[...] = jnp.zeros_like(conv_scr)
        h_scr[...] = jnp.zeros_like(h_scr)

    sub = lax.broadcasted_iota(jnp.int32, (SUBLANES, LRU_BLOCK), 0)
    blocks = [(hd * LRU_BLOCK, (hd + 1) * LRU_BLOCK) for hd in range(w // LRU_BLOCK)]
    streams = [(bi, hd) for bi in range(nbatch) for hd in range(len(blocks))]
    hp, uy, ux, xc, gates, hy = {}, {}, {}, {}, {}, {}

    def normalize(bi):
        mod = mod_ref[bi]
        h = _norm_mod(x_ref[bi], g_ref[...], mod[:, 0:d], mod[:, d:2 * d]).astype(BF16)
        hp[bi] = _dot(perm_ref[...], h).astype(BF16)

    def in_proj(bi, hd):
        lo, hi = blocks[hd]
        uy[bi, hd] = _dot(hp[bi], win_ref[:, lo:hi])
        ux[bi, hd] = _dot(hp[bi], win_ref[:, w + lo:w + hi])

    def conv(bi, hd):
        lo, hi = blocks[hd]
        u = ux[bi, hd]
        tops = []
        for i in range(halo):
            cur = pltpu.roll(_vrow(u, seg - halo + i), 1, 0)
            prv = pltpu.roll(conv_scr[bi, i, :, lo:hi], 1, 0)
            tops.append(jnp.where(sub == 0, prv, cur))
            conv_scr[bi, i, :, lo:hi] = _vrow(u, seg - halo + i)
        cw = cw_ref[:, lo:hi]
        v = u * cw[CONV_WIDTH - 1:CONV_WIDTH]
        for k in range(1, CONV_WIDTH):
            shifted = jnp.concatenate(tops[halo - k:] + [u[:tm - k * SUBLANES]], axis=0)
            v = v + shifted * cw[CONV_WIDTH - 1 - k:CONV_WIDTH - k]
        xc[bi, hd] = v + cb_ref[:, lo:hi]

    def gate_proj(bi, hd):
        gates[bi, hd] = _dot(xc[bi, hd].astype(BF16), wg_ref[hd])

    def recur(bi, hd):
        lo, hi = blocks[hd]
        u, v, gt = uy[bi, hd], xc[bi, hd], gates[bi, hd]
        y = u * jax.nn.sigmoid(u * (GELU_C1 + GELU_C2 * (u * u)))
        r = jax.nn.sigmoid(gt[:, :LRU_BLOCK] + bg_ref[0:1, lo:hi])
        i_gate = jax.nn.sigmoid(gt[:, LRU_BLOCK:] + bg_ref[1:2, lo:hi])
        lam = lam_ref[:, lo:hi]
        softplus_neg = jnp.maximum(-lam, 0.0) + jnp.log1p(jnp.exp(-jnp.abs(lam)))
        neg_log_a = r * (LRU_C * softplus_neg)
        a = jnp.exp(-neg_log_a)
        one_minus_a2 = (1.0 + a * a) * jnp.tanh(neg_log_a)
        mult = jnp.where(one_minus_a2 == 0.0, 0.0, one_minus_a2 * lax.rsqrt(one_minus_a2))
        b = mult * (i_gate * v)
        loc = [_vrow(b, 0)]
        prod = [_vrow(a, 0)]
        for g in range(1, seg):
            ag = _vrow(a, g)
            loc.append(ag * loc[-1] + _vrow(b, g))
            prod.append(ag * prod[-1])
        h_in = h_scr[bi, 0:1, lo:hi]
        seg_end = _sublane_scan(prod[-1], loc[-1], h_in)
        seg_in = jnp.where(sub == 0, h_in, pltpu.roll(seg_end, 1, 0))
        h_scr[bi, 0:1, lo:hi] = seg_end[SUBLANES - 1:SUBLANES]
        hs = jnp.concatenate([loc[g] + prod[g] * seg_in for g in range(seg)], axis=0)
        hy[bi, hd] = (hs * y).astype(BF16)

    def out_proj(bi):
        hy_seg = jnp.concatenate([hy[bi, hd] for hd in range(len(blocks))], axis=1)
        hy_time = _dot(permt_ref[...], hy_seg).astype(BF16)
        o_ref[bi] = x_ref[bi] + mod_ref[bi][:, 2 * d:3 * d] * _dot(hy_time, wout_ref[...])

    stages = (in_proj, conv, gate_proj, recur)
    normalize(0)
    for slot in range(len(streams) + len(stages) - 1):
        for depth, stage in enumerate(stages):
            i = slot - depth
            if 0 <= i < len(streams):
                bi, hd = streams[i]
                if stage is in_proj and hd == len(blocks) - 2 and bi + 1 < nbatch:
                    normalize(bi + 1)
                stage(bi, hd)
                if stage is recur and hd == len(blocks) - 1:
                    out_proj(bi)


def _rglru_layer(x, mod, gain, w_in, conv_w, conv_b, w_gates, b_gates, lam, w_out):
    b, s, d = x.shape
    w = lam.shape[-1]
    tm = TM_LRU
    perm, perm_t = _time_permutation(tm)
    return pl.pallas_call(
        _rglru_kernel,
        grid=(s // tm,),
        in_specs=[pl.BlockSpec((b, tm, d), lambda t: (0, t, 0)),
                  _const_spec(mod.shape),
                  _const_spec((1, d)),
                  _const_spec((tm, tm)),
                  _const_spec((tm, tm)),
                  _const_spec(w_in.shape),
                  _const_spec(conv_w.shape),
                  _const_spec((1, w)),
                  _const_spec(w_gates.shape),
                  _const_spec(b_gates.shape),
                  _const_spec((1, w)),
                  _const_spec(w_out.shape)],
        out_specs=pl.BlockSpec((b, tm, d), lambda t: (0, t, 0)),
        out_shape=jax.ShapeDtypeStruct(x.shape, F32),
        scratch_shapes=[pltpu.VMEM((b, CONV_WIDTH - 1, SUBLANES, w), F32),
                        pltpu.VMEM((b, SUBLANES, w), F32)],
        compiler_params=_cparams(1),
        name="rglru",
    )(x, mod, gain.reshape(1, d), perm, perm_t, w_in, conv_w, conv_b.reshape(1, w), w_gates, b_gates,
      lam.reshape(1, w), w_out)


def _ffn_kernel(*refs, has_attn, final):
    x_ref, mod_ref, g_ref, wg_ref, wu_ref, wd_ref = refs[:6]
    rest = list(refs[6:])
    att_ref = wo_ref = fn_ref = None
    if has_attn:
        att_ref, wo_ref = rest[:2]
        rest = rest[2:]
    if final:
        fn_ref = rest[0]
        rest = rest[1:]
    o_ref, = rest
    d = x_ref.shape[-1]
    ff = wg_ref.shape[-1]
    tm = x_ref.shape[1]

    xt = x_ref[0]
    mod = mod_ref[0]
    if has_attn:
        xt = xt + mod[:, 2 * d:3 * d] * _dot(att_ref[0], wo_ref[...])
    h = _norm_mod(xt, g_ref[...], mod[:, 3 * d:4 * d], mod[:, 4 * d:5 * d]).astype(BF16)
    acc = jnp.zeros((tm, d), F32)
    for c in range(ff // FF_CHUNK):
        lo = c * FF_CHUNK
        hi = lo + FF_CHUNK
        gate = _dot(h, wg_ref[:, lo:hi])
        up = _dot(h, wu_ref[:, lo:hi])
        act = (gate * jax.nn.sigmoid(gate)) * up
        acc = acc + _dot(act.astype(BF16), wd_ref[lo:hi, :])
    y = xt + mod[:, 5 * d:6 * d] * acc
    if final:
        y = y * lax.rsqrt(jnp.mean(y * y, axis=-1, keepdims=True) + RMS_EPS) * fn_ref[...]
    o_ref[0] = y


def _ffn_layer(x, mod, gain, w_gate, w_up, w_down, attn=None, w_o=None, final_gain=None):
    b, s, d = x.shape
    tm = TM_FFN
    has_attn = attn is not None
    final = final_gain is not None
    in_specs = [pl.BlockSpec((1, tm, d), lambda i, t: (i, t, 0)),
                pl.BlockSpec((1, 1, mod.shape[-1]), lambda i, t: (i, 0, 0)),
                _const_spec((1, d)),
                _const_spec(w_gate.shape),
                _const_spec(w_up.shape),
                _const_spec(w_down.shape)]
    args = [x, mod, gain.reshape(1, d), w_gate, w_up, w_down]
    if has_attn:
        in_specs += [pl.BlockSpec((1, tm, d), lambda i, t: (i, t, 0)), _const_spec(w_o.shape)]
        args += [attn, w_o]
    if final:
        in_specs += [_const_spec((1, d))]
        args += [final_gain.reshape(1, d)]
    return pl.pallas_call(
        functools.partial(_ffn_kernel, has_attn=has_attn, final=final),
        grid=(b, s // tm),
        in_specs=in_specs,
        out_specs=pl.BlockSpec((1, tm, d), lambda i, t: (i, t, 0)),
        out_shape=jax.ShapeDtypeStruct(x.shape, F32),
        compiler_params=_cparams(2),
        name="ffn_attn_final" if has_attn else "ffn",
    )(*args)


def _qkv_kernel(x_ref, xn_ref, kvmod_ref, mod_ref, kvg_ref, qg_ref, wk_ref, wvt_ref, wq_ref,
                q_ref, k_ref, vt_ref, mask_ref, km_scr, hqa_scr, hkva_scr, hqb_scr, hkvb_scr):
    nbatch = x_ref.shape[0]
    d = x_ref.shape[-1]
    tm = MOBA_BLOCK
    nb = mask_ref.shape[3]
    step = pl.program_id(0)
    buf_a = (hqa_scr, hkva_scr)
    buf_b = (hqb_scr, hkvb_scr)
    blk = lax.broadcasted_iota(jnp.int32, (nb, tm), 0).astype(F32)
    q_scale = HEAD_DIM ** -0.5 * LOG2E
    pair_w = KV_GROUP * HEAD_DIM

    def normalize(src_ref, row0, bufs):
        for bi in range(nbatch):
            xt = src_ref[bi, row0:row0 + tm, :]
            xn = xt * lax.rsqrt(jnp.mean(xt * xt, axis=-1, keepdims=True) + RMS_EPS)
            mod = mod_ref[bi]
            kvmod = kvmod_ref[bi]
            rows = pl.ds(bi * tm, tm)
            bufs[0][rows, :] = (xn * (qg_ref[...] * (1.0 + mod[:, d:2 * d])) + mod[:, 0:d]).astype(BF16)
            bufs[1][rows, :] = (xn * (kvg_ref[...] * (1.0 + kvmod[:, d:2 * d])) + kvmod[:, 0:d]).astype(BF16)

    def process(half, cur, nxt_src, nxt_row0, nxt):
        j = 2 * step + half
        past = blk < j.astype(F32)
        out_rows = slice(half * tm, (half + 1) * tm)
        q_pair = {}
        h_q = cur[0][...]
        h_kv = cur[1][...]
        normalize(nxt_src, nxt_row0, nxt)

        def q_proj(hk):
            qp = _dot(h_q, wq_ref[:, hk * pair_w:(hk + 1) * pair_w]) * q_scale
            for bi in range(nbatch):
                q_ref[bi, out_rows, hk * pair_w:(hk + 1) * pair_w] = qp[bi * tm:(bi + 1) * tm].astype(BF16)
            q_pair[hk] = qp

        def select_blocks(bi, hq):
            hk, sub = divmod(hq, KV_GROUP)
            q_h = q_pair[hk][bi * tm:(bi + 1) * tm, sub * HEAD_DIM:(sub + 1) * HEAD_DIM]
            km_h = km_scr[bi, :, hk * HEAD_DIM:(hk + 1) * HEAD_DIM]
            gate = _dot_nt(km_h.astype(BF16), q_h.astype(BF16))
            gate = jnp.where(past, gate, NEG_INF)
            mask = jnp.full((nb, tm), NEG_INF, F32)
            for _ in range(MOBA_TOPK):
                best = jnp.max(gate, axis=0, keepdims=True)
                first = jnp.min(jnp.where(gate == best, blk, nb), axis=0, keepdims=True)
                hit = blk == first
                mask = jnp.where(hit, 0.0, mask)
                gate = jnp.where(hit, -jnp.inf, gate)
            mask = jnp.where(past, mask, NEG_INF)
            mask_ref[bi, hk, half, :, sub * tm:(sub + 1) * tm] = mask

        def k_proj():
            k = _dot(h_kv, wk_ref[...])
            for bi in range(nbatch):
                kb = k[bi * tm:(bi + 1) * tm]
                for hk in range(N_KV_HEADS):
                    k_ref[bi, hk, half] = kb[:, hk * HEAD_DIM:(hk + 1) * HEAD_DIM].astype(BF16)
            return k

        def v_proj():
            vt = _dot_nt(wvt_ref[...], h_kv)
            for bi in range(nbatch):
                for hk in range(N_KV_HEADS):
                    vt_ref[bi, hk, half] = vt[hk * HEAD_DIM:(hk + 1) * HEAD_DIM,
                                              bi * tm:(bi + 1) * tm].astype(BF16)

        k = None
        q_proj(0)
        for hk in range(N_KV_HEADS):
            if hk + 1 < N_KV_HEADS:
                q_proj(hk + 1)
            if hk == N_KV_HEADS - 2:
                k = k_proj()
            if hk == N_KV_HEADS - 1:
                v_proj()
            for bi in range(nbatch):
                for sub in range(KV_GROUP):
                    select_blocks(bi, hk * KV_GROUP + sub)
        for bi in range(nbatch):
            km_scr[bi, pl.ds(j, 1), :] = jnp.mean(k[bi * tm:(bi + 1) * tm], axis=0, keepdims=True)

    @pl.when(step == 0)
    def _():
        km_scr[...] = jnp.zeros_like(km_scr)
        normalize(x_ref, 0, buf_a)

    process(0, buf_a, x_ref, tm, buf_b)
    process(1, buf_b, xn_ref, 0, buf_a)


def _qkv_gate(x, kvmod, mod, kv_gain, q_gain, w_k, w_vt, w_q):
    b, s, d = x.shape
    tm = MOBA_BLOCK
    nb = s // tm
    dkv = N_KV_HEADS * HEAD_DIM
    dq = N_HEADS * HEAD_DIM
    out_shape = (jax.ShapeDtypeStruct((b, s, dq), BF16),
                 jax.ShapeDtypeStruct((b, N_KV_HEADS, nb, tm, HEAD_DIM), BF16),
                 jax.ShapeDtypeStruct((b, N_KV_HEADS, nb, HEAD_DIM, tm), BF16),
                 jax.ShapeDtypeStruct((b, N_KV_HEADS, nb, nb, KV_GROUP * tm), F32))
    out_specs = (pl.BlockSpec((b, 2 * tm, dq), lambda t: (0, t, 0)),
                 pl.BlockSpec((b, N_KV_HEADS, 2, tm, HEAD_DIM), lambda t: (0, 0, t, 0, 0)),
                 pl.BlockSpec((b, N_KV_HEADS, 2, HEAD_DIM, tm), lambda t: (0, 0, t, 0, 0)),
                 pl.BlockSpec((b, N_KV_HEADS, 2, nb, KV_GROUP * tm), lambda t: (0, 0, t, 0, 0)))
    return pl.pallas_call(
        _qkv_kernel,
        grid=(nb // 2,),
        in_specs=[pl.BlockSpec((b, 2 * tm, d), lambda t: (0, t, 0)),
                  pl.BlockSpec((b, tm, d), lambda t: (0, jnp.minimum(2 * t + 2, nb - 1), 0)),
                  _const_spec(kvmod.shape),
                  _const_spec(mod.shape),
                  _const_spec((1, d)),
                  _const_spec((1, d)),
                  _const_spec(w_k.shape),
                  _const_spec(w_vt.shape),
                  _const_spec(w_q.shape)],
        out_specs=out_specs,
        out_shape=out_shape,
        scratch_shapes=[pltpu.VMEM((b, nb, dkv), F32),
                        pltpu.VMEM((b * tm, d), BF16),
                        pltpu.VMEM((b * tm, d), BF16),
                        pltpu.VMEM((b * tm, d), BF16),
                        pltpu.VMEM((b * tm, d), BF16)],
        compiler_params=_cparams(1),
        name="qkv_gate",
    )(x, x, kvmod, mod, kv_gain.reshape(1, d), q_gain.reshape(1, d), w_k, w_vt, w_q)


def _attn_kernel(rb_ref, q_ref, qn_ref, k_ref, vt_ref, mask_ref, maskn_ref, tab_ref, o_ref, m_scr, l_scr,
                 acc_scr, sa_scr, sb_scr, cma_scr, cmb_scr, rowa_scr, rowb_scr, phase_scr):
    g0 = pl.program_id(1) * ATTN_STREAMS
    j = pl.program_id(2)
    nb = k_ref.shape[2]
    tq = q_ref.shape[1]
    pair_w = KV_GROUP * HEAD_DIM
    streams = range(ATTN_STREAMS)

    def stack_heads(qref, si):
        lo = si * pair_w
        return jnp.concatenate([qref[0, :, lo:lo + HEAD_DIM], qref[0, :, lo + HEAD_DIM:lo + pair_w]], axis=0)

    qs = [stack_heads(q_ref, si) for si in streams]

    def scores(si, n, qx):
        return _dot_nt(k_ref[0, si, n], qx)

    def tile_bias(si, lo):
        return jnp.concatenate([tab_ref[KV_GROUP * si, lo:lo + MOBA_BLOCK, :],
                                tab_ref[KV_GROUP * si + 1, lo:lo + MOBA_BLOCK, :]], axis=1)

    def mask_row(si, n, mref):
        return mref[0, si, 0, pl.ds(n, 1), :]

    def col_max(s):
        return jnp.max(s, axis=0, keepdims=True)

    def col_sum(s):
        return jnp.sum(s, axis=0, keepdims=True)

    lane = lax.broadcasted_iota(jnp.int32, (1, KV_GROUP * tq), 1)
    far_bias = [LOG2E * jnp.where(lane < tq, rb_ref[KV_GROUP * (g0 + si), REL_BUCKETS - 1],
                                  rb_ref[KV_GROUP * (g0 + si) + 1, REL_BUCKETS - 1]) for si in streams]

    def far_row(si, n, valid, mref):
        return jnp.where(valid, mask_row(si, n, mref) + far_bias[si], NEG_INF)

    n_far = jnp.maximum(j - 1 - NEAR_FAR, 0)
    n_groups = 1 + (n_far + GROUP - 1) // GROUP
    jp = jnp.maximum(j - 1, 0)
    head_blocks = [j, jp] + [jnp.maximum(j - 2 - u, 0) for u in range(NEAR_FAR)]

    def group_blocks(k):
        return [jnp.where(k == 0, head_blocks[u], jnp.minimum((k - 1) * GROUP + u, nb - 1))
                for u in range(GROUP)]

    def put(buf, si, u, s, row):
        s_buf, cm_buf, row_buf = buf
        s_buf[si, u] = s
        cm_buf[si, u:u + 1] = col_max(s)
        row_buf[si, u:u + 1] = row

    def head_slot(si, u, jq, qx, mref):
        if u == 0:
            return scores(si, jq, qx) + tile_bias(si, MOBA_BLOCK), jnp.zeros((1, KV_GROUP * tq), F32)
        if u == 1:
            jqp = jnp.maximum(jq - 1, 0)
            return (scores(si, jqp, qx) + tile_bias(si, 0),
                    jnp.where(jq >= 1, mask_row(si, jqp, mref), NEG_INF))
        n = jnp.maximum(jq - u, 0)
        return scores(si, n, qx), far_row(si, n, jq - u >= 0, mref)

    j_next = jnp.minimum(j + 1, nb - 1)

    def step(si, k, buf, nxt, last):
        s_buf, cm_buf, row_buf = buf
        blocks = group_blocks(k)
        m_old = m_scr[si, 0:1]
        rows = [row_buf[si, u:u + 1] for u in range(GROUP)]
        m_new = m_old
        for u in range(GROUP):
            m_new = jnp.maximum(m_new, cm_buf[si, u:u + 1] + rows[u])
        alpha = jnp.exp2(m_old - m_new)
        l_new = alpha * l_scr[si, 0:1]
        pv = None
        qs_next = stack_heads(qn_ref, si) if last else None

        def stage1(u):
            if last:
                put(nxt, si, u, *head_slot(si, u, j_next, qs_next, maskn_ref))
            else:
                n = k * GROUP + u
                nc = jnp.minimum(n, nb - 1)
                put(nxt, si, u, scores(si, nc, qs[si]), far_row(si, nc, n < n_far, mask_ref))

        for u in range(GROUP):
            stage1(u)
            p = jnp.exp2(s_buf[si, u] + (rows[u] - m_new))
            l_new = l_new + col_sum(p)
            d = _dot(vt_ref[0, si, blocks[u]], p.astype(BF16))
            pv = d if pv is None else pv + d
        acc_scr[si] = alpha * acc_scr[si] + pv
        l_scr[si, 0:1] = l_new
        m_scr[si, 0:1] = m_new

    buf_a = (sa_scr, cma_scr, rowa_scr)
    buf_b = (sb_scr, cmb_scr, rowb_scr)
    m_scr[...] = jnp.full_like(m_scr, NEG_INF)
    l_scr[...] = jnp.zeros_like(l_scr)
    acc_scr[...] = jnp.zeros_like(acc_scr)

    @pl.when(j == 0)
    def _():
        for si in streams:
            for u in range(GROUP):
                put(buf_a, si, u, *head_slot(si, u, j, qs[si], mask_ref))
        phase_scr[0] = 0

    phase = phase_scr[0]

    def run(k, buf, nxt):
        @pl.when((k >= 0) & (k + 1 < n_groups))
        def _():
            for si in streams:
                step(si, k, buf, nxt, last=False)

        @pl.when((k >= 0) & (k + 1 == n_groups))
        def _():
            for si in streams:
                step(si, k, buf, nxt, last=True)

    def pair(t, carry):
        run(2 * t - phase, buf_a, buf_b)
        run(2 * t + 1 - phase, buf_b, buf_a)
        return carry

    lax.fori_loop(0, (n_groups + phase + 1) // 2, pair, 0)
    phase_scr[0] = (n_groups + phase) % 2

    for si in streams:
        o = acc_scr[si] * (1.0 / l_scr[si, 0:1])
        o_ref[0, :, si * pair_w:(si + 1) * pair_w] = jnp.concatenate(
            [o[:, :tq].T, o[:, tq:].T], axis=1).astype(BF16)


def _moba_attn(rel_bias, q, k, vt, mask, table):
    b, s, dq = q.shape
    nb = k.shape[2]
    tq = MOBA_BLOCK
    ns = ATTN_STREAMS
    gw = ns * KV_GROUP * HEAD_DIM
    lanes = KV_GROUP * tq
    return pl.pallas_call(
        _attn_kernel,
        grid=(b, N_KV_HEADS // ns, nb),
        in_specs=[pl.BlockSpec(memory_space=pltpu.SMEM),
                  pl.BlockSpec((1, tq, gw), lambda i, g, j: (i, j, g)),
                  pl.BlockSpec((1, tq, gw), lambda i, g, j: (i, jnp.minimum(j + 1, nb - 1), g)),
                  pl.BlockSpec((1, ns, nb, tq, HEAD_DIM), lambda i, g, j: (i, g, 0, 0, 0),
                               pipeline_mode=pl.Buffered(1)),
                  pl.BlockSpec((1, ns, nb, HEAD_DIM, tq), lambda i, g, j: (i, g, 0, 0, 0),
                               pipeline_mode=pl.Buffered(1)),
                  pl.BlockSpec((1, ns, 1, nb, lanes), lambda i, g, j: (i, g, j, 0, 0)),
                  pl.BlockSpec((1, ns, 1, nb, lanes),
                               lambda i, g, j: (i, g, jnp.minimum(j + 1, nb - 1), 0, 0)),
                  pl.BlockSpec((ns * KV_GROUP, 2 * MOBA_BLOCK, MOBA_BLOCK), lambda i, g, j: (g, 0, 0))],
        out_specs=pl.BlockSpec((1, tq, gw), lambda i, g, j: (i, j, g)),
        out_shape=jax.ShapeDtypeStruct((b, s, dq), BF16),
        scratch_shapes=[pltpu.VMEM((ns, SUBLANES, lanes), F32),
                        pltpu.VMEM((ns, SUBLANES, lanes), F32),
                        pltpu.VMEM((ns, HEAD_DIM, lanes), F32),
                        pltpu.VMEM((ns, GROUP, MOBA_BLOCK, lanes), F32),
                        pltpu.VMEM((ns, GROUP, MOBA_BLOCK, lanes), F32),
                        pltpu.VMEM((ns, SUBLANES, lanes), F32),
                        pltpu.VMEM((ns, SUBLANES, lanes), F32),
                        pltpu.VMEM((ns, SUBLANES, lanes), F32),
                        pltpu.VMEM((ns, SUBLANES, lanes), F32),
                        pltpu.SMEM((1,), jnp.int32)],
        compiler_params=_cparams(3),
        name="moba_attn",
    )(rel_bias.astype(F32), q, q, k, vt, mask, mask, table)


def kernel(x, c, mod_w, mod_b, norm_mix, norm_ffn, lru_w_in, lru_conv_w, lru_conv_b, lru_w_gates,
           lru_b_gates, lru_lambda, lru_w_out, kv_mod_w, kv_mod_b, kv_norm, w_kv, attn_w_q, attn_w_o,
           rel_bias, ffn_w_gate, ffn_w_up, ffn_w_down, final_norm):
    b, s, d = x.shape
    assert s % TM_FFN == 0 and s % MOBA_BLOCK == 0 and b <= SUBLANES
    dkv = N_KV_HEADS * HEAD_DIM

    c_pad = jnp.zeros((SUBLANES, d), F32).at[:b].set(c.astype(F32))
    mod = _adaln_mod(c_pad, mod_w, mod_b)[:, :b]
    kvmod = _adaln_mod(c_pad, kv_mod_w[None], kv_mod_b[None])[0, :b]
    mod0 = mod[0][:, None, :]
    mod1 = mod[1][:, None, :]
    kvmod = kvmod[:, None, :]

    x = _rglru_layer(x, mod0, norm_mix[0], lru_w_in[0].astype(BF16), lru_conv_w[0], lru_conv_b[0],
                     lru_w_gates[0].astype(BF16), lru_b_gates[0], lru_lambda[0], lru_w_out[0].astype(BF16))
    x = _ffn_layer(x, mod0, norm_ffn[0], ffn_w_gate[0].astype(BF16), ffn_w_up[0].astype(BF16),
                   ffn_w_down[0].astype(BF16))

    q, k, vt, mask = _qkv_gate(x, kvmod, mod1, kv_norm, norm_mix[1], w_kv[:, :dkv].astype(BF16),
                               w_kv[:, dkv:].T.astype(BF16), attn_w_q[0].astype(BF16))
    table = _t5_table(rel_bias)
    attn = _moba_attn(rel_bias, q, k, vt, mask, table)
    return _ffn_layer(x, mod1, norm_ffn[1], ffn_w_gate[1].astype(BF16), ffn_w_up[1].astype(BF16),
                      ffn_w_down[1].astype(BF16), attn=attn, w_o=attn_w_o[0].astype(BF16),
                      final_gain=final_norm)
```

```python
import functools
import math

import numpy as np
import jax
import jax.numpy as jnp
from jax import lax
from jax.experimental import pallas as pl
from jax.experimental.pallas import tpu as pltpu

F32 = jnp.float32
BF16 = jnp.bfloat16
HIGHEST = lax.Precision.HIGHEST

LRU_BLOCK = 256
CONV_WIDTH = 4
LRU_C = 8.0
N_HEADS = 8
N_KV_HEADS = 4
HEAD_DIM = 128
KV_GROUP = N_HEADS // N_KV_HEADS
MOBA_BLOCK = 256
MOBA_TOPK = 3
REL_BUCKETS = 32
REL_MAX_DIST = 128
RMS_EPS = 1e-6
NEG_INF = -1e30
LOG2E = math.log2(math.e)
GELU_C1 = 2.0 * math.sqrt(2.0 / math.pi)
GELU_C2 = GELU_C1 * 0.044715

SUBLANES = 8
LANES = 128
VMEM_LIMIT = 56 * 1024 * 1024

TM_LRU = 256
TM_FFN = 512
FF_CHUNK = 256
MOD_TN = 2048
GROUP = 4
NEAR_FAR = GROUP - 2
ATTN_STREAMS = 4


def _cparams(n_axes):
    return pltpu.CompilerParams(dimension_semantics=("arbitrary",) * n_axes,
                                vmem_limit_bytes=VMEM_LIMIT)


def _const_spec(shape):
    nd = len(shape)
    return pl.BlockSpec(shape, lambda *_: (0,) * nd, pipeline_mode=pl.Buffered(1))


def _dot(a, b):
    return jnp.dot(a, b, preferred_element_type=F32)


def _dot_nt(a, b, precision=None):
    return lax.dot_general(a, b, (((1,), (1,)), ((), ())), preferred_element_type=F32,
                           precision=precision)


def _norm_mod(xt, gain, shift, scale):
    inv = lax.rsqrt(jnp.mean(xt * xt, axis=-1, keepdims=True) + RMS_EPS)
    return (xt * inv) * (gain * (1.0 + scale)) + shift


def _mod_kernel(c_ref, w_ref, b_ref, o_ref):
    c = c_ref[...]
    cs = c * jax.nn.sigmoid(c)
    o_ref[0] = jnp.dot(cs, w_ref[0], preferred_element_type=F32, precision=HIGHEST) + b_ref[0]


def _adaln_mod(c_pad, w, b):
    n_layers, d, n = w.shape
    tn = min(MOD_TN, n)
    return pl.pallas_call(
        _mod_kernel,
        grid=(n_layers, n // tn),
        in_specs=[pl.BlockSpec((SUBLANES, d), lambda l, j: (0, 0)),
                  pl.BlockSpec((1, d, tn), lambda l, j: (l, 0, j)),
                  pl.BlockSpec((1, 1, tn), lambda l, j: (l, 0, j))],
        out_specs=pl.BlockSpec((1, SUBLANES, tn), lambda l, j: (l, 0, j)),
        out_shape=jax.ShapeDtypeStruct((n_layers, SUBLANES, n), F32),
        compiler_params=_cparams(2),
        name="adaln_mod",
    )(c_pad, w, b.reshape(n_layers, 1, n))


def _t5_thresholds():
    max_exact = REL_BUCKETS // 2
    d = np.arange(0, 4 * REL_MAX_DIST)
    dd = np.maximum(d, 1).astype(np.float32)
    large = max_exact + (np.log(dd / max_exact) / math.log(REL_MAX_DIST / max_exact)
                         * (REL_BUCKETS - max_exact)).astype(np.int32)
    large = np.minimum(large, REL_BUCKETS - 1)
    bucket = np.where(d < max_exact, d, large)
    return [int(np.argmax(bucket >= b)) for b in range(REL_BUCKETS)]


def _t5_kernel(rb_ref, o_ref, *, thr):
    h = pl.program_id(0)
    shape = (2 * MOBA_BLOCK, MOBA_BLOCK)
    key = lax.broadcasted_iota(jnp.int32, shape, 0)
    qry = lax.broadcasted_iota(jnp.int32, shape, 1)
    dist = qry + MOBA_BLOCK - key
    val = jnp.full(shape, rb_ref[h, REL_BUCKETS - 1], F32)
    for b in range(REL_BUCKETS - 2, -1, -1):
        val = jnp.where(dist < thr[b + 1], rb_ref[h, b], val)
    o_ref[0] = jnp.where(dist < 0, NEG_INF, LOG2E * val)


def _t5_table(rel_bias):
    return pl.pallas_call(
        functools.partial(_t5_kernel, thr=_t5_thresholds()),
        grid=(N_HEADS,),
        in_specs=[pl.BlockSpec(memory_space=pltpu.SMEM)],
        out_specs=pl.BlockSpec((1, 2 * MOBA_BLOCK, MOBA_BLOCK), lambda h: (h, 0, 0)),
        out_shape=jax.ShapeDtypeStruct((N_HEADS, 2 * MOBA_BLOCK, MOBA_BLOCK), F32),
        compiler_params=_cparams(1),
        name="t5_table",
    )(rel_bias.astype(F32))


def _time_permutation(tm):
    seg = tm // SUBLANES
    row = np.arange(tm)
    perm = np.zeros((tm, tm), np.float32)
    perm[row, (row % SUBLANES) * seg + row // SUBLANES] = 1.0
    return jnp.asarray(perm, BF16), jnp.asarray(perm.T, BF16)


def _vrow(v, g):
    return v[g * SUBLANES:(g + 1) * SUBLANES]


def _sublane_scan(a, b, h0):
    sub = lax.broadcasted_iota(jnp.int32, a.shape, 0)
    s = 1
    while s < SUBLANES:
        a_sh = jnp.where(sub < s, 1.0, pltpu.roll(a, s, 0))
        b_sh = jnp.where(sub < s, 0.0, pltpu.roll(b, s, 0))
        b = a * b_sh + b
        a = a * a_sh
        s *= 2
    return a * h0 + b


def _rglru_kernel(x_ref, mod_ref, g_ref, perm_ref, permt_ref, win_ref, cw_ref, cb_ref, wg_ref, bg_ref,
                  lam_ref, wout_ref, o_ref, conv_scr, h_scr):
    nbatch = x_ref.shape[0]
    d = x_ref.shape[-1]
    w = lam_ref.shape[-1]
    tm = x_ref.shape[1]
    seg = tm // SUBLANES
    halo = CONV_WIDTH - 1

    @pl.when(pl.program_id(0) == 0)
    def _():
        conv_scr[...] = jnp.zeros_like(conv_scr)
        h_scr[...] = jnp.zeros_like(h_scr)

    sub = lax.broadcasted_iota(jnp.int32, (SUBLANES, LRU_BLOCK), 0)
    blocks = [(hd * LRU_BLOCK, (hd + 1) * LRU_BLOCK) for hd in range(w // LRU_BLOCK)]
    streams = [(bi, hd) for bi in range(nbatch) for hd in range(len(blocks))]
    hp, uy, ux, xc, gates, hy = {}, {}, {}, {}, {}, {}

    def normalize(bi):
        mod = mod_ref[bi]
        h = _norm_mod(x_ref[bi], g_ref[...], mod[:, 0:d], mod[:, d:2 * d]).astype(BF16)
        hp[bi] = _dot(perm_ref[...], h).astype(BF16)

    def in_proj(bi, hd):
        lo, hi = blocks[hd]
        uy[bi, hd] = _dot(hp[bi], win_ref[:, lo:hi])
        ux[bi, hd] = _dot(hp[bi], win_ref[:, w + lo:w + hi])

    def conv(bi, hd):
        lo, hi = blocks[hd]
        u = ux[bi, hd]
        tops = []
        for i in range(halo):
            cur = pltpu.roll(_vrow(u, seg - halo + i), 1, 0)
            prv = pltpu.roll(conv_scr[bi, i, :, lo:hi], 1, 0)
            tops.append(jnp.where(sub == 0, prv, cur))
            conv_scr[bi, i, :, lo:hi] = _vrow(u, seg - halo + i)
        cw = cw_ref[:, lo:hi]
        v = u * cw[CONV_WIDTH - 1:CONV_WIDTH]
        for k in range(1, CONV_WIDTH):
            shifted = jnp.concatenate(tops[halo - k:] + [u[:tm - k * SUBLANES]], axis=0)
            v = v + shifted * cw[CONV_WIDTH - 1 - k:CONV_WIDTH - k]
        xc[bi, hd] = v + cb_ref[:, lo:hi]

    def gate_proj(bi, hd):
        gates[bi, hd] = _dot(xc[bi, hd].astype(BF16), wg_ref[hd])

    def recur(bi, hd):
        lo, hi = blocks[hd]
        u, v, gt = uy[bi, hd], xc[bi, hd], gates[bi, hd]
        y = u * jax.nn.sigmoid(u * (GELU_C1 + GELU_C2 * (u * u)))
        r = jax.nn.sigmoid(gt[:, :LRU_BLOCK] + bg_ref[0:1, lo:hi])
        i_gate = jax.nn.sigmoid(gt[:, LRU_BLOCK:] + bg_ref[1:2, lo:hi])
        lam = lam_ref[:, lo:hi]
        softplus_neg = jnp.maximum(-lam, 0.0) + jnp.log1p(jnp.exp(-jnp.abs(lam)))
        neg_log_a = r * (LRU_C * softplus_neg)
        a = jnp.exp(-neg_log_a)
        one_minus_a2 = (1.0 + a * a) * jnp.tanh(neg_log_a)
        mult = jnp.where(one_minus_a2 == 0.0, 0.0, one_minus_a2 * lax.rsqrt(one_minus_a2))
        b = mult * (i_gate * v)
        loc = [_vrow(b, 0)]
        prod = [_vrow(a, 0)]
        for g in range(1, seg):
            ag = _vrow(a, g)
            loc.append(ag * loc[-1] + _vrow(b, g))
            prod.append(ag * prod[-1])
        h_in = h_scr[bi, 0:1, lo:hi]
        seg_end = _sublane_scan(prod[-1], loc[-1], h_in)
        seg_in = jnp.where(sub == 0, h_in, pltpu.roll(seg_end, 1, 0))
        h_scr[bi, 0:1, lo:hi] = seg_end[SUBLANES - 1:SUBLANES]
        hs = jnp.concatenate([loc[g] + prod[g] * seg_in for g in range(seg)], axis=0)
        hy[bi, hd] = (hs * y).astype(BF16)

    def out_proj(bi):
        hy_seg = jnp.concatenate([hy[bi, hd] for hd in range(len(blocks))], axis=1)
        hy_time = _dot(permt_ref[...], hy_seg).astype(BF16)
        o_ref[bi] = x_ref[bi] + mod_ref[bi][:, 2 * d:3 * d] * _dot(hy_time, wout_ref[...])

    stages = (in_proj, conv, gate_proj, recur)
    normalize(0)
    for slot in range(len(streams) + len(stages) - 1):
        for depth, stage in enumerate(stages):
            i = slot - depth
            if 0 <= i < len(streams):
                bi, hd = streams[i]
                if stage is in_proj and hd == len(blocks) - 2 and bi + 1 < nbatch:
                    normalize(bi + 1)
                stage(bi, hd)
                if stage is recur and hd == len(blocks) - 1:
                    out_proj(bi)


def _rglru_layer(x, mod, gain, w_in, conv_w, conv_b, w_gates, b_gates, lam, w_out):
    b, s, d = x.shape
    w = lam.shape[-1]
    tm = TM_LRU
    perm, perm_t = _time_permutation(tm)
    return pl.pallas_call(
        _rglru_kernel,
        grid=(s // tm,),
        in_specs=[pl.BlockSpec((b, tm, d), lambda t: (0, t, 0)),
                  _const_spec(mod.shape),
                  _const_spec((1, d)),
                  _const_spec((tm, tm)),
                  _const_spec((tm, tm)),
                  _const_spec(w_in.shape),
                  _const_spec(conv_w.shape),
                  _const_spec((1, w)),
                  _const_spec(w_gates.shape),
                  _const_spec(b_gates.shape),
                  _const_spec((1, w)),
                  _const_spec(w_out.shape)],
        out_specs=pl.BlockSpec((b, tm, d), lambda t: (0, t, 0)),
        out_shape=jax.ShapeDtypeStruct(x.shape, F32),
        scratch_shapes=[pltpu.VMEM((b, CONV_WIDTH - 1, SUBLANES, w), F32),
                        pltpu.VMEM((b, SUBLANES, w), F32)],
        compiler_params=_cparams(1),
        name="rglru",
    )(x, mod, gain.reshape(1, d), perm, perm_t, w_in, conv_w, conv_b.reshape(1, w), w_gates, b_gates,
      lam.reshape(1, w), w_out)


def _ffn_kernel(*refs, has_attn, final):
    x_ref, mod_ref, g_ref, wg_ref, wu_ref, wd_ref = refs[:6]
    rest = list(refs[6:])
    att_ref = wo_ref = fn_ref = None
    if has_attn:
        att_ref, wo_ref = rest[:2]
        rest = rest[2:]
    if final:
        fn_ref = rest[0]
        rest = rest[1:]
    o_ref, = rest
    d = x_ref.shape[-1]
    ff = wg_ref.shape[-1]
    tm = x_ref.shape[1]

    xt = x_ref[0]
    mod = mod_ref[0]
    if has_attn:
        xt = xt + mod[:, 2 * d:3 * d] * _dot(att_ref[0], wo_ref[...])
    h = _norm_mod(xt, g_ref[...], mod[:, 3 * d:4 * d], mod[:, 4 * d:5 * d]).astype(BF16)
    acc = jnp.zeros((tm, d), F32)
    for c in range(ff // FF_CHUNK):
        lo = c * FF_CHUNK
        hi = lo + FF_CHUNK
        gate = _dot(h, wg_ref[:, lo:hi])
        up = _dot(h, wu_ref[:, lo:hi])
        act = (gate * jax.nn.sigmoid(gate)) * up
        acc = acc + _dot(act.astype(BF16), wd_ref[lo:hi, :])
    y = xt + mod[:, 5 * d:6 * d] * acc
    if final:
        y = y * lax.rsqrt(jnp.mean(y * y, axis=-1, keepdims=True) + RMS_EPS) * fn_ref[...]
    o_ref[0] = y


def _ffn_layer(x, mod, gain, w_gate, w_up, w_down, attn=None, w_o=None, final_gain=None):
    b, s, d = x.shape
    tm = TM_FFN
    has_attn = attn is not None
    final = final_gain is not None
    in_specs = [pl.BlockSpec((1, tm, d), lambda i, t: (i, t, 0)),
                pl.BlockSpec((1, 1, mod.shape[-1]), lambda i, t: (i, 0, 0)),
                _const_spec((1, d)),
                _const_spec(w_gate.shape),
                _const_spec(w_up.shape),
                _const_spec(w_down.shape)]
    args = [x, mod, gain.reshape(1, d), w_gate, w_up, w_down]
    if has_attn:
        in_specs += [pl.BlockSpec((1, tm, d), lambda i, t: (i, t, 0)), _const_spec(w_o.shape)]
        args += [attn, w_o]
    if final:
        in_specs += [_const_spec((1, d))]
        args += [final_gain.reshape(1, d)]
    return pl.pallas_call(
        functools.partial(_ffn_kernel, has_attn=has_attn, final=final),
        grid=(b, s // tm),
        in_specs=in_specs,
        out_specs=pl.BlockSpec((1, tm, d), lambda i, t: (i, t, 0)),
        out_shape=jax.ShapeDtypeStruct(x.shape, F32),
        compiler_params=_cparams(2),
        name="ffn_attn_final" if has_attn else "ffn",
    )(*args)


def _qkv_kernel(x_ref, xn_ref, kvmod_ref, mod_ref, kvg_ref, qg_ref, wk_ref, wvt_ref, wq_ref,
                q_ref, k_ref, vt_ref, mask_ref, km_scr, hqa_scr, hkva_scr, hqb_scr, hkvb_scr):
    nbatch = x_ref.shape[0]
    d = x_ref.shape[-1]
    tm = MOBA_BLOCK
    nb = mask_ref.shape[3]
    step = pl.program_id(0)
    buf_a = (hqa_scr, hkva_scr)
    buf_b = (hqb_scr, hkvb_scr)
    blk = lax.broadcasted_iota(jnp.int32, (nb, tm), 0).astype(F32)
    q_scale = HEAD_DIM ** -0.5 * LOG2E
    pair_w = KV_GROUP * HEAD_DIM

    def normalize(src_ref, row0, bufs):
        for bi in range(nbatch):
            xt = src_ref[bi, row0:row0 + tm, :]
            xn = xt * lax.rsqrt(jnp.mean(xt * xt, axis=-1, keepdims=True) + RMS_EPS)
            mod = mod_ref[bi]
            kvmod = kvmod_ref[bi]
            rows = pl.ds(bi * tm, tm)
            bufs[0][rows, :] = (xn * (qg_ref[...] * (1.0 + mod[:, d:2 * d])) + mod[:, 0:d]).astype(BF16)
            bufs[1][rows, :] = (xn * (kvg_ref[...] * (1.0 + kvmod[:, d:2 * d])) + kvmod[:, 0:d]).astype(BF16)

    def process(half, cur, nxt_src, nxt_row0, nxt):
        j = 2 * step + half
        past = blk < j.astype(F32)
        out_rows = slice(half * tm, (half + 1) * tm)
        q_pair = {}
        h_q = cur[0][...]
        h_kv = cur[1][...]
        normalize(nxt_src, nxt_row0, nxt)

        def q_proj(hk):
            qp = _dot(h_q, wq_ref[:, hk * pair_w:(hk + 1) * pair_w]) * q_scale
            for bi in range(nbatch):
                q_ref[bi, out_rows, hk * pair_w:(hk + 1) * pair_w] = qp[bi * tm:(bi + 1) * tm].astype(BF16)
            q_pair[hk] = qp

        def select_blocks(bi, hq):
            hk, sub = divmod(hq, KV_GROUP)
            q_h = q_pair[hk][bi * tm:(bi + 1) * tm, sub * HEAD_DIM:(sub + 1) * HEAD_DIM]
            km_h = km_scr[bi, :, hk * HEAD_DIM:(hk + 1) * HEAD_DIM]
            gate = _dot_nt(km_h.astype(BF16), q_h.astype(BF16))
            gate = jnp.where(past, gate, NEG_INF)
            mask = jnp.full((nb, tm), NEG_INF, F32)
            for _ in range(MOBA_TOPK):
                best = jnp.max(gate, axis=0, keepdims=True)
                first = jnp.min(jnp.where(gate == best, blk, nb), axis=0, keepdims=True)
                hit = blk == first
                mask = jnp.where(hit, 0.0, mask)
                gate = jnp.where(hit, -jnp.inf, gate)
            mask = jnp.where(past, mask, NEG_INF)
            mask_ref[bi, hk, half, :, sub * tm:(sub + 1) * tm] = mask

        def k_proj():
            k = _dot(h_kv, wk_ref[...])
            for bi in range(nbatch):
                kb = k[bi * tm:(bi + 1) * tm]
                for hk in range(N_KV_HEADS):
                    k_ref[bi, hk, half] = kb[:, hk * HEAD_DIM:(hk + 1) * HEAD_DIM].astype(BF16)
            return k

        def v_proj():
            vt = _dot_nt(wvt_ref[...], h_kv)
            for bi in range(nbatch):
                for hk in range(N_KV_HEADS):
                    vt_ref[bi, hk, half] = vt[hk * HEAD_DIM:(hk + 1) * HEAD_DIM,
                                              bi * tm:(bi + 1) * tm].astype(BF16)

        k = None
        q_proj(0)
        for hk in range(N_KV_HEADS):
            if hk + 1 < N_KV_HEADS:
                q_proj(hk + 1)
            if hk == N_KV_HEADS - 2:
                k = k_proj()
            if hk == N_KV_HEADS - 1:
                v_proj()
            for bi in range(nbatch):
                for sub in range(KV_GROUP):
                    select_blocks(bi, hk * KV_GROUP + sub)
        for bi in range(nbatch):
            km_scr[bi, pl.ds(j, 1), :] = jnp.mean(k[bi * tm:(bi + 1) * tm], axis=0, keepdims=True)

    @pl.when(step == 0)
    def _():
        km_scr[...] = jnp.zeros_like(km_scr)
        normalize(x_ref, 0, buf_a)

    process(0, buf_a, x_ref, tm, buf_b)
    process(1, buf_b, xn_ref, 0, buf_a)


def _qkv_gate(x, kvmod, mod, kv_gain, q_gain, w_k, w_vt, w_q):
    b, s, d = x.shape
    tm = MOBA_BLOCK
    nb = s // tm
    dkv = N_KV_HEADS * HEAD_DIM
    dq = N_HEADS * HEAD_DIM
    out_shape = (jax.ShapeDtypeStruct((b, s, dq), BF16),
                 jax.ShapeDtypeStruct((b, N_KV_HEADS, nb, tm, HEAD_DIM), BF16),
                 jax.ShapeDtypeStruct((b, N_KV_HEADS, nb, HEAD_DIM, tm), BF16),
                 jax.ShapeDtypeStruct((b, N_KV_HEADS, nb, nb, KV_GROUP * tm), F32))
    out_specs = (pl.BlockSpec((b, 2 * tm, dq), lambda t: (0, t, 0)),
                 pl.BlockSpec((b, N_KV_HEADS, 2, tm, HEAD_DIM), lambda t: (0, 0, t, 0, 0)),
                 pl.BlockSpec((b, N_KV_HEADS, 2, HEAD_DIM, tm), lambda t: (0, 0, t, 0, 0)),
                 pl.BlockSpec((b, N_KV_HEADS, 2, nb, KV_GROUP * tm), lambda t: (0, 0, t, 0, 0)))
    return pl.pallas_call(
        _qkv_kernel,
        grid=(nb // 2,),
        in_specs=[pl.BlockSpec((b, 2 * tm, d), lambda t: (0, t, 0)),
                  pl.BlockSpec((b, tm, d), lambda t: (0, jnp.minimum(2 * t + 2, nb - 1), 0)),
                  _const_spec(kvmod.shape),
                  _const_spec(mod.shape),
                  _const_spec((1, d)),
                  _const_spec((1, d)),
                  _const_spec(w_k.shape),
                  _const_spec(w_vt.shape),
                  _const_spec(w_q.shape)],
        out_specs=out_specs,
        out_shape=out_shape,
        scratch_shapes=[pltpu.VMEM((b, nb, dkv), F32),
                        pltpu.VMEM((b * tm, d), BF16),
                        pltpu.VMEM((b * tm, d), BF16),
                        pltpu.VMEM((b * tm, d), BF16),
                        pltpu.VMEM((b * tm, d), BF16)],
        compiler_params=_cparams(1),
        name="qkv_gate",
    )(x, x, kvmod, mod, kv_gain.reshape(1, d), q_gain.reshape(1, d), w_k, w_vt, w_q)


def _attn_kernel(rb_ref, q_ref, qn_ref, k_ref, vt_ref, mask_ref, maskn_ref, tab_ref, o_ref, m_scr, l_scr,
                 acc_scr, sa_scr, sb_scr, cma_scr, cmb_scr, rowa_scr, rowb_scr, phase_scr):
    g0 = pl.program_id(1) * ATTN_STREAMS
    j = pl.program_id(2)
    nb = k_ref.shape[2]
    tq = q_ref.shape[1]
    pair_w = KV_GROUP * HEAD_DIM
    streams = range(ATTN_STREAMS)

    def stack_heads(qref, si):
        lo = si * pair_w
        return jnp.concatenate([qref[0, :, lo:lo + HEAD_DIM], qref[0, :, lo + HEAD_DIM:lo + pair_w]], axis=0)

    qs = [stack_heads(q_ref, si) for si in streams]

    def scores(si, n, qx):
        return _dot_nt(k_ref[0, si, n], qx)

    def tile_bias(si, lo):
        return jnp.concatenate([tab_ref[KV_GROUP * si, lo:lo + MOBA_BLOCK, :],
                                tab_ref[KV_GROUP * si + 1, lo:lo + MOBA_BLOCK, :]], axis=1)

    def mask_row(si, n, mref):
        return mref[0, si, 0, pl.ds(n, 1), :]

    def col_max(s):
        return jnp.max(s, axis=0, keepdims=True)

    def col_sum(s):
        return jnp.sum(s, axis=0, keepdims=True)

    lane = lax.broadcasted_iota(jnp.int32, (1, KV_GROUP * tq), 1)
    far_bias = [LOG2E * jnp.where(lane < tq, rb_ref[KV_GROUP * (g0 + si), REL_BUCKETS - 1],
                                  rb_ref[KV_GROUP * (g0 + si) + 1, REL_BUCKETS - 1]) for si in streams]

    def far_row(si, n, valid, mref):
        return jnp.where(valid, mask_row(si, n, mref) + far_bias[si], NEG_INF)

    n_far = jnp.maximum(j - 1 - NEAR_FAR, 0)
    n_groups = 1 + (n_far + GROUP - 1) // GROUP
    jp = jnp.maximum(j - 1, 0)
    head_blocks = [j, jp] + [jnp.maximum(j - 2 - u, 0) for u in range(NEAR_FAR)]

    def group_blocks(k):
        return [jnp.where(k == 0, head_blocks[u], jnp.minimum((k - 1) * GROUP + u, nb - 1))
                for u in range(GROUP)]

    def put(buf, si, u, s, row):
        s_buf, cm_buf, row_buf = buf
        s_buf[si, u] = s
        cm_buf[si, u:u + 1] = col_max(s)
        row_buf[si, u:u + 1] = row

    def head_slot(si, u, jq, qx, mref):
        if u == 0:
            return scores(si, jq, qx) + tile_bias(si, MOBA_BLOCK), jnp.zeros((1, KV_GROUP * tq), F32)
        if u == 1:
            jqp = jnp.maximum(jq - 1, 0)
            return (scores(si, jqp, qx) + tile_bias(si, 0),
                    jnp.where(jq >= 1, mask_row(si, jqp, mref), NEG_INF))
        n = jnp.maximum(jq - u, 0)
        return scores(si, n, qx), far_row(si, n, jq - u >= 0, mref)

    j_next = jnp.minimum(j + 1, nb - 1)

    def step(si, k, buf, nxt, last):
        s_buf, cm_buf, row_buf = buf
        blocks = group_blocks(k)
        m_old = m_scr[si, 0:1]
        rows = [row_buf[si, u:u + 1] for u in range(GROUP)]
        m_new = m_old
        for u in range(GROUP):
            m_new = jnp.maximum(m_new, cm_buf[si, u:u + 1] + rows[u])
        alpha = jnp.exp2(m_old - m_new)
        l_new = alpha * l_scr[si, 0:1]
        pv = None
        qs_next = stack_heads(qn_ref, si) if last else None

        def stage1(u):
            if last:
                put(nxt, si, u, *head_slot(si, u, j_next, qs_next, maskn_ref))
            else:
                n = k * GROUP + u
                nc = jnp.minimum(n, nb - 1)
                put(nxt, si, u, scores(si, nc, qs[si]), far_row(si, nc, n < n_far, mask_ref))

        for u in range(GROUP):
            stage1(u)
            shift = rows[u] - m_new
            halves = []
            for hq in range(KV_GROUP):
                cols = slice(hq * tq, (hq + 1) * tq)
                p = jnp.exp2(s_buf[si, u, :, cols] + shift[:, cols])
                halves.append((col_sum(p), _dot(vt_ref[0, si, blocks[u]], p.astype(BF16))))
            l_new = l_new + jnp.concatenate([h[0] for h in halves], axis=1)
            d = jnp.concatenate([h[1] for h in halves], axis=1)
            pv = d if pv is None else pv + d
        acc_scr[si] = alpha * acc_scr[si] + pv
        l_scr[si, 0:1] = l_new
        m_scr[si, 0:1] = m_new

    buf_a = (sa_scr, cma_scr, rowa_scr)
    buf_b = (sb_scr, cmb_scr, rowb_scr)
    m_scr[...] = jnp.full_like(m_scr, NEG_INF)
    l_scr[...] = jnp.zeros_like(l_scr)
    acc_scr[...] = jnp.zeros_like(acc_scr)

    @pl.when(j == 0)
    def _():
        for si in streams:
            for u in range(GROUP):
                put(buf_a, si, u, *head_slot(si, u, j, qs[si], mask_ref))
        phase_scr[0] = 0

    phase = phase_scr[0]

    def run(k, buf, nxt):
        @pl.when((k >= 0) & (k + 1 < n_groups))
        def _():
            for si in streams:
                step(si, k, buf, nxt, last=False)

        @pl.when((k >= 0) & (k + 1 == n_groups))
        def _():
            for si in streams:
                step(si, k, buf, nxt, last=True)

    def pair(t, carry):
        run(2 * t - phase, buf_a, buf_b)
        run(2 * t + 1 - phase, buf_b, buf_a)
        return carry

    lax.fori_loop(0, (n_groups + phase + 1) // 2, pair, 0)
    phase_scr[0] = (n_groups + phase) % 2

    for si in streams:
        o = acc_scr[si] * (1.0 / l_scr[si, 0:1])
        o_ref[0, :, si * pair_w:(si + 1) * pair_w] = jnp.concatenate(
            [o[:, :tq].T, o[:, tq:].T], axis=1).astype(BF16)


def _moba_attn(rel_bias, q, k, vt, mask, table):
    b, s, dq = q.shape
    nb = k.shape[2]
    tq = MOBA_BLOCK
    ns = ATTN_STREAMS
    gw = ns * KV_GROUP * HEAD_DIM
    lanes = KV_GROUP * tq
    return pl.pallas_call(
        _attn_kernel,
        grid=(b, N_KV_HEADS // ns, nb),
        in_specs=[pl.BlockSpec(memory_space=pltpu.SMEM),
                  pl.BlockSpec((1, tq, gw), lambda i, g, j: (i, j, g)),
                  pl.BlockSpec((1, tq, gw), lambda i, g, j: (i, jnp.minimum(j + 1, nb - 1), g)),
                  pl.BlockSpec((1, ns, nb, tq, HEAD_DIM), lambda i, g, j: (i, g, 0, 0, 0),
                               pipeline_mode=pl.Buffered(1)),
                  pl.BlockSpec((1, ns, nb, HEAD_DIM, tq), lambda i, g, j: (i, g, 0, 0, 0),
                               pipeline_mode=pl.Buffered(1)),
                  pl.BlockSpec((1, ns, 1, nb, lanes), lambda i, g, j: (i, g, j, 0, 0)),
                  pl.BlockSpec((1, ns, 1, nb, lanes),
                               lambda i, g, j: (i, g, jnp.minimum(j + 1, nb - 1), 0, 0)),
                  pl.BlockSpec((ns * KV_GROUP, 2 * MOBA_BLOCK, MOBA_BLOCK), lambda i, g, j: (g, 0, 0))],
        out_specs=pl.BlockSpec((1, tq, gw), lambda i, g, j: (i, j, g)),
        out_shape=jax.ShapeDtypeStruct((b, s, dq), BF16),
        scratch_shapes=[pltpu.VMEM((ns, SUBLANES, lanes), F32),
                        pltpu.VMEM((ns, SUBLANES, lanes), F32),
                        pltpu.VMEM((ns, HEAD_DIM, lanes), F32),
                        pltpu.VMEM((ns, GROUP, MOBA_BLOCK, lanes), F32),
                        pltpu.VMEM((ns, GROUP, MOBA_BLOCK, lanes), F32),
                        pltpu.VMEM((ns, SUBLANES, lanes), F32),
                        pltpu.VMEM((ns, SUBLANES, lanes), F32),
                        pltpu.VMEM((ns, SUBLANES, lanes), F32),
                        pltpu.VMEM((ns, SUBLANES, lanes), F32),
                        pltpu.SMEM((1,), jnp.int32)],
        compiler_params=_cparams(3),
        name="moba_attn",
    )(rel_bias.astype(F32), q, q, k, vt, mask, mask, table)


def kernel(x, c, mod_w, mod_b, norm_mix, norm_ffn, lru_w_in, lru_conv_w, lru_conv_b, lru_w_gates,
           lru_b_gates, lru_lambda, lru_w_out, kv_mod_w, kv_mod_b, kv_norm, w_kv, attn_w_q, attn_w_o,
           rel_bias, ffn_w_gate, ffn_w_up, ffn_w_down, final_norm):
    b, s, d = x.shape
    assert s % TM_FFN == 0 and s % MOBA_BLOCK == 0 and b <= SUBLANES
    dkv = N_KV_HEADS * HEAD_DIM

    c_pad = jnp.zeros((SUBLANES, d), F32).at[:b].set(c.astype(F32))
    mod = _adaln_mod(c_pad, mod_w, mod_b)[:, :b]
    kvmod = _adaln_mod(c_pad, kv_mod_w[None], kv_mod_b[None])[0, :b]
    mod0 = mod[0][:, None, :]
    mod1 = mod[1][:, None, :]
    kvmod = kvmod[:, None, :]

    x = _rglru_layer(x, mod0, norm_mix[0], lru_w_in[0].astype(BF16), lru_conv_w[0], lru_conv_b[0],
                     lru_w_gates[0].astype(BF16), lru_b_gates[0], lru_lambda[0], lru_w_out[0].astype(BF16))
    x = _ffn_layer(x, mod0, norm_ffn[0], ffn_w_gate[0].astype(BF16), ffn_w_up[0].astype(BF16),
                   ffn_w_down[0].astype(BF16))

    q, k, vt, mask = _qkv_gate(x, kvmod, mod1, kv_norm, norm_mix[1], w_kv[:, :dkv].astype(BF16),
                               w_kv[:, dkv:].T.astype(BF16), attn_w_q[0].astype(BF16))
    table = _t5_table(rel_bias)
    attn = _moba_attn(rel_bias, q, k, vt, mask, table)
    return _ffn_layer(x, mod1, norm_ffn[1], ffn_w_gate[1].astype(BF16), ffn_w_up[1].astype(BF16),
                      ffn_w_down[1].astype(BF16), attn=attn, w_o=attn_w_o[0].astype(BF16),
                      final_gain=final_norm)
```

```python
import functools
import math

import numpy as np
import jax
import jax.numpy as jnp
from jax import lax
from jax.experimental import pallas as pl
from jax.experimental.pallas import tpu as pltpu

F32 = jnp.float32
BF16 = jnp.bfloat16
HIGHEST = lax.Precision.HIGHEST

LRU_BLOCK = 256
CONV_WIDTH = 4
LRU_C = 8.0
N_HEADS = 8
N_KV_HEADS = 4
HEAD_DIM = 128
KV_GROUP = N_HEADS // N_KV_HEADS
MOBA_BLOCK = 256
MOBA_TOPK = 3
REL_BUCKETS = 32
REL_MAX_DIST = 128
RMS_EPS = 1e-6
NEG_INF = -1e30
LOG2E = math.log2(math.e)
GELU_C1 = 2.0 * math.sqrt(2.0 / math.pi)
GELU_C2 = GELU_C1 * 0.044715

SUBLANES = 8
LANES = 128
VMEM_LIMIT = 56 * 1024 * 1024

TM_LRU = 256
TM_FFN = 1024
FF_CHUNK = 256
MOD_TN = 2048
GROUP = 4
NEAR_FAR = GROUP - 2
ATTN_STREAMS = 4


def _cparams(n_axes):
    return pltpu.CompilerParams(dimension_semantics=("arbitrary",) * n_axes,
                                vmem_limit_bytes=VMEM_LIMIT)


def _const_spec(shape):
    nd = len(shape)
    return pl.BlockSpec(shape, lambda *_: (0,) * nd, pipeline_mode=pl.Buffered(1))


def _dot(a, b):
    return jnp.dot(a, b, preferred_element_type=F32)


def _dot_nt(a, b, precision=None):
    return lax.dot_general(a, b, (((1,), (1,)), ((), ())), preferred_element_type=F32,
                           precision=precision)


def _norm_mod(xt, gain, shift, scale):
    inv = lax.rsqrt(jnp.mean(xt * xt, axis=-1, keepdims=True) + RMS_EPS)
    return (xt * inv) * (gain * (1.0 + scale)) + shift


def _mod_kernel(c_ref, w_ref, b_ref, o_ref):
    c = c_ref[...]
    cs = c * jax.nn.sigmoid(c)
    o_ref[0] = jnp.dot(cs, w_ref[0], preferred_element_type=F32, precision=HIGHEST) + b_ref[0]


def _adaln_mod(c_pad, w, b):
    n_layers, d, n = w.shape
    tn = min(MOD_TN, n)
    return pl.pallas_call(
        _mod_kernel,
        grid=(n_layers, n // tn),
        in_specs=[pl.BlockSpec((SUBLANES, d), lambda l, j: (0, 0)),
                  pl.BlockSpec((1, d, tn), lambda l, j: (l, 0, j)),
                  pl.BlockSpec((1, 1, tn), lambda l, j: (l, 0, j))],
        out_specs=pl.BlockSpec((1, SUBLANES, tn), lambda l, j: (l, 0, j)),
        out_shape=jax.ShapeDtypeStruct((n_layers, SUBLANES, n), F32),
        compiler_params=_cparams(2),
        name="adaln_mod",
    )(c_pad, w, b.reshape(n_layers, 1, n))


def _t5_thresholds():
    max_exact = REL_BUCKETS // 2
    d = np.arange(0, 4 * REL_MAX_DIST)
    dd = np.maximum(d, 1).astype(np.float32)
    large = max_exact + (np.log(dd / max_exact) / math.log(REL_MAX_DIST / max_exact)
                         * (REL_BUCKETS - max_exact)).astype(np.int32)
    large = np.minimum(large, REL_BUCKETS - 1)
    bucket = np.where(d < max_exact, d, large)
    return [int(np.argmax(bucket >= b)) for b in range(REL_BUCKETS)]


def _t5_kernel(rb_ref, o_ref, *, thr):
    h = pl.program_id(0)
    shape = (2 * MOBA_BLOCK, MOBA_BLOCK)
    key = lax.broadcasted_iota(jnp.int32, shape, 0)
    qry = lax.broadcasted_iota(jnp.int32, shape, 1)
    dist = qry + MOBA_BLOCK - key
    val = jnp.full(shape, rb_ref[h, REL_BUCKETS - 1], F32)
    for b in range(REL_BUCKETS - 2, -1, -1):
        val = jnp.where(dist < thr[b + 1], rb_ref[h, b], val)
    o_ref[0] = jnp.where(dist < 0, NEG_INF, LOG2E * val)


def _t5_table(rel_bias):
    return pl.pallas_call(
        functools.partial(_t5_kernel, thr=_t5_thresholds()),
        grid=(N_HEADS,),
        in_specs=[pl.BlockSpec(memory_space=pltpu.SMEM)],
        out_specs=pl.BlockSpec((1, 2 * MOBA_BLOCK, MOBA_BLOCK), lambda h: (h, 0, 0)),
        out_shape=jax.ShapeDtypeStruct((N_HEADS, 2 * MOBA_BLOCK, MOBA_BLOCK), F32),
        compiler_params=_cparams(1),
        name="t5_table",
    )(rel_bias.astype(F32))


def _time_permutation(tm):
    seg = tm // SUBLANES
    row = np.arange(tm)
    perm = np.zeros((tm, tm), np.float32)
    perm[row, (row % SUBLANES) * seg + row // SUBLANES] = 1.0
    return jnp.asarray(perm, BF16), jnp.asarray(perm.T, BF16)


def _vrow(v, g):
    return v[g * SUBLANES:(g + 1) * SUBLANES]


def _sublane_scan(a, b, h0):
    sub = lax.broadcasted_iota(jnp.int32, a.shape, 0)
    s = 1
    while s < SUBLANES:
        a_sh = jnp.where(sub < s, 1.0, pltpu.roll(a, s, 0))
        b_sh = jnp.where(sub < s, 0.0, pltpu.roll(b, s, 0))
        b = a * b_sh + b
        a = a * a_sh
        s *= 2
    return a * h0 + b


def _rglru_kernel(x_ref, mod_ref, g_ref, perm_ref, permt_ref, win_ref, cw_ref, cb_ref, wg_ref, bg_ref,
                  lam_ref, wout_ref, o_ref, conv_scr, h_scr):
    nbatch = x_ref.shape[0]
    d = x_ref.shape[-1]
    w = lam_ref.shape[-1]
    tm = x_ref.shape[1]
    seg = tm // SUBLANES
    halo = CONV_WIDTH - 1

    @pl.when(pl.program_id(0) == 0)
    def _():
        conv_scr[...] = jnp.zeros_like(conv_scr)
        h_scr[...] = jnp.zeros_like(h_scr)

    sub = lax.broadcasted_iota(jnp.int32, (SUBLANES, LRU_BLOCK), 0)
    blocks = [(hd * LRU_BLOCK, (hd + 1) * LRU_BLOCK) for hd in range(w // LRU_BLOCK)]
    streams = [(bi, hd) for bi in range(nbatch) for hd in range(len(blocks))]
    hp, uy, ux, xc, gates, hy = {}, {}, {}, {}, {}, {}

    def normalize(bi):
        mod = mod_ref[bi]
        h = _norm_mod(x_ref[bi], g_ref[...], mod[:, 0:d], mod[:, d:2 * d]).astype(BF16)
        hp[bi] = _dot(perm_ref[...], h).astype(BF16)

    def in_proj(bi, hd):
        lo, hi = blocks[hd]
        uy[bi, hd] = _dot(hp[bi], win_ref[:, lo:hi])
        ux[bi, hd] = _dot(hp[bi], win_ref[:, w + lo:w + hi])

    def conv(bi, hd):
        lo, hi = blocks[hd]
        u = ux[bi, hd]
        tops = []
        for i in range(halo):
            cur = pltpu.roll(_vrow(u, seg - halo + i), 1, 0)
            prv = pltpu.roll(conv_scr[bi, i, :, lo:hi], 1, 0)
            tops.append(jnp.where(sub == 0, prv, cur))
            conv_scr[bi, i, :, lo:hi] = _vrow(u, seg - halo + i)
        cw = cw_ref[:, lo:hi]
        v = u * cw[CONV_WIDTH - 1:CONV_WIDTH]
        for k in range(1, CONV_WIDTH):
            shifted = jnp.concatenate(tops[halo - k:] + [u[:tm - k * SUBLANES]], axis=0)
            v = v + shifted * cw[CONV_WIDTH - 1 - k:CONV_WIDTH - k]
        xc[bi, hd] = v + cb_ref[:, lo:hi]

    def gate_proj(bi, hd):
        gates[bi, hd] = _dot(xc[bi, hd].astype(BF16), wg_ref[hd])

    def recur(bi, hd):
        lo, hi = blocks[hd]
        u, v, gt = uy[bi, hd], xc[bi, hd], gates[bi, hd]
        y = u * jax.nn.sigmoid(u * (GELU_C1 + GELU_C2 * (u * u)))
        r = jax.nn.sigmoid(gt[:, :LRU_BLOCK] + bg_ref[0:1, lo:hi])
        i_gate = jax.nn.sigmoid(gt[:, LRU_BLOCK:] + bg_ref[1:2, lo:hi])
        lam = lam_ref[:, lo:hi]
        softplus_neg = jnp.maximum(-lam, 0.0) + jnp.log1p(jnp.exp(-jnp.abs(lam)))
        neg_log_a = r * (LRU_C * softplus_neg)
        a = jnp.exp(-neg_log_a)
        one_minus_a2 = (1.0 + a * a) * jnp.tanh(neg_log_a)
        mult = jnp.where(one_minus_a2 == 0.0, 0.0, one_minus_a2 * lax.rsqrt(one_minus_a2))
        b = mult * (i_gate * v)
        loc = [_vrow(b, 0)]
        prod = [_vrow(a, 0)]
        for g in range(1, seg):
            ag = _vrow(a, g)
            loc.append(ag * loc[-1] + _vrow(b, g))
            prod.append(ag * prod[-1])
        h_in = h_scr[bi, 0:1, lo:hi]
        seg_end = _sublane_scan(prod[-1], loc[-1], h_in)
        seg_in = jnp.where(sub == 0, h_in, pltpu.roll(seg_end, 1, 0))
        h_scr[bi, 0:1, lo:hi] = seg_end[SUBLANES - 1:SUBLANES]
        hs = jnp.concatenate([loc[g] + prod[g] * seg_in for g in range(seg)], axis=0)
        hy[bi, hd] = (hs * y).astype(BF16)

    def out_proj(bi):
        hy_seg = jnp.concatenate([hy[bi, hd] for hd in range(len(blocks))], axis=1)
        hy_time = _dot(permt_ref[...], hy_seg).astype(BF16)
        o_ref[bi] = x_ref[bi] + mod_ref[bi][:, 2 * d:3 * d] * _dot(hy_time, wout_ref[...])

    stages = (in_proj, conv, gate_proj, recur)
    normalize(0)
    for slot in range(len(streams) + len(stages) - 1):
        for depth, stage in enumerate(stages):
            i = slot - depth
            if 0 <= i < len(streams):
                bi, hd = streams[i]
                if stage is in_proj and hd == len(blocks) - 2 and bi + 1 < nbatch:
                    normalize(bi + 1)
                stage(bi, hd)
                if stage is recur and hd == len(blocks) - 1:
                    out_proj(bi)


def _rglru_layer(x, mod, gain, w_in, conv_w, conv_b, w_gates, b_gates, lam, w_out):
    b, s, d = x.shape
    w = lam.shape[-1]
    tm = TM_LRU
    perm, perm_t = _time_permutation(tm)
    return pl.pallas_call(
        _rglru_kernel,
        grid=(s // tm,),
        in_specs=[pl.BlockSpec((b, tm, d), lambda t: (0, t, 0)),
                  _const_spec(mod.shape),
                  _const_spec((1, d)),
                  _const_spec((tm, tm)),
                  _const_spec((tm, tm)),
                  _const_spec(w_in.shape),
                  _const_spec(conv_w.shape),
                  _const_spec((1, w)),
                  _const_spec(w_gates.shape),
                  _const_spec(b_gates.shape),
                  _const_spec((1, w)),
                  _const_spec(w_out.shape)],
        out_specs=pl.BlockSpec((b, tm, d), lambda t: (0, t, 0)),
        out_shape=jax.ShapeDtypeStruct(x.shape, F32),
        scratch_shapes=[pltpu.VMEM((b, CONV_WIDTH - 1, SUBLANES, w), F32),
                        pltpu.VMEM((b, SUBLANES, w), F32)],
        compiler_params=_cparams(1),
        name="rglru",
    )(x, mod, gain.reshape(1, d), perm, perm_t, w_in, conv_w, conv_b.reshape(1, w), w_gates, b_gates,
      lam.reshape(1, w), w_out)


def _ffn_kernel(*refs, has_attn, final):
    x_ref, mod_ref, g_ref, wg_ref, wu_ref, wd_ref = refs[:6]
    rest = list(refs[6:])
    att_ref = wo_ref = fn_ref = None
    if has_attn:
        att_ref, wo_ref = rest[:2]
        rest = rest[2:]
    if final:
        fn_ref = rest[0]
        rest = rest[1:]
    o_ref, = rest
    d = x_ref.shape[-1]
    ff = wg_ref.shape[-1]
    tm = x_ref.shape[1]

    xt = x_ref[0]
    mod = mod_ref[0]
    if has_attn:
        xt = xt + mod[:, 2 * d:3 * d] * _dot(att_ref[0], wo_ref[...])
    h = _norm_mod(xt, g_ref[...], mod[:, 3 * d:4 * d], mod[:, 4 * d:5 * d]).astype(BF16)
    acc = jnp.zeros((tm, d), F32)
    for c in range(ff // FF_CHUNK):
        lo = c * FF_CHUNK
        hi = lo + FF_CHUNK
        gate = _dot(h, wg_ref[0, :, lo:hi])
        up = _dot(h, wu_ref[0, :, lo:hi])
        act = (gate * jax.nn.sigmoid(gate)) * up
        acc = acc + _dot(act.astype(BF16), wd_ref[0, lo:hi, :])
    y = xt + mod[:, 5 * d:6 * d] * acc
    if final:
        y = y * lax.rsqrt(jnp.mean(y * y, axis=-1, keepdims=True) + RMS_EPS) * fn_ref[...]
    o_ref[0] = y


def _layer_spec(shape, layer):
    nd = len(shape)
    return pl.BlockSpec((1,) + tuple(shape[1:]), lambda *_: (layer,) + (0,) * (nd - 1),
                        pipeline_mode=pl.Buffered(1))


def _ffn_layer(x, mod, gain, layer, w_gate, w_up, w_down, attn=None, w_o=None, final_gain=None):
    b, s, d = x.shape
    tm = TM_FFN
    has_attn = attn is not None
    final = final_gain is not None
    in_specs = [pl.BlockSpec((1, tm, d), lambda i, t: (i, t, 0)),
                pl.BlockSpec((1, 1, mod.shape[-1]), lambda i, t: (i, 0, 0)),
                _const_spec((1, d)),
                _layer_spec(w_gate.shape, layer),
                _layer_spec(w_up.shape, layer),
                _layer_spec(w_down.shape, layer)]
    args = [x, mod, gain.reshape(1, d), w_gate, w_up, w_down]
    if has_attn:
        in_specs += [pl.BlockSpec((1, tm, d), lambda i, t: (i, t, 0)), _const_spec(w_o.shape)]
        args += [attn, w_o]
    if final:
        in_specs += [_const_spec((1, d))]
        args += [final_gain.reshape(1, d)]
    return pl.pallas_call(
        functools.partial(_ffn_kernel, has_attn=has_attn, final=final),
        grid=(b, s // tm),
        in_specs=in_specs,
        out_specs=pl.BlockSpec((1, tm, d), lambda i, t: (i, t, 0)),
        out_shape=jax.ShapeDtypeStruct(x.shape, F32),
        compiler_params=_cparams(2),
        name="ffn_attn_final" if has_attn else "ffn",
    )(*args)


def _qkv_kernel(x_ref, xn_ref, kvmod_ref, mod_ref, kvg_ref, qg_ref, wk_ref, wvt_ref, wq_ref,
                q_ref, k_ref, vt_ref, mask_ref, km_scr, hqa_scr, hkva_scr, hqb_scr, hkvb_scr):
    nbatch = x_ref.shape[0]
    d = x_ref.shape[-1]
    tm = MOBA_BLOCK
    nb = mask_ref.shape[3]
    step = pl.program_id(0)
    buf_a = (hqa_scr, hkva_scr)
    buf_b = (hqb_scr, hkvb_scr)
    blk = lax.broadcasted_iota(jnp.int32, (nb, tm), 0).astype(F32)
    q_scale = HEAD_DIM ** -0.5 * LOG2E
    pair_w = KV_GROUP * HEAD_DIM

    def normalize(src_ref, row0, bufs):
        for bi in range(nbatch):
            xt = src_ref[bi, row0:row0 + tm, :]
            xn = xt * lax.rsqrt(jnp.mean(xt * xt, axis=-1, keepdims=True) + RMS_EPS)
            mod = mod_ref[bi]
            kvmod = kvmod_ref[bi]
            rows = pl.ds(bi * tm, tm)
            bufs[0][rows, :] = (xn * (qg_ref[...] * (1.0 + mod[:, d:2 * d])) + mod[:, 0:d]).astype(BF16)
            bufs[1][rows, :] = (xn * (kvg_ref[...] * (1.0 + kvmod[:, d:2 * d])) + kvmod[:, 0:d]).astype(BF16)

    def process(half, cur, nxt_src, nxt_row0, nxt):
        j = 2 * step + half
        past = blk < j.astype(F32)
        out_rows = slice(half * tm, (half + 1) * tm)
        q_pair = {}
        h_q = cur[0][...]
        h_kv = cur[1][...]
        normalize(nxt_src, nxt_row0, nxt)

        def q_proj(hk):
            qp = _dot(h_q, wq_ref[:, hk * pair_w:(hk + 1) * pair_w]) * q_scale
            for bi in range(nbatch):
                q_ref[bi, out_rows, hk * pair_w:(hk + 1) * pair_w] = qp[bi * tm:(bi + 1) * tm].astype(BF16)
            q_pair[hk] = qp

        def select_blocks(bi, hq):
            hk, sub = divmod(hq, KV_GROUP)
            q_h = q_pair[hk][bi * tm:(bi + 1) * tm, sub * HEAD_DIM:(sub + 1) * HEAD_DIM]
            km_h = km_scr[bi, :, hk * HEAD_DIM:(hk + 1) * HEAD_DIM]
            gate = _dot_nt(km_h.astype(BF16), q_h.astype(BF16))
            gate = jnp.where(past, gate, NEG_INF)
            mask = jnp.full((nb, tm), NEG_INF, F32)
            for _ in range(MOBA_TOPK):
                best = jnp.max(gate, axis=0, keepdims=True)
                first = jnp.min(jnp.where(gate == best, blk, nb), axis=0, keepdims=True)
                hit = blk == first
                mask = jnp.where(hit, 0.0, mask)
                gate = jnp.where(hit, -jnp.inf, gate)
            mask = jnp.where(past, mask, NEG_INF)
            mask_ref[bi, hk, half, :, sub * tm:(sub + 1) * tm] = mask

        def k_proj():
            k = _dot(h_kv, wk_ref[...])
            for bi in range(nbatch):
                kb = k[bi * tm:(bi + 1) * tm]
                for hk in range(N_KV_HEADS):
                    k_ref[bi, hk, half] = kb[:, hk * HEAD_DIM:(hk + 1) * HEAD_DIM].astype(BF16)
            return k

        def v_proj():
            vt = _dot_nt(wvt_ref[...], h_kv)
            for bi in range(nbatch):
                for hk in range(N_KV_HEADS):
                    vt_ref[bi, hk, half] = vt[hk * HEAD_DIM:(hk + 1) * HEAD_DIM,
                                              bi * tm:(bi + 1) * tm].astype(BF16)

        k = None
        q_proj(0)
        for hk in range(N_KV_HEADS):
            if hk + 1 < N_KV_HEADS:
                q_proj(hk + 1)
            if hk == N_KV_HEADS - 2:
                k = k_proj()
            if hk == N_KV_HEADS - 1:
                v_proj()
            for bi in range(nbatch):
                for sub in range(KV_GROUP):
                    select_blocks(bi, hk * KV_GROUP + sub)
        for bi in range(nbatch):
            km_scr[bi, pl.ds(j, 1), :] = jnp.mean(k[bi * tm:(bi + 1) * tm], axis=0, keepdims=True)

    @pl.when(step == 0)
    def _():
        km_scr[...] = jnp.zeros_like(km_scr)
        normalize(x_ref, 0, buf_a)

    process(0, buf_a, x_ref, tm, buf_b)
    process(1, buf_b, xn_ref, 0, buf_a)


def _qkv_gate(x, kvmod, mod, kv_gain, q_gain, w_k, w_vt, w_q):
    b, s, d = x.shape
    tm = MOBA_BLOCK
    nb = s // tm
    dkv = N_KV_HEADS * HEAD_DIM
    dq = N_HEADS * HEAD_DIM
    out_shape = (jax.ShapeDtypeStruct((b, s, dq), BF16),
                 jax.ShapeDtypeStruct((b, N_KV_HEADS, nb, tm, HEAD_DIM), BF16),
                 jax.ShapeDtypeStruct((b, N_KV_HEADS, nb, HEAD_DIM, tm), BF16),
                 jax.ShapeDtypeStruct((b, N_KV_HEADS, nb, nb, KV_GROUP * tm), F32))
    out_specs = (pl.BlockSpec((b, 2 * tm, dq), lambda t: (0, t, 0)),
                 pl.BlockSpec((b, N_KV_HEADS, 2, tm, HEAD_DIM), lambda t: (0, 0, t, 0, 0)),
                 pl.BlockSpec((b, N_KV_HEADS, 2, HEAD_DIM, tm), lambda t: (0, 0, t, 0, 0)),
                 pl.BlockSpec((b, N_KV_HEADS, 2, nb, KV_GROUP * tm), lambda t: (0, 0, t, 0, 0)))
    return pl.pallas_call(
        _qkv_kernel,
        grid=(nb // 2,),
        in_specs=[pl.BlockSpec((b, 2 * tm, d), lambda t: (0, t, 0)),
                  pl.BlockSpec((b, tm, d), lambda t: (0, jnp.minimum(2 * t + 2, nb - 1), 0)),
                  _const_spec(kvmod.shape),
                  _const_spec(mod.shape),
                  _const_spec((1, d)),
                  _const_spec((1, d)),
                  _const_spec(w_k.shape),
                  _const_spec(w_vt.shape),
                  _const_spec(w_q.shape)],
        out_specs=out_specs,
        out_shape=out_shape,
        scratch_shapes=[pltpu.VMEM((b, nb, dkv), F32),
                        pltpu.VMEM((b * tm, d), BF16),
                        pltpu.VMEM((b * tm, d), BF16),
                        pltpu.VMEM((b * tm, d), BF16),
                        pltpu.VMEM((b * tm, d), BF16)],
        compiler_params=_cparams(1),
        name="qkv_gate",
    )(x, x, kvmod, mod, kv_gain.reshape(1, d), q_gain.reshape(1, d), w_k, w_vt, w_q)


def _attn_kernel(rb_ref, q_ref, qn_ref, k_ref, vt_ref, mask_ref, maskn_ref, tab_ref, o_ref, m_scr, l_scr,
                 acc_scr, sa_scr, sb_scr, cma_scr, cmb_scr, rowa_scr, rowb_scr, phase_scr):
    g0 = pl.program_id(1) * ATTN_STREAMS
    j = pl.program_id(2)
    nb = k_ref.shape[2]
    tq = q_ref.shape[1]
    pair_w = KV_GROUP * HEAD_DIM
    streams = range(ATTN_STREAMS)

    def stack_heads(qref, si):
        lo = si * pair_w
        return jnp.concatenate([qref[0, :, lo:lo + HEAD_DIM], qref[0, :, lo + HEAD_DIM:lo + pair_w]], axis=0)

    qs = [stack_heads(q_ref, si) for si in streams]

    def scores(si, n, qx):
        kb = k_ref[0, si, n]
        return [_dot_nt(kb, qx[hq * tq:(hq + 1) * tq]) for hq in range(KV_GROUP)]

    def tile_bias(si, lo):
        return [tab_ref[KV_GROUP * si + hq, lo:lo + MOBA_BLOCK, :] for hq in range(KV_GROUP)]

    def mask_row(si, n, mref):
        return mref[0, si, 0, pl.ds(n, 1), :]

    def col_max(s):
        return jnp.max(s, axis=0, keepdims=True)

    def col_sum(s):
        return jnp.sum(s, axis=0, keepdims=True)

    lane = lax.broadcasted_iota(jnp.int32, (1, KV_GROUP * tq), 1)
    far_bias = [LOG2E * jnp.where(lane < tq, rb_ref[KV_GROUP * (g0 + si), REL_BUCKETS - 1],
                                  rb_ref[KV_GROUP * (g0 + si) + 1, REL_BUCKETS - 1]) for si in streams]

    def far_row(si, n, valid, mref):
        return jnp.where(valid, mask_row(si, n, mref) + far_bias[si], NEG_INF)

    n_far = jnp.maximum(j - 1 - NEAR_FAR, 0)
    n_groups = 1 + (n_far + GROUP - 1) // GROUP
    jp = jnp.maximum(j - 1, 0)
    head_blocks = [j, jp] + [jnp.maximum(j - 2 - u, 0) for u in range(NEAR_FAR)]

    def group_blocks(k):
        return [jnp.where(k == 0, head_blocks[u], jnp.minimum((k - 1) * GROUP + u, nb - 1))
                for u in range(GROUP)]

    def put(buf, si, u, halves, row):
        s_buf, cm_buf, row_buf = buf
        for hq, s in enumerate(halves):
            cols = slice(hq * tq, (hq + 1) * tq)
            s_buf[si, u, :, cols] = s
            cm_buf[si, u:u + 1, cols] = col_max(s)
        row_buf[si, u:u + 1] = row

    def head_slot(si, u, jq, qx, mref):
        if u == 0:
            return ([s + t for s, t in zip(scores(si, jq, qx), tile_bias(si, MOBA_BLOCK))],
                    jnp.zeros((1, KV_GROUP * tq), F32))
        if u == 1:
            jqp = jnp.maximum(jq - 1, 0)
            return ([s + t for s, t in zip(scores(si, jqp, qx), tile_bias(si, 0))],
                    jnp.where(jq >= 1, mask_row(si, jqp, mref), NEG_INF))
        n = jnp.maximum(jq - u, 0)
        return scores(si, n, qx), far_row(si, n, jq - u >= 0, mref)

    j_next = jnp.minimum(j + 1, nb - 1)

    def step(si, k, buf, nxt, last):
        s_buf, cm_buf, row_buf = buf
        blocks = group_blocks(k)
        m_old = m_scr[si, 0:1]
        rows = [row_buf[si, u:u + 1] for u in range(GROUP)]
        m_new = m_old
        for u in range(GROUP):
            m_new = jnp.maximum(m_new, cm_buf[si, u:u + 1] + rows[u])
        alpha = jnp.exp2(m_old - m_new)
        l_new = alpha * l_scr[si, 0:1]
        pv = None
        qs_next = stack_heads(qn_ref, si) if last else None

        def stage1(u):
            if last:
                put(nxt, si, u, *head_slot(si, u, j_next, qs_next, maskn_ref))
            else:
                n = k * GROUP + u
                nc = jnp.minimum(n, nb - 1)
                put(nxt, si, u, scores(si, nc, qs[si]), far_row(si, nc, n < n_far, mask_ref))

        for u in range(GROUP):
            stage1(u)
            shift = rows[u] - m_new
            halves = []
            for hq in range(KV_GROUP):
                cols = slice(hq * tq, (hq + 1) * tq)
                p = jnp.exp2(s_buf[si, u, :, cols] + shift[:, cols])
                halves.append((col_sum(p), _dot(vt_ref[0, si, blocks[u]], p.astype(BF16))))
            l_new = l_new + jnp.concatenate([h[0] for h in halves], axis=1)
            d = jnp.concatenate([h[1] for h in halves], axis=1)
            pv = d if pv is None else pv + d
        acc_scr[si] = alpha * acc_scr[si] + pv
        l_scr[si, 0:1] = l_new
        m_scr[si, 0:1] = m_new

    buf_a = (sa_scr, cma_scr, rowa_scr)
    buf_b = (sb_scr, cmb_scr, rowb_scr)
    m_scr[...] = jnp.full_like(m_scr, NEG_INF)
    l_scr[...] = jnp.zeros_like(l_scr)
    acc_scr[...] = jnp.zeros_like(acc_scr)

    @pl.when(j == 0)
    def _():
        for si in streams:
            for u in range(GROUP):
                put(buf_a, si, u, *head_slot(si, u, j, qs[si], mask_ref))
        phase_scr[0] = 0

    phase = phase_scr[0]

    def run(k, buf, nxt):
        @pl.when((k >= 0) & (k + 1 < n_groups))
        def _():
            for si in streams:
                step(si, k, buf, nxt, last=False)

        @pl.when((k >= 0) & (k + 1 == n_groups))
        def _():
            for si in streams:
                step(si, k, buf, nxt, last=True)

    def pair(t, carry):
        run(2 * t - phase, buf_a, buf_b)
        run(2 * t + 1 - phase, buf_b, buf_a)
        return carry

    lax.fori_loop(0, (n_groups + phase + 1) // 2, pair, 0)
    phase_scr[0] = (n_groups + phase) % 2

    for si in streams:
        o = acc_scr[si] * (1.0 / l_scr[si, 0:1])
        o_ref[0, :, si * pair_w:(si + 1) * pair_w] = jnp.concatenate(
            [o[:, :tq].T, o[:, tq:].T], axis=1).astype(BF16)


def _moba_attn(rel_bias, q, k, vt, mask, table):
    b, s, dq = q.shape
    nb = k.shape[2]
    tq = MOBA_BLOCK
    ns = ATTN_STREAMS
    gw = ns * KV_GROUP * HEAD_DIM
    lanes = KV_GROUP * tq
    return pl.pallas_call(
        _attn_kernel,
        grid=(b, N_KV_HEADS // ns, nb),
        in_specs=[pl.BlockSpec(memory_space=pltpu.SMEM),
                  pl.BlockSpec((1, tq, gw), lambda i, g, j: (i, j, g)),
                  pl.BlockSpec((1, tq, gw), lambda i, g, j: (i, jnp.minimum(j + 1, nb - 1), g)),
                  pl.BlockSpec((1, ns, nb, tq, HEAD_DIM), lambda i, g, j: (i, g, 0, 0, 0),
                               pipeline_mode=pl.Buffered(1)),
                  pl.BlockSpec((1, ns, nb, HEAD_DIM, tq), lambda i, g, j: (i, g, 0, 0, 0),
                               pipeline_mode=pl.Buffered(1)),
                  pl.BlockSpec((1, ns, 1, nb, lanes), lambda i, g, j: (i, g, j, 0, 0)),
                  pl.BlockSpec((1, ns, 1, nb, lanes),
                               lambda i, g, j: (i, g, jnp.minimum(j + 1, nb - 1), 0, 0)),
                  pl.BlockSpec((ns * KV_GROUP, 2 * MOBA_BLOCK, MOBA_BLOCK), lambda i, g, j: (g, 0, 0))],
        out_specs=pl.BlockSpec((1, tq, gw), lambda i, g, j: (i, j, g)),
        out_shape=jax.ShapeDtypeStruct((b, s, dq), BF16),
        scratch_shapes=[pltpu.VMEM((ns, SUBLANES, lanes), F32),
                        pltpu.VMEM((ns, SUBLANES, lanes), F32),
                        pltpu.VMEM((ns, HEAD_DIM, lanes), F32),
                        pltpu.VMEM((ns, GROUP, MOBA_BLOCK, lanes), F32),
                        pltpu.VMEM((ns, GROUP, MOBA_BLOCK, lanes), F32),
                        pltpu.VMEM((ns, SUBLANES, lanes), F32),
                        pltpu.VMEM((ns, SUBLANES, lanes), F32),
                        pltpu.VMEM((ns, SUBLANES, lanes), F32),
                        pltpu.VMEM((ns, SUBLANES, lanes), F32),
                        pltpu.SMEM((1,), jnp.int32)],
        compiler_params=_cparams(3),
        name="moba_attn",
    )(rel_bias.astype(F32), q, q, k, vt, mask, mask, table)


def kernel(x, c, mod_w, mod_b, norm_mix, norm_ffn, lru_w_in, lru_conv_w, lru_conv_b, lru_w_gates,
           lru_b_gates, lru_lambda, lru_w_out, kv_mod_w, kv_mod_b, kv_norm, w_kv, attn_w_q, attn_w_o,
           rel_bias, ffn_w_gate, ffn_w_up, ffn_w_down, final_norm):
    b, s, d = x.shape
    assert s % TM_FFN == 0 and s % MOBA_BLOCK == 0 and b <= SUBLANES
    dkv = N_KV_HEADS * HEAD_DIM

    c_pad = jnp.zeros((SUBLANES, d), F32).at[:b].set(c.astype(F32))
    mod = _adaln_mod(c_pad, mod_w, mod_b)[:, :b]
    kvmod = _adaln_mod(c_pad, kv_mod_w[None], kv_mod_b[None])[0, :b]
    mod0 = mod[0][:, None, :]
    mod1 = mod[1][:, None, :]
    kvmod = kvmod[:, None, :]

    x = _rglru_layer(x, mod0, norm_mix[0], lru_w_in[0].astype(BF16), lru_conv_w[0], lru_conv_b[0],
                     lru_w_gates[0].astype(BF16), lru_b_gates[0], lru_lambda[0], lru_w_out[0].astype(BF16))
    ffn_w = (ffn_w_gate.astype(BF16), ffn_w_up.astype(BF16), ffn_w_down.astype(BF16))
    x = _ffn_layer(x, mod0, norm_ffn[0], 0, *ffn_w)

    q, k, vt, mask = _qkv_gate(x, kvmod, mod1, kv_norm, norm_mix[1], w_kv[:, :dkv].astype(BF16),
                               w_kv[:, dkv:].T.astype(BF16), attn_w_q[0].astype(BF16))
    table = _t5_table(rel_bias)
    attn = _moba_attn(rel_bias, q, k, vt, mask, table)
    return _ffn_layer(x, mod1, norm_ffn[1], 1, *ffn_w, attn=attn, w_o=attn_w_o[0].astype(BF16),
                      final_gain=final_norm)
```

```python
import functools
import math

import numpy as np
import jax
import jax.numpy as jnp
from jax import lax
from jax.experimental import pallas as pl
from jax.experimental.pallas import tpu as pltpu

F32 = jnp.float32
BF16 = jnp.bfloat16

LRU_BLOCK = 256
CONV_WIDTH = 4
LRU_C = 8.0
N_HEADS = 8
N_KV_HEADS = 4
HEAD_DIM = 128
KV_GROUP = N_HEADS // N_KV_HEADS
MOBA_BLOCK = 256
MOBA_TOPK = 3
REL_BUCKETS = 32
REL_MAX_DIST = 128
RMS_EPS = 1e-6
NEG_INF = -1e30
LOG2E = math.log2(math.e)
GELU_C1 = 2.0 * math.sqrt(2.0 / math.pi)
GELU_C2 = GELU_C1 * 0.044715

SUBLANES = 8
LANES = 128
VMEM_LIMIT = 56 * 1024 * 1024

LRU_TILE = 256
TM_LRU = 512
TM_FFN = 1024
FF_CHUNK = 256
MOD_TN = 2048
GROUP = 4
NEAR_FAR = GROUP - 2
ATTN_STREAMS = 4


def _cparams(n_axes):
    return pltpu.CompilerParams(dimension_semantics=("arbitrary",) * n_axes,
                                vmem_limit_bytes=VMEM_LIMIT)


def _const_spec(shape):
    nd = len(shape)
    return pl.BlockSpec(shape, lambda *_: (0,) * nd, pipeline_mode=pl.Buffered(1))


def _dot(a, b):
    return jnp.dot(a, b, preferred_element_type=F32)


def _dot_nt(a, b, precision=None):
    return lax.dot_general(a, b, (((1,), (1,)), ((), ())), preferred_element_type=F32,
                           precision=precision)


def _norm_mod(xt, gain, shift, scale):
    inv = lax.rsqrt(jnp.mean(xt * xt, axis=-1, keepdims=True) + RMS_EPS)
    return (xt * inv) * (gain * (1.0 + scale)) + shift


def _mod_kernel(c_ref, w_ref, b_ref, o_ref, *, nbatch):
    c = c_ref[...]
    cs_t = (c * jax.nn.sigmoid(c)).T
    w = w_ref[0]
    o_ref[0] = jnp.zeros(o_ref.shape[1:], F32)
    for bi in range(nbatch):
        o_ref[0, bi:bi + 1, :] = jnp.sum(cs_t[:, bi:bi + 1] * w, axis=0, keepdims=True) + b_ref[0]


def _adaln_mod(c_pad, nbatch, w, b):
    n_layers, d, n = w.shape
    tn = min(MOD_TN, n)
    return pl.pallas_call(
        functools.partial(_mod_kernel, nbatch=nbatch),
        grid=(n_layers, n // tn),
        in_specs=[pl.BlockSpec((SUBLANES, d), lambda l, j: (0, 0)),
                  pl.BlockSpec((1, d, tn), lambda l, j: (l, 0, j)),
                  pl.BlockSpec((1, 1, tn), lambda l, j: (l, 0, j))],
        out_specs=pl.BlockSpec((1, SUBLANES, tn), lambda l, j: (l, 0, j)),
        out_shape=jax.ShapeDtypeStruct((n_layers, SUBLANES, n), F32),
        compiler_params=_cparams(2),
        name="adaln_mod",
    )(c_pad, w, b.reshape(n_layers, 1, n))


def _t5_thresholds():
    max_exact = REL_BUCKETS // 2
    d = np.arange(0, 4 * REL_MAX_DIST)
    dd = np.maximum(d, 1).astype(np.float32)
    large = max_exact + (np.log(dd / max_exact) / math.log(REL_MAX_DIST / max_exact)
                         * (REL_BUCKETS - max_exact)).astype(np.int32)
    large = np.minimum(large, REL_BUCKETS - 1)
    bucket = np.where(d < max_exact, d, large)
    return [int(np.argmax(bucket >= b)) for b in range(REL_BUCKETS)]


def _t5_kernel(rb_ref, o_ref, *, thr):
    h = pl.program_id(0)
    shape = (2 * MOBA_BLOCK, MOBA_BLOCK)
    key = lax.broadcasted_iota(jnp.int32, shape, 0)
    qry = lax.broadcasted_iota(jnp.int32, shape, 1)
    dist = qry + MOBA_BLOCK - key
    val = jnp.full(shape, rb_ref[h, REL_BUCKETS - 1], F32)
    for b in range(REL_BUCKETS - 2, -1, -1):
        val = jnp.where(dist < thr[b + 1], rb_ref[h, b], val)
    o_ref[0] = jnp.where(dist < 0, NEG_INF, LOG2E * val)


def _t5_table(rel_bias):
    return pl.pallas_call(
        functools.partial(_t5_kernel, thr=_t5_thresholds()),
        grid=(N_HEADS,),
        in_specs=[pl.BlockSpec(memory_space=pltpu.SMEM)],
        out_specs=pl.BlockSpec((1, 2 * MOBA_BLOCK, MOBA_BLOCK), lambda h: (h, 0, 0)),
        out_shape=jax.ShapeDtypeStruct((N_HEADS, 2 * MOBA_BLOCK, MOBA_BLOCK), F32),
        compiler_params=_cparams(1),
        name="t5_table",
    )(rel_bias.astype(F32))


def _time_permutation(tm):
    seg = tm // SUBLANES
    row = np.arange(tm)
    perm = np.zeros((tm, tm), np.float32)
    perm[row, (row % SUBLANES) * seg + row // SUBLANES] = 1.0
    return jnp.asarray(perm, BF16), jnp.asarray(perm.T, BF16)


def _vrow(v, g):
    return v[g * SUBLANES:(g + 1) * SUBLANES]


def _sublane_scan(a, b, h0):
    sub = lax.broadcasted_iota(jnp.int32, a.shape, 0)
    s = 1
    while s < SUBLANES:
        a_sh = jnp.where(sub < s, 1.0, pltpu.roll(a, s, 0))
        b_sh = jnp.where(sub < s, 0.0, pltpu.roll(b, s, 0))
        b = a * b_sh + b
        a = a * a_sh
        s *= 2
    return a * h0 + b


def _rglru_kernel(x_ref, mod_ref, g_ref, perm_ref, permt_ref, win_ref, cw_ref, cb_ref, wg_ref, bg_ref,
                  lam_ref, wout_ref, o_ref, conv_scr, h_scr):
    nbatch = x_ref.shape[0]
    d = x_ref.shape[-1]
    w = lam_ref.shape[-1]
    tm = perm_ref.shape[0]
    seg = tm // SUBLANES
    halo = CONV_WIDTH - 1

    @pl.when(pl.program_id(0) == 0)
    def _():
        conv_scr[...] = jnp.zeros_like(conv_scr)
        h_scr[...] = jnp.zeros_like(h_scr)

    sub = lax.broadcasted_iota(jnp.int32, (SUBLANES, LRU_BLOCK), 0)
    blocks = [(hd * LRU_BLOCK, (hd + 1) * LRU_BLOCK) for hd in range(w // LRU_BLOCK)]
    keys = [(ti, b) for ti in range(x_ref.shape[1] // tm) for b in range(nbatch)]
    streams = [(bi, hd) for bi in keys for hd in range(len(blocks))]
    hp, uy, ux, xc, gates, hy = {}, {}, {}, {}, {}, {}

    def rows(bi):
        return slice(bi[0] * tm, (bi[0] + 1) * tm)

    def normalize(bi):
        mod = mod_ref[bi[1]]
        h = _norm_mod(x_ref[bi[1], rows(bi), :], g_ref[...], mod[:, 0:d], mod[:, d:2 * d]).astype(BF16)
        hp[bi] = _dot(perm_ref[...], h).astype(BF16)

    def in_proj(bi, hd):
        lo, hi = blocks[hd]
        uy[bi, hd] = _dot(hp[bi], win_ref[:, lo:hi])
        ux[bi, hd] = _dot(hp[bi], win_ref[:, w + lo:w + hi])

    def conv(bi, hd):
        lo, hi = blocks[hd]
        u = ux[bi, hd]
        tops = []
        for i in range(halo):
            cur = pltpu.roll(_vrow(u, seg - halo + i), 1, 0)
            prv = pltpu.roll(conv_scr[bi[1], i, :, lo:hi], 1, 0)
            tops.append(jnp.where(sub == 0, prv, cur))
            conv_scr[bi[1], i, :, lo:hi] = _vrow(u, seg - halo + i)
        cw = cw_ref[:, lo:hi]
        v = u * cw[CONV_WIDTH - 1:CONV_WIDTH]
        for k in range(1, CONV_WIDTH):
            shifted = jnp.concatenate(tops[halo - k:] + [u[:tm - k * SUBLANES]], axis=0)
            v = v + shifted * cw[CONV_WIDTH - 1 - k:CONV_WIDTH - k]
        xc[bi, hd] = v + cb_ref[:, lo:hi]

    def gate_proj(bi, hd):
        gates[bi, hd] = _dot(xc[bi, hd].astype(BF16), wg_ref[hd])

    def recur(bi, hd):
        lo, hi = blocks[hd]
        u, v, gt = uy[bi, hd], xc[bi, hd], gates[bi, hd]
        y = u * jax.nn.sigmoid(u * (GELU_C1 + GELU_C2 * (u * u)))
        r = jax.nn.sigmoid(gt[:, :LRU_BLOCK] + bg_ref[0:1, lo:hi])
        i_gate = jax.nn.sigmoid(gt[:, LRU_BLOCK:] + bg_ref[1:2, lo:hi])
        lam = lam_ref[:, lo:hi]
        softplus_neg = jnp.maximum(-lam, 0.0) + jnp.log1p(jnp.exp(-jnp.abs(lam)))
        neg_log_a = r * (LRU_C * softplus_neg)
        a = jnp.exp(-neg_log_a)
        one_minus_a2 = (1.0 + a * a) * jnp.tanh(neg_log_a)
        mult = jnp.where(one_minus_a2 == 0.0, 0.0, one_minus_a2 * lax.rsqrt(one_minus_a2))
        b = mult * (i_gate * v)
        loc = [_vrow(b, 0)]
        prod = [_vrow(a, 0)]
        for g in range(1, seg):
            ag = _vrow(a, g)
            loc.append(ag * loc[-1] + _vrow(b, g))
            prod.append(ag * prod[-1])
        h_in = h_scr[bi[1], 0:1, lo:hi]
        seg_end = _sublane_scan(prod[-1], loc[-1], h_in)
        seg_in = jnp.where(sub == 0, h_in, pltpu.roll(seg_end, 1, 0))
        h_scr[bi[1], 0:1, lo:hi] = seg_end[SUBLANES - 1:SUBLANES]
        hs = jnp.concatenate([loc[g] + prod[g] * seg_in for g in range(seg)], axis=0)
        hy[bi, hd] = (hs * y).astype(BF16)

    def out_proj(bi):
        hy_seg = jnp.concatenate([hy[bi, hd] for hd in range(len(blocks))], axis=1)
        hy_time = _dot(permt_ref[...], hy_seg).astype(BF16)
        o_ref[bi[1], rows(bi), :] = (x_ref[bi[1], rows(bi), :]
                                     + mod_ref[bi[1]][:, 2 * d:3 * d] * _dot(hy_time, wout_ref[...]))

    stages = (in_proj, conv, gate_proj, recur)
    normalize(keys[0])
    for slot in range(len(streams) + len(stages) - 1):
        for depth, stage in enumerate(stages):
            i = slot - depth
            if 0 <= i < len(streams):
                bi, hd = streams[i]
                nxt = keys.index(bi) + 1
                if stage is in_proj and hd == len(blocks) - 2 and nxt < len(keys):
                    normalize(keys[nxt])
                stage(bi, hd)
                if stage is recur and hd == len(blocks) - 1:
                    out_proj(bi)


def _rglru_layer(x, mod, gain, w_in, conv_w, conv_b, w_gates, b_gates, lam, w_out):
    b, s, d = x.shape
    w = lam.shape[-1]
    tm = TM_LRU
    perm, perm_t = _time_permutation(LRU_TILE)
    return pl.pallas_call(
        _rglru_kernel,
        grid=(s // tm,),
        in_specs=[pl.BlockSpec((b, tm, d), lambda t: (0, t, 0)),
                  _const_spec(mod.shape),
                  _const_spec((1, d)),
                  _const_spec(perm.shape),
                  _const_spec(perm_t.shape),
                  _const_spec(w_in.shape),
                  _const_spec(conv_w.shape),
                  _const_spec((1, w)),
                  _const_spec(w_gates.shape),
                  _const_spec(b_gates.shape),
                  _const_spec((1, w)),
                  _const_spec(w_out.shape)],
        out_specs=pl.BlockSpec((b, tm, d), lambda t: (0, t, 0)),
        out_shape=jax.ShapeDtypeStruct(x.shape, F32),
        scratch_shapes=[pltpu.VMEM((b, CONV_WIDTH - 1, SUBLANES, w), F32),
                        pltpu.VMEM((b, SUBLANES, w), F32)],
        compiler_params=_cparams(1),
        name="rglru",
    )(x, mod, gain.reshape(1, d), perm, perm_t, w_in, conv_w, conv_b.reshape(1, w), w_gates, b_gates,
      lam.reshape(1, w), w_out)


def _ffn_kernel(*refs, has_attn, final):
    x_ref, mod_ref, g_ref, wg_ref, wu_ref, wd_ref = refs[:6]
    rest = list(refs[6:])
    att_ref = wo_ref = fn_ref = None
    if has_attn:
        att_ref, wo_ref = rest[:2]
        rest = rest[2:]
    if final:
        fn_ref = rest[0]
        rest = rest[1:]
    o_ref, = rest
    d = x_ref.shape[-1]
    ff = wg_ref.shape[-1]
    tm = x_ref.shape[1]

    xt = x_ref[0]
    mod = mod_ref[0]
    if has_attn:
        xt = xt + mod[:, 2 * d:3 * d] * _dot(att_ref[0], wo_ref[...])
    h = _norm_mod(xt, g_ref[...], mod[:, 3 * d:4 * d], mod[:, 4 * d:5 * d]).astype(BF16)
    acc = jnp.zeros((tm, d), F32)
    for c in range(ff // FF_CHUNK):
        lo = c * FF_CHUNK
        hi = lo + FF_CHUNK
        gate = _dot(h, wg_ref[0, :, lo:hi])
        up = _dot(h, wu_ref[0, :, lo:hi])
        act = (gate * jax.nn.sigmoid(gate)) * up
        acc = acc + _dot(act.astype(BF16), wd_ref[0, lo:hi, :])
    y = xt + mod[:, 5 * d:6 * d] * acc
    if final:
        y = y * lax.rsqrt(jnp.mean(y * y, axis=-1, keepdims=True) + RMS_EPS) * fn_ref[...]
    o_ref[0] = y


def _layer_spec(shape, layer):
    nd = len(shape)
    return pl.BlockSpec((1,) + tuple(shape[1:]), lambda *_: (layer,) + (0,) * (nd - 1),
                        pipeline_mode=pl.Buffered(1))


def _ffn_layer(x, mod, gain, layer, w_gate, w_up, w_down, attn=None, w_o=None, final_gain=None):
    b, s, d = x.shape
    tm = TM_FFN
    has_attn = attn is not None
    final = final_gain is not None
    in_specs = [pl.BlockSpec((1, tm, d), lambda i, t: (i, t, 0)),
                pl.BlockSpec((1, 1, mod.shape[-1]), lambda i, t: (i, 0, 0)),
                _const_spec((1, d)),
                _layer_spec(w_gate.shape, layer),
                _layer_spec(w_up.shape, layer),
                _layer_spec(w_down.shape, layer)]
    args = [x, mod, gain.reshape(1, d), w_gate, w_up, w_down]
    if has_attn:
        in_specs += [pl.BlockSpec((1, tm, d), lambda i, t: (i, t, 0)), _const_spec(w_o.shape)]
        args += [attn, w_o]
    if final:
        in_specs += [_const_spec((1, d))]
        args += [final_gain.reshape(1, d)]
    return pl.pallas_call(
        functools.partial(_ffn_kernel, has_attn=has_attn, final=final),
        grid=(b, s // tm),
        in_specs=in_specs,
        out_specs=pl.BlockSpec((1, tm, d), lambda i, t: (i, t, 0)),
        out_shape=jax.ShapeDtypeStruct(x.shape, F32),
        compiler_params=_cparams(2),
        name="ffn_attn_final" if has_attn else "ffn",
    )(*args)


def _qkv_kernel(x_ref, xn_ref, kvmod_ref, mod_ref, kvg_ref, qg_ref, wk_ref, wvt_ref, wq_ref,
                q_ref, k_ref, vt_ref, mask_ref, km_scr, hqa_scr, hkva_scr, hqb_scr, hkvb_scr):
    nbatch = x_ref.shape[0]
    d = x_ref.shape[-1]
    tm = MOBA_BLOCK
    nb = mask_ref.shape[3]
    step = pl.program_id(0)
    buf_a = (hqa_scr, hkva_scr)
    buf_b = (hqb_scr, hkvb_scr)
    blk = lax.broadcasted_iota(jnp.int32, (nb, tm), 0).astype(F32)
    q_scale = HEAD_DIM ** -0.5 * LOG2E
    pair_w = KV_GROUP * HEAD_DIM

    def normalize(src_ref, row0, bufs):
        for bi in range(nbatch):
            xt = src_ref[bi, row0:row0 + tm, :]
            xn = xt * lax.rsqrt(jnp.mean(xt * xt, axis=-1, keepdims=True) + RMS_EPS)
            mod = mod_ref[bi]
            kvmod = kvmod_ref[bi]
            rows = pl.ds(bi * tm, tm)
            bufs[0][rows, :] = (xn * (qg_ref[...] * (1.0 + mod[:, d:2 * d])) + mod[:, 0:d]).astype(BF16)
            bufs[1][rows, :] = (xn * (kvg_ref[...] * (1.0 + kvmod[:, d:2 * d])) + kvmod[:, 0:d]).astype(BF16)

    def process(half, cur, nxt_src, nxt_row0, nxt):
        j = 2 * step + half
        past = blk < j.astype(F32)
        out_rows = slice(half * tm, (half + 1) * tm)
        q_pair = {}
        h_q = cur[0][...]
        h_kv = cur[1][...]
        normalize(nxt_src, nxt_row0, nxt)

        def q_proj(hk):
            qp = _dot(h_q, wq_ref[:, hk * pair_w:(hk + 1) * pair_w]) * q_scale
            for bi in range(nbatch):
                q_ref[bi, out_rows, hk * pair_w:(hk + 1) * pair_w] = qp[bi * tm:(bi + 1) * tm].astype(BF16)
            q_pair[hk] = qp

        def select_blocks(bi, hq):
            hk, sub = divmod(hq, KV_GROUP)
            q_h = q_pair[hk][bi * tm:(bi + 1) * tm, sub * HEAD_DIM:(sub + 1) * HEAD_DIM]
            km_h = km_scr[bi, :, hk * HEAD_DIM:(hk + 1) * HEAD_DIM]
            gate = _dot_nt(km_h.astype(BF16), q_h.astype(BF16))
            gate = jnp.where(past, gate, NEG_INF)
            mask = jnp.full((nb, tm), NEG_INF, F32)
            for _ in range(MOBA_TOPK):
                best = jnp.max(gate, axis=0, keepdims=True)
                first = jnp.min(jnp.where(gate == best, blk, nb), axis=0, keepdims=True)
                hit = blk == first
                mask = jnp.where(hit, 0.0, mask)
                gate = jnp.where(hit, -jnp.inf, gate)
            mask = jnp.where(past, mask, NEG_INF)
            mask_ref[bi, hk, half, :, sub * tm:(sub + 1) * tm] = mask

        def k_proj():
            k = _dot(h_kv, wk_ref[...])
            for bi in range(nbatch):
                kb = k[bi * tm:(bi + 1) * tm]
                for hk in range(N_KV_HEADS):
                    k_ref[bi, hk, half] = kb[:, hk * HEAD_DIM:(hk + 1) * HEAD_DIM].astype(BF16)
            return k

        def v_proj():
            vt = _dot_nt(wvt_ref[...], h_kv)
            for bi in range(nbatch):
                for hk in range(N_KV_HEADS):
                    vt_ref[bi, hk, half] = vt[hk * HEAD_DIM:(hk + 1) * HEAD_DIM,
                                              bi * tm:(bi + 1) * tm].astype(BF16)

        k = None
        q_proj(0)
        for hk in range(N_KV_HEADS):
            if hk + 1 < N_KV_HEADS:
                q_proj(hk + 1)
            if hk == N_KV_HEADS - 2:
                k = k_proj()
            if hk == N_KV_HEADS - 1:
                v_proj()
            for bi in range(nbatch):
                for sub in range(KV_GROUP):
                    select_blocks(bi, hk * KV_GROUP + sub)
        for bi in range(nbatch):
            km_scr[bi, pl.ds(j, 1), :] = jnp.mean(k[bi * tm:(bi + 1) * tm], axis=0, keepdims=True)

    @pl.when(step == 0)
    def _():
        km_scr[...] = jnp.zeros_like(km_scr)
        normalize(x_ref, 0, buf_a)

    process(0, buf_a, x_ref, tm, buf_b)
    process(1, buf_b, xn_ref, 0, buf_a)


def _qkv_gate(x, kvmod, mod, kv_gain, q_gain, w_k, w_vt, w_q):
    b, s, d = x.shape
    tm = MOBA_BLOCK
    nb = s // tm
    dkv = N_KV_HEADS * HEAD_DIM
    dq = N_HEADS * HEAD_DIM
    out_shape = (jax.ShapeDtypeStruct((b, s, dq), BF16),
                 jax.ShapeDtypeStruct((b, N_KV_HEADS, nb, tm, HEAD_DIM), BF16),
                 jax.ShapeDtypeStruct((b, N_KV_HEADS, nb, HEAD_DIM, tm), BF16),
                 jax.ShapeDtypeStruct((b, N_KV_HEADS, nb, nb, KV_GROUP * tm), F32))
    out_specs = (pl.BlockSpec((b, 2 * tm, dq), lambda t: (0, t, 0)),
                 pl.BlockSpec((b, N_KV_HEADS, 2, tm, HEAD_DIM), lambda t: (0, 0, t, 0, 0)),
                 pl.BlockSpec((b, N_KV_HEADS, 2, HEAD_DIM, tm), lambda t: (0, 0, t, 0, 0)),
                 pl.BlockSpec((b, N_KV_HEADS, 2, nb, KV_GROUP * tm), lambda t: (0, 0, t, 0, 0)))
    return pl.pallas_call(
        _qkv_kernel,
        grid=(nb // 2,),
        in_specs=[pl.BlockSpec((b, 2 * tm, d), lambda t: (0, t, 0)),
                  pl.BlockSpec((b, tm, d), lambda t: (0, jnp.minimum(2 * t + 2, nb - 1), 0)),
                  _const_spec(kvmod.shape),
                  _const_spec(mod.shape),
                  _const_spec((1, d)),
                  _const_spec((1, d)),
                  _const_spec(w_k.shape),
                  _const_spec(w_vt.shape),
                  _const_spec(w_q.shape)],
        out_specs=out_specs,
        out_shape=out_shape,
        scratch_shapes=[pltpu.VMEM((b, nb, dkv), F32),
                        pltpu.VMEM((b * tm, d), BF16),
                        pltpu.VMEM((b * tm, d), BF16),
                        pltpu.VMEM((b * tm, d), BF16),
                        pltpu.VMEM((b * tm, d), BF16)],
        compiler_params=_cparams(1),
        name="qkv_gate",
    )(x, x, kvmod, mod, kv_gain.reshape(1, d), q_gain.reshape(1, d), w_k, w_vt, w_q)


def _attn_kernel(rb_ref, q_ref, qn_ref, k_ref, vt_ref, mask_ref, maskn_ref, tab_ref, o_ref, m_scr, l_scr,
                 acc_scr, sa_scr, sb_scr, cma_scr, cmb_scr, rowa_scr, rowb_scr, phase_scr):
    g0 = pl.program_id(1) * ATTN_STREAMS
    j = pl.program_id(2)
    nb = k_ref.shape[2]
    tq = q_ref.shape[1]
    pair_w = KV_GROUP * HEAD_DIM
    streams = range(ATTN_STREAMS)

    def stack_heads(qref, si):
        lo = si * pair_w
        return jnp.concatenate([qref[0, :, lo:lo + HEAD_DIM], qref[0, :, lo + HEAD_DIM:lo + pair_w]], axis=0)

    qs = [stack_heads(q_ref, si) for si in streams]

    def scores(si, n, qx):
        kb = k_ref[0, si, n]
        return [_dot_nt(kb, qx[hq * tq:(hq + 1) * tq]) for hq in range(KV_GROUP)]

    def tile_bias(si, lo):
        return [tab_ref[KV_GROUP * si + hq, lo:lo + MOBA_BLOCK, :] for hq in range(KV_GROUP)]

    def mask_row(si, n, mref):
        return mref[0, si, 0, pl.ds(n, 1), :]

    def col_max(s):
        return jnp.max(s, axis=0, keepdims=True)

    def col_sum(s):
        return jnp.sum(s, axis=0, keepdims=True)

    lane = lax.broadcasted_iota(jnp.int32, (1, KV_GROUP * tq), 1)
    far_bias = [LOG2E * jnp.where(lane < tq, rb_ref[KV_GROUP * (g0 + si), REL_BUCKETS - 1],
                                  rb_ref[KV_GROUP * (g0 + si) + 1, REL_BUCKETS - 1]) for si in streams]

    def far_row(si, n, valid, mref):
        return jnp.where(valid, mask_row(si, n, mref) + far_bias[si], NEG_INF)

    n_far = jnp.maximum(j - 1 - NEAR_FAR, 0)
    n_groups = 1 + (n_far + GROUP - 1) // GROUP
    jp = jnp.maximum(j - 1, 0)
    head_blocks = [j, jp] + [jnp.maximum(j - 2 - u, 0) for u in range(NEAR_FAR)]

    def group_blocks(k):
        return [jnp.where(k == 0, head_blocks[u], jnp.minimum((k - 1) * GROUP + u, nb - 1))
                for u in range(GROUP)]

    def put(buf, si, u, halves, row):
        s_buf, cm_buf, row_buf = buf
        for hq, s in enumerate(halves):
            cols = slice(hq * tq, (hq + 1) * tq)
            s_buf[si, u, :, cols] = s
            cm_buf[si, u:u + 1, cols] = col_max(s)
        row_buf[si, u:u + 1] = row

    def head_slot(si, u, jq, qx, mref):
        if u == 0:
            return ([s + t for s, t in zip(scores(si, jq, qx), tile_bias(si, MOBA_BLOCK))],
                    jnp.zeros((1, KV_GROUP * tq), F32))
        if u == 1:
            jqp = jnp.maximum(jq - 1, 0)
            return ([s + t for s, t in zip(scores(si, jqp, qx), tile_bias(si, 0))],
                    jnp.where(jq >= 1, mask_row(si, jqp, mref), NEG_INF))
        n = jnp.maximum(jq - u, 0)
        return scores(si, n, qx), far_row(si, n, jq - u >= 0, mref)

    j_next = jnp.minimum(j + 1, nb - 1)

    def step(si, k, buf, nxt, last):
        s_buf, cm_buf, row_buf = buf
        blocks = group_blocks(k)
        m_old = m_scr[si, 0:1]
        rows = [row_buf[si, u:u + 1] for u in range(GROUP)]
        m_new = m_old
        for u in range(GROUP):
            m_new = jnp.maximum(m_new, cm_buf[si, u:u + 1] + rows[u])
        alpha = jnp.exp2(m_old - m_new)
        l_new = alpha * l_scr[si, 0:1]
        pv = None
        qs_next = stack_heads(qn_ref, si) if last else None

        def stage1(u):
            if last:
                put(nxt, si, u, *head_slot(si, u, j_next, qs_next, maskn_ref))
            else:
                n = k * GROUP + u
                nc = jnp.minimum(n, nb - 1)
                put(nxt, si, u, scores(si, nc, qs[si]), far_row(si, nc, n < n_far, mask_ref))

        for u in range(GROUP):
            stage1(u)
            shift = rows[u] - m_new
            halves = []
            for hq in range(KV_GROUP):
                cols = slice(hq * tq, (hq + 1) * tq)
                p = jnp.exp2(s_buf[si, u, :, cols] + shift[:, cols])
                halves.append((col_sum(p), _dot(vt_ref[0, si, blocks[u]], p.astype(BF16))))
            l_new = l_new + jnp.concatenate([h[0] for h in halves], axis=1)
            d = jnp.concatenate([h[1] for h in halves], axis=1)
            pv = d if pv is None else pv + d
        acc_scr[si] = alpha * acc_scr[si] + pv
        l_scr[si, 0:1] = l_new
        m_scr[si, 0:1] = m_new

    buf_a = (sa_scr, cma_scr, rowa_scr)
    buf_b = (sb_scr, cmb_scr, rowb_scr)
    m_scr[...] = jnp.full_like(m_scr, NEG_INF)
    l_scr[...] = jnp.zeros_like(l_scr)
    acc_scr[...] = jnp.zeros_like(acc_scr)

    @pl.when(j == 0)
    def _():
        for si in streams:
            for u in range(GROUP):
                put(buf_a, si, u, *head_slot(si, u, j, qs[si], mask_ref))
        phase_scr[0] = 0

    phase = phase_scr[0]

    def run(k, buf, nxt):
        @pl.when((k >= 0) & (k + 1 < n_groups))
        def _():
            for si in streams:
                step(si, k, buf, nxt, last=False)

        @pl.when((k >= 0) & (k + 1 == n_groups))
        def _():
            for si in streams:
                step(si, k, buf, nxt, last=True)

    def pair(t, carry):
        run(2 * t - phase, buf_a, buf_b)
        run(2 * t + 1 - phase, buf_b, buf_a)
        return carry

    lax.fori_loop(0, (n_groups + phase + 1) // 2, pair, 0)
    phase_scr[0] = (n_groups + phase) % 2

    for si in streams:
        o = acc_scr[si] * (1.0 / l_scr[si, 0:1])
        o_ref[0, :, si * pair_w:(si + 1) * pair_w] = jnp.concatenate(
            [o[:, :tq].T, o[:, tq:].T], axis=1).astype(BF16)


def _moba_attn(rel_bias, q, k, vt, mask, table):
    b, s, dq = q.shape
    nb = k.shape[2]
    tq = MOBA_BLOCK
    ns = ATTN_STREAMS
    gw = ns * KV_GROUP * HEAD_DIM
    lanes = KV_GROUP * tq
    return pl.pallas_call(
        _attn_kernel,
        grid=(b, N_KV_HEADS // ns, nb),
        in_specs=[pl.BlockSpec(memory_space=pltpu.SMEM),
                  pl.BlockSpec((1, tq, gw), lambda i, g, j: (i, j, g)),
                  pl.BlockSpec((1, tq, gw), lambda i, g, j: (i, jnp.minimum(j + 1, nb - 1), g)),
                  pl.BlockSpec((1, ns, nb, tq, HEAD_DIM), lambda i, g, j: (i, g, 0, 0, 0),
                               pipeline_mode=pl.Buffered(1)),
                  pl.BlockSpec((1, ns, nb, HEAD_DIM, tq), lambda i, g, j: (i, g, 0, 0, 0),
                               pipeline_mode=pl.Buffered(1)),
                  pl.BlockSpec((1, ns, 1, nb, lanes), lambda i, g, j: (i, g, j, 0, 0)),
                  pl.BlockSpec((1, ns, 1, nb, lanes),
                               lambda i, g, j: (i, g, jnp.minimum(j + 1, nb - 1), 0, 0)),
                  pl.BlockSpec((ns * KV_GROUP, 2 * MOBA_BLOCK, MOBA_BLOCK), lambda i, g, j: (g, 0, 0))],
        out_specs=pl.BlockSpec((1, tq, gw), lambda i, g, j: (i, j, g)),
        out_shape=jax.ShapeDtypeStruct((b, s, dq), BF16),
        scratch_shapes=[pltpu.VMEM((ns, SUBLANES, lanes), F32),
                        pltpu.VMEM((ns, SUBLANES, lanes), F32),
                        pltpu.VMEM((ns, HEAD_DIM, lanes), F32),
                        pltpu.VMEM((ns, GROUP, MOBA_BLOCK, lanes), F32),
                        pltpu.VMEM((ns, GROUP, MOBA_BLOCK, lanes), F32),
                        pltpu.VMEM((ns, SUBLANES, lanes), F32),
                        pltpu.VMEM((ns, SUBLANES, lanes), F32),
                        pltpu.VMEM((ns, SUBLANES, lanes), F32),
                        pltpu.VMEM((ns, SUBLANES, lanes), F32),
                        pltpu.SMEM((1,), jnp.int32)],
        compiler_params=_cparams(3),
        name="moba_attn",
    )(rel_bias.astype(F32), q, q, k, vt, mask, mask, table)


def kernel(x, c, mod_w, mod_b, norm_mix, norm_ffn, lru_w_in, lru_conv_w, lru_conv_b, lru_w_gates,
           lru_b_gates, lru_lambda, lru_w_out, kv_mod_w, kv_mod_b, kv_norm, w_kv, attn_w_q, attn_w_o,
           rel_bias, ffn_w_gate, ffn_w_up, ffn_w_down, final_norm):
    b, s, d = x.shape
    assert s % TM_FFN == 0 and s % MOBA_BLOCK == 0 and b <= SUBLANES
    dkv = N_KV_HEADS * HEAD_DIM

    c_pad = jnp.zeros((SUBLANES, d), F32).at[:b].set(c.astype(F32))
    mod = _adaln_mod(c_pad, b, mod_w, mod_b)[:, :b]
    kvmod = _adaln_mod(c_pad, b, kv_mod_w[None], kv_mod_b[None])[0, :b]
    mod0 = mod[0][:, None, :]
    mod1 = mod[1][:, None, :]
    kvmod = kvmod[:, None, :]

    x = _rglru_layer(x, mod0, norm_mix[0], lru_w_in[0].astype(BF16), lru_conv_w[0], lru_conv_b[0],
                     lru_w_gates[0].astype(BF16), lru_b_gates[0], lru_lambda[0], lru_w_out[0].astype(BF16))
    ffn_w = (ffn_w_gate.astype(BF16), ffn_w_up.astype(BF16), ffn_w_down.astype(BF16))
    x = _ffn_layer(x, mod0, norm_ffn[0], 0, *ffn_w)

    q, k, vt, mask = _qkv_gate(x, kvmod, mod1, kv_norm, norm_mix[1], w_kv[:, :dkv].astype(BF16),
                               w_kv[:, dkv:].T.astype(BF16), attn_w_q[0].astype(BF16))
    table = _t5_table(rel_bias)
    attn = _moba_attn(rel_bias, q, k, vt, mask, table)
    return _ffn_layer(x, mod1, norm_ffn[1], 1, *ffn_w, attn=attn, w_o=attn_w_o[0].astype(BF16),
                      final_gain=final_norm)
```

```python
import functools
import math

import numpy as np
import jax
import jax.numpy as jnp
from jax import lax
from jax.experimental import pallas as pl
from jax.experimental.pallas import tpu as pltpu

F32 = jnp.float32
BF16 = jnp.bfloat16

LRU_BLOCK = 256
CONV_WIDTH = 4
LRU_C = 8.0
N_HEADS = 8
N_KV_HEADS = 4
HEAD_DIM = 128
KV_GROUP = N_HEADS // N_KV_HEADS
MOBA_BLOCK = 256
MOBA_TOPK = 3
REL_BUCKETS = 32
REL_MAX_DIST = 128
RMS_EPS = 1e-6
NEG_INF = -1e30
LOG2E = math.log2(math.e)
GELU_C1 = 2.0 * math.sqrt(2.0 / math.pi)
GELU_C2 = GELU_C1 * 0.044715

SUBLANES = 8
LANES = 128
VMEM_LIMIT = 56 * 1024 * 1024

LRU_TILE = 256
TM_LRU = 512
TM_FFN = 512
FF_CHUNK = 256
MOD_TN = 2048
GROUP = 4
NEAR_FAR = GROUP - 2
ATTN_STREAMS = 4


def _cparams(n_axes):
    return pltpu.CompilerParams(dimension_semantics=("arbitrary",) * n_axes,
                                vmem_limit_bytes=VMEM_LIMIT)


def _const_spec(shape):
    nd = len(shape)
    return pl.BlockSpec(shape, lambda *_: (0,) * nd, pipeline_mode=pl.Buffered(1))


def _dot(a, b):
    return jnp.dot(a, b, preferred_element_type=F32)


def _dot_nt(a, b, precision=None):
    return lax.dot_general(a, b, (((1,), (1,)), ((), ())), preferred_element_type=F32,
                           precision=precision)


def _norm_mod(xt, gain, shift, scale):
    inv = lax.rsqrt(jnp.mean(xt * xt, axis=-1, keepdims=True) + RMS_EPS)
    return (xt * inv) * (gain * (1.0 + scale)) + shift


def _mod_kernel(c_ref, w_ref, b_ref, o_ref, *, nbatch):
    c = c_ref[...]
    cs_t = (c * jax.nn.sigmoid(c)).T
    w = w_ref[0]
    o_ref[0] = jnp.zeros(o_ref.shape[1:], F32)
    for bi in range(nbatch):
        o_ref[0, bi:bi + 1, :] = jnp.sum(cs_t[:, bi:bi + 1] * w, axis=0, keepdims=True) + b_ref[0]


def _adaln_mod(c_pad, nbatch, w, b):
    n_layers, d, n = w.shape
    tn = min(MOD_TN, n)
    return pl.pallas_call(
        functools.partial(_mod_kernel, nbatch=nbatch),
        grid=(n_layers, n // tn),
        in_specs=[pl.BlockSpec((SUBLANES, d), lambda l, j: (0, 0)),
                  pl.BlockSpec((1, d, tn), lambda l, j: (l, 0, j)),
                  pl.BlockSpec((1, 1, tn), lambda l, j: (l, 0, j))],
        out_specs=pl.BlockSpec((1, SUBLANES, tn), lambda l, j: (l, 0, j)),
        out_shape=jax.ShapeDtypeStruct((n_layers, SUBLANES, n), F32),
        compiler_params=_cparams(2),
        name="adaln_mod",
    )(c_pad, w, b.reshape(n_layers, 1, n))


def _t5_thresholds():
    max_exact = REL_BUCKETS // 2
    d = np.arange(0, 4 * REL_MAX_DIST)
    dd = np.maximum(d, 1).astype(np.float32)
    large = max_exact + (np.log(dd / max_exact) / math.log(REL_MAX_DIST / max_exact)
                         * (REL_BUCKETS - max_exact)).astype(np.int32)
    large = np.minimum(large, REL_BUCKETS - 1)
    bucket = np.where(d < max_exact, d, large)
    return [int(np.argmax(bucket >= b)) for b in range(REL_BUCKETS)]


def _t5_kernel(rb_ref, o_ref, *, thr):
    h = pl.program_id(0)
    shape = (2 * MOBA_BLOCK, MOBA_BLOCK)
    key = lax.broadcasted_iota(jnp.int32, shape, 0)
    qry = lax.broadcasted_iota(jnp.int32, shape, 1)
    dist = qry + MOBA_BLOCK - key
    val = jnp.full(shape, rb_ref[h, REL_BUCKETS - 1], F32)
    for b in range(REL_BUCKETS - 2, -1, -1):
        val = jnp.where(dist < thr[b + 1], rb_ref[h, b], val)
    o_ref[0] = jnp.where(dist < 0, NEG_INF, LOG2E * val)


def _t5_table(rel_bias):
    return pl.pallas_call(
        functools.partial(_t5_kernel, thr=_t5_thresholds()),
        grid=(N_HEADS,),
        in_specs=[pl.BlockSpec(memory_space=pltpu.SMEM)],
        out_specs=pl.BlockSpec((1, 2 * MOBA_BLOCK, MOBA_BLOCK), lambda h: (h, 0, 0)),
        out_shape=jax.ShapeDtypeStruct((N_HEADS, 2 * MOBA_BLOCK, MOBA_BLOCK), F32),
        compiler_params=_cparams(1),
        name="t5_table",
    )(rel_bias.astype(F32))


def _time_permutation(tm):
    seg = tm // SUBLANES
    row = np.arange(tm)
    perm = np.zeros((tm, tm), np.float32)
    perm[row, (row % SUBLANES) * seg + row // SUBLANES] = 1.0
    return jnp.asarray(perm, BF16), jnp.asarray(perm.T, BF16)


def _vrow(v, g):
    return v[g * SUBLANES:(g + 1) * SUBLANES]


def _sublane_scan(a, b, h0):
    sub = lax.broadcasted_iota(jnp.int32, a.shape, 0)
    s = 1
    while s < SUBLANES:
        a_sh = jnp.where(sub < s, 1.0, pltpu.roll(a, s, 0))
        b_sh = jnp.where(sub < s, 0.0, pltpu.roll(b, s, 0))
        b = a * b_sh + b
        a = a * a_sh
        s *= 2
    return a * h0 + b


def _rglru_kernel(x_ref, mod_ref, g_ref, perm_ref, permt_ref, win_ref, cw_ref, cb_ref, wg_ref, bg_ref,
                  lam_ref, wout_ref, o_ref, conv_scr, h_scr):
    nbatch = x_ref.shape[0]
    d = x_ref.shape[-1]
    w = lam_ref.shape[-1]
    tm = perm_ref.shape[0]
    seg = tm // SUBLANES
    halo = CONV_WIDTH - 1

    @pl.when(pl.program_id(0) == 0)
    def _():
        conv_scr[...] = jnp.zeros_like(conv_scr)
        h_scr[...] = jnp.zeros_like(h_scr)

    sub = lax.broadcasted_iota(jnp.int32, (SUBLANES, LRU_BLOCK), 0)
    blocks = [(hd * LRU_BLOCK, (hd + 1) * LRU_BLOCK) for hd in range(w // LRU_BLOCK)]
    keys = [(ti, b) for ti in range(x_ref.shape[1] // tm) for b in range(nbatch)]
    streams = [(bi, hd) for bi in keys for hd in range(len(blocks))]
    hp, uy, ux, xc, gates, hy = {}, {}, {}, {}, {}, {}

    def rows(bi):
        return slice(bi[0] * tm, (bi[0] + 1) * tm)

    def normalize(bi):
        mod = mod_ref[bi[1]]
        h = _norm_mod(x_ref[bi[1], rows(bi), :], g_ref[...], mod[:, 0:d], mod[:, d:2 * d]).astype(BF16)
        hp[bi] = _dot(perm_ref[...], h).astype(BF16)

    def in_proj(bi, hd):
        lo, hi = blocks[hd]
        uy[bi, hd] = _dot(hp[bi], win_ref[:, lo:hi])
        ux[bi, hd] = _dot(hp[bi], win_ref[:, w + lo:w + hi])

    def conv(bi, hd):
        lo, hi = blocks[hd]
        u = ux[bi, hd]
        tops = []
        for i in range(halo):
            cur = pltpu.roll(_vrow(u, seg - halo + i), 1, 0)
            prv = pltpu.roll(conv_scr[bi[1], i, :, lo:hi], 1, 0)
            tops.append(jnp.where(sub == 0, prv, cur))
            conv_scr[bi[1], i, :, lo:hi] = _vrow(u, seg - halo + i)
        cw = cw_ref[:, lo:hi]
        v = u * cw[CONV_WIDTH - 1:CONV_WIDTH]
        for k in range(1, CONV_WIDTH):
            shifted = jnp.concatenate(tops[halo - k:] + [u[:tm - k * SUBLANES]], axis=0)
            v = v + shifted * cw[CONV_WIDTH - 1 - k:CONV_WIDTH - k]
        xc[bi, hd] = v + cb_ref[:, lo:hi]

    def gate_proj(bi, hd):
        gates[bi, hd] = _dot(xc[bi, hd].astype(BF16), wg_ref[hd])

    def recur(bi, hd):
        lo, hi = blocks[hd]
        u, v, gt = uy[bi, hd], xc[bi, hd], gates[bi, hd]
        y = u * jax.nn.sigmoid(u * (GELU_C1 + GELU_C2 * (u * u)))
        r = jax.nn.sigmoid(gt[:, :LRU_BLOCK] + bg_ref[0:1, lo:hi])
        i_gate = jax.nn.sigmoid(gt[:, LRU_BLOCK:] + bg_ref[1:2, lo:hi])
        lam = lam_ref[:, lo:hi]
        softplus_neg = jnp.maximum(-lam, 0.0) + jnp.log1p(jnp.exp(-jnp.abs(lam)))
        neg_log_a = r * (LRU_C * softplus_neg)
        a = jnp.exp(-neg_log_a)
        one_minus_a2 = (1.0 + a * a) * jnp.tanh(neg_log_a)
        mult = jnp.where(one_minus_a2 == 0.0, 0.0, one_minus_a2 * lax.rsqrt(one_minus_a2))
        b = mult * (i_gate * v)
        loc = [_vrow(b, 0)]
        prod = [_vrow(a, 0)]
        for g in range(1, seg):
            ag = _vrow(a, g)
            loc.append(ag * loc[-1] + _vrow(b, g))
            prod.append(ag * prod[-1])
        h_in = h_scr[bi[1], 0:1, lo:hi]
        seg_end = _sublane_scan(prod[-1], loc[-1], h_in)
        seg_in = jnp.where(sub == 0, h_in, pltpu.roll(seg_end, 1, 0))
        h_scr[bi[1], 0:1, lo:hi] = seg_end[SUBLANES - 1:SUBLANES]
        hs = jnp.concatenate([loc[g] + prod[g] * seg_in for g in range(seg)], axis=0)
        hy[bi, hd] = (hs * y).astype(BF16)

    def out_proj(bi):
        hy_seg = jnp.concatenate([hy[bi, hd] for hd in range(len(blocks))], axis=1)
        hy_time = _dot(permt_ref[...], hy_seg).astype(BF16)
        o_ref[bi[1], rows(bi), :] = (x_ref[bi[1], rows(bi), :]
                                     + mod_ref[bi[1]][:, 2 * d:3 * d] * _dot(hy_time, wout_ref[...]))

    stages = (in_proj, conv, gate_proj, recur)
    normalize(keys[0])
    for slot in range(len(streams) + len(stages) - 1):
        for depth, stage in enumerate(stages):
            i = slot - depth
            if 0 <= i < len(streams):
                bi, hd = streams[i]
                nxt = keys.index(bi) + 1
                if stage is in_proj and hd == len(blocks) - 2 and nxt < len(keys):
                    normalize(keys[nxt])
                stage(bi, hd)
                if stage is recur and hd == len(blocks) - 1:
                    out_proj(bi)


def _rglru_layer(x, mod, gain, w_in, conv_w, conv_b, w_gates, b_gates, lam, w_out):
    b, s, d = x.shape
    w = lam.shape[-1]
    tm = TM_LRU
    perm, perm_t = _time_permutation(LRU_TILE)
    return pl.pallas_call(
        _rglru_kernel,
        grid=(s // tm,),
        in_specs=[pl.BlockSpec((b, tm, d), lambda t: (0, t, 0)),
                  _const_spec(mod.shape),
                  _const_spec((1, d)),
                  _const_spec(perm.shape),
                  _const_spec(perm_t.shape),
                  _const_spec(w_in.shape),
                  _const_spec(conv_w.shape),
                  _const_spec((1, w)),
                  _const_spec(w_gates.shape),
                  _const_spec(b_gates.shape),
                  _const_spec((1, w)),
                  _const_spec(w_out.shape)],
        out_specs=pl.BlockSpec((b, tm, d), lambda t: (0, t, 0)),
        out_shape=jax.ShapeDtypeStruct(x.shape, F32),
        scratch_shapes=[pltpu.VMEM((b, CONV_WIDTH - 1, SUBLANES, w), F32),
                        pltpu.VMEM((b, SUBLANES, w), F32)],
        compiler_params=_cparams(1),
        name="rglru",
    )(x, mod, gain.reshape(1, d), perm, perm_t, w_in, conv_w, conv_b.reshape(1, w), w_gates, b_gates,
      lam.reshape(1, w), w_out)


def _ffn_kernel(*refs, has_attn, final):
    x_ref, mod_ref, g_ref, wg_ref, wu_ref, wd_ref = refs[:6]
    rest = list(refs[6:])
    att_ref = wo_ref = fn_ref = None
    if has_attn:
        att_ref, wo_ref = rest[:2]
        rest = rest[2:]
    if final:
        fn_ref = rest[0]
        rest = rest[1:]
    o_ref, = rest
    d = x_ref.shape[-1]
    ff = wg_ref.shape[-1]
    tm = x_ref.shape[1]

    xt = x_ref[0]
    mod = mod_ref[0]
    if has_attn:
        xt = xt + mod[:, 2 * d:3 * d] * _dot(att_ref[0], wo_ref[...])
    h = _norm_mod(xt, g_ref[...], mod[:, 3 * d:4 * d], mod[:, 4 * d:5 * d]).astype(BF16)
    acc = jnp.zeros((tm, d), F32)
    for c in range(ff // FF_CHUNK):
        lo = c * FF_CHUNK
        hi = lo + FF_CHUNK
        gate = _dot(h, wg_ref[0, :, lo:hi].astype(BF16))
        up = _dot(h, wu_ref[0, :, lo:hi].astype(BF16))
        act = (gate * jax.nn.sigmoid(gate)) * up
        acc = acc + _dot(act.astype(BF16), wd_ref[0, lo:hi, :].astype(BF16))
    y = xt + mod[:, 5 * d:6 * d] * acc
    if final:
        y = y * lax.rsqrt(jnp.mean(y * y, axis=-1, keepdims=True) + RMS_EPS) * fn_ref[...]
    o_ref[0] = y


def _layer_spec(shape, layer):
    nd = len(shape)
    return pl.BlockSpec((1,) + tuple(shape[1:]), lambda *_: (layer,) + (0,) * (nd - 1),
                        pipeline_mode=pl.Buffered(1))


def _ffn_layer(x, mod, gain, layer, w_gate, w_up, w_down, attn=None, w_o=None, final_gain=None):
    b, s, d = x.shape
    tm = TM_FFN
    has_attn = attn is not None
    final = final_gain is not None
    in_specs = [pl.BlockSpec((1, tm, d), lambda i, t: (i, t, 0)),
                pl.BlockSpec((1, 1, mod.shape[-1]), lambda i, t: (i, 0, 0)),
                _const_spec((1, d)),
                _layer_spec(w_gate.shape, layer),
                _layer_spec(w_up.shape, layer),
                _layer_spec(w_down.shape, layer)]
    args = [x, mod, gain.reshape(1, d), w_gate, w_up, w_down]
    if has_attn:
        in_specs += [pl.BlockSpec((1, tm, d), lambda i, t: (i, t, 0)), _const_spec(w_o.shape)]
        args += [attn, w_o]
    if final:
        in_specs += [_const_spec((1, d))]
        args += [final_gain.reshape(1, d)]
    return pl.pallas_call(
        functools.partial(_ffn_kernel, has_attn=has_attn, final=final),
        grid=(b, s // tm),
        in_specs=in_specs,
        out_specs=pl.BlockSpec((1, tm, d), lambda i, t: (i, t, 0)),
        out_shape=jax.ShapeDtypeStruct(x.shape, F32),
        compiler_params=_cparams(2),
        name="ffn_attn_final" if has_attn else "ffn",
    )(*args)


def _qkv_kernel(x_ref, xn_ref, kvmod_ref, mod_ref, kvg_ref, qg_ref, wk_ref, wvt_ref, wq_ref,
                q_ref, k_ref, vt_ref, mask_ref, km_scr, hqa_scr, hkva_scr, hqb_scr, hkvb_scr):
    nbatch = x_ref.shape[0]
    d = x_ref.shape[-1]
    tm = MOBA_BLOCK
    nb = mask_ref.shape[3]
    step = pl.program_id(0)
    buf_a = (hqa_scr, hkva_scr)
    buf_b = (hqb_scr, hkvb_scr)
    blk = lax.broadcasted_iota(jnp.int32, (nb, tm), 0).astype(F32)
    q_scale = HEAD_DIM ** -0.5 * LOG2E
    pair_w = KV_GROUP * HEAD_DIM

    def normalize(src_ref, row0, bufs):
        for bi in range(nbatch):
            xt = src_ref[bi, row0:row0 + tm, :]
            xn = xt * lax.rsqrt(jnp.mean(xt * xt, axis=-1, keepdims=True) + RMS_EPS)
            mod = mod_ref[bi]
            kvmod = kvmod_ref[bi]
            rows = pl.ds(bi * tm, tm)
            bufs[0][rows, :] = (xn * (qg_ref[...] * (1.0 + mod[:, d:2 * d])) + mod[:, 0:d]).astype(BF16)
            bufs[1][rows, :] = (xn * (kvg_ref[...] * (1.0 + kvmod[:, d:2 * d])) + kvmod[:, 0:d]).astype(BF16)

    def process(half, cur, nxt_src, nxt_row0, nxt):
        j = 2 * step + half
        past = blk < j.astype(F32)
        out_rows = slice(half * tm, (half + 1) * tm)
        q_pair = {}
        h_q = cur[0][...]
        h_kv = cur[1][...]
        normalize(nxt_src, nxt_row0, nxt)

        def q_proj(hk):
            qp = _dot(h_q, wq_ref[:, hk * pair_w:(hk + 1) * pair_w]) * q_scale
            for bi in range(nbatch):
                q_ref[bi, out_rows, hk * pair_w:(hk + 1) * pair_w] = qp[bi * tm:(bi + 1) * tm].astype(BF16)
            q_pair[hk] = qp

        def select_blocks(bi, hq):
            hk, sub = divmod(hq, KV_GROUP)
            q_h = q_pair[hk][bi * tm:(bi + 1) * tm, sub * HEAD_DIM:(sub + 1) * HEAD_DIM]
            km_h = km_scr[bi, :, hk * HEAD_DIM:(hk + 1) * HEAD_DIM]
            gate = _dot_nt(km_h.astype(BF16), q_h.astype(BF16))
            gate = jnp.where(past, gate, NEG_INF)
            mask = jnp.full((nb, tm), NEG_INF, F32)
            for _ in range(MOBA_TOPK):
                best = jnp.max(gate, axis=0, keepdims=True)
                first = jnp.min(jnp.where(gate == best, blk, nb), axis=0, keepdims=True)
                hit = blk == first
                mask = jnp.where(hit, 0.0, mask)
                gate = jnp.where(hit, -jnp.inf, gate)
            mask = jnp.where(past, mask, NEG_INF)
            mask_ref[bi, hk, half, :, sub * tm:(sub + 1) * tm] = mask

        def k_proj():
            k = _dot(h_kv, wk_ref[...])
            for bi in range(nbatch):
                kb = k[bi * tm:(bi + 1) * tm]
                for hk in range(N_KV_HEADS):
                    k_ref[bi, hk, half] = kb[:, hk * HEAD_DIM:(hk + 1) * HEAD_DIM].astype(BF16)
            return k

        def v_proj():
            vt = _dot_nt(wvt_ref[...], h_kv)
            for bi in range(nbatch):
                for hk in range(N_KV_HEADS):
                    vt_ref[bi, hk, half] = vt[hk * HEAD_DIM:(hk + 1) * HEAD_DIM,
                                              bi * tm:(bi + 1) * tm].astype(BF16)

        k = None
        q_proj(0)
        for hk in range(N_KV_HEADS):
            if hk + 1 < N_KV_HEADS:
                q_proj(hk + 1)
            if hk == N_KV_HEADS - 2:
                k = k_proj()
            if hk == N_KV_HEADS - 1:
                v_proj()
            for bi in range(nbatch):
                for sub in range(KV_GROUP):
                    select_blocks(bi, hk * KV_GROUP + sub)
        for bi in range(nbatch):
            km_scr[bi, pl.ds(j, 1), :] = jnp.mean(k[bi * tm:(bi + 1) * tm], axis=0, keepdims=True)

    @pl.when(step == 0)
    def _():
        km_scr[...] = jnp.zeros_like(km_scr)
        normalize(x_ref, 0, buf_a)

    process(0, buf_a, x_ref, tm, buf_b)
    process(1, buf_b, xn_ref, 0, buf_a)


def _qkv_gate(x, kvmod, mod, kv_gain, q_gain, w_k, w_vt, w_q):
    b, s, d = x.shape
    tm = MOBA_BLOCK
    nb = s // tm
    dkv = N_KV_HEADS * HEAD_DIM
    dq = N_HEADS * HEAD_DIM
    out_shape = (jax.ShapeDtypeStruct((b, s, dq), BF16),
                 jax.ShapeDtypeStruct((b, N_KV_HEADS, nb, tm, HEAD_DIM), BF16),
                 jax.ShapeDtypeStruct((b, N_KV_HEADS, nb, HEAD_DIM, tm), BF16),
                 jax.ShapeDtypeStruct((b, N_KV_HEADS, nb, nb, KV_GROUP * tm), F32))
    out_specs = (pl.BlockSpec((b, 2 * tm, dq), lambda t: (0, t, 0)),
                 pl.BlockSpec((b, N_KV_HEADS, 2, tm, HEAD_DIM), lambda t: (0, 0, t, 0, 0)),
                 pl.BlockSpec((b, N_KV_HEADS, 2, HEAD_DIM, tm), lambda t: (0, 0, t, 0, 0)),
                 pl.BlockSpec((b, N_KV_HEADS, 2, nb, KV_GROUP * tm), lambda t: (0, 0, t, 0, 0)))
    return pl.pallas_call(
        _qkv_kernel,
        grid=(nb // 2,),
        in_specs=[pl.BlockSpec((b, 2 * tm, d), lambda t: (0, t, 0)),
                  pl.BlockSpec((b, tm, d), lambda t: (0, jnp.minimum(2 * t + 2, nb - 1), 0)),
                  _const_spec(kvmod.shape),
                  _const_spec(mod.shape),
                  _const_spec((1, d)),
                  _const_spec((1, d)),
                  _const_spec(w_k.shape),
                  _const_spec(w_vt.shape),
                  _const_spec(w_q.shape)],
        out_specs=out_specs,
        out_shape=out_shape,
        scratch_shapes=[pltpu.VMEM((b, nb, dkv), F32),
                        pltpu.VMEM((b * tm, d), BF16),
                        pltpu.VMEM((b * tm, d), BF16),
                        pltpu.VMEM((b * tm, d), BF16),
                        pltpu.VMEM((b * tm, d), BF16)],
        compiler_params=_cparams(1),
        name="qkv_gate",
    )(x, x, kvmod, mod, kv_gain.reshape(1, d), q_gain.reshape(1, d), w_k, w_vt, w_q)


def _attn_kernel(rb_ref, q_ref, qn_ref, k_ref, vt_ref, mask_ref, maskn_ref, tab_ref, o_ref, m_scr, l_scr,
                 acc_scr, sa_scr, sb_scr, cma_scr, cmb_scr, rowa_scr, rowb_scr, phase_scr):
    g0 = pl.program_id(1) * ATTN_STREAMS
    j = pl.program_id(2)
    nb = k_ref.shape[2]
    tq = q_ref.shape[1]
    pair_w = KV_GROUP * HEAD_DIM
    streams = range(ATTN_STREAMS)

    def stack_heads(qref, si):
        lo = si * pair_w
        return jnp.concatenate([qref[0, :, lo:lo + HEAD_DIM], qref[0, :, lo + HEAD_DIM:lo + pair_w]], axis=0)

    qs = [stack_heads(q_ref, si) for si in streams]

    def scores(si, n, qx):
        kb = k_ref[0, si, n]
        return [_dot_nt(kb, qx[hq * tq:(hq + 1) * tq]) for hq in range(KV_GROUP)]

    def tile_bias(si, lo):
        return [tab_ref[KV_GROUP * si + hq, lo:lo + MOBA_BLOCK, :] for hq in range(KV_GROUP)]

    def mask_row(si, n, mref):
        return mref[0, si, 0, pl.ds(n, 1), :]

    def col_max(s):
        return jnp.max(s, axis=0, keepdims=True)

    def col_sum(s):
        return jnp.sum(s, axis=0, keepdims=True)

    lane = lax.broadcasted_iota(jnp.int32, (1, KV_GROUP * tq), 1)
    far_bias = [LOG2E * jnp.where(lane < tq, rb_ref[KV_GROUP * (g0 + si), REL_BUCKETS - 1],
                                  rb_ref[KV_GROUP * (g0 + si) + 1, REL_BUCKETS - 1]) for si in streams]

    def far_row(si, n, valid, mref):
        return jnp.where(valid, mask_row(si, n, mref) + far_bias[si], NEG_INF)

    n_far = jnp.maximum(j - 1 - NEAR_FAR, 0)
    n_groups = 1 + (n_far + GROUP - 1) // GROUP
    jp = jnp.maximum(j - 1, 0)
    head_blocks = [j, jp] + [jnp.maximum(j - 2 - u, 0) for u in range(NEAR_FAR)]

    def group_blocks(k):
        return [jnp.where(k == 0, head_blocks[u], jnp.minimum((k - 1) * GROUP + u, nb - 1))
                for u in range(GROUP)]

    def put(buf, si, u, halves, row):
        s_buf, cm_buf, row_buf = buf
        for hq, s in enumerate(halves):
            cols = slice(hq * tq, (hq + 1) * tq)
            s_buf[si, u, :, cols] = s
            cm_buf[si, u:u + 1, cols] = col_max(s)
        row_buf[si, u:u + 1] = row

    def head_slot(si, u, jq, qx, mref):
        if u == 0:
            return ([s + t for s, t in zip(scores(si, jq, qx), tile_bias(si, MOBA_BLOCK))],
                    jnp.zeros((1, KV_GROUP * tq), F32))
        if u == 1:
            jqp = jnp.maximum(jq - 1, 0)
            return ([s + t for s, t in zip(scores(si, jqp, qx), tile_bias(si, 0))],
                    jnp.where(jq >= 1, mask_row(si, jqp, mref), NEG_INF))
        n = jnp.maximum(jq - u, 0)
        return scores(si, n, qx), far_row(si, n, jq - u >= 0, mref)

    j_next = jnp.minimum(j + 1, nb - 1)

    def step(si, k, buf, nxt, last):
        s_buf, cm_buf, row_buf = buf
        blocks = group_blocks(k)
        m_old = m_scr[si, 0:1]
        rows = [row_buf[si, u:u + 1] for u in range(GROUP)]
        m_new = m_old
        for u in range(GROUP):
            m_new = jnp.maximum(m_new, cm_buf[si, u:u + 1] + rows[u])
        alpha = jnp.exp2(m_old - m_new)
        l_new = alpha * l_scr[si, 0:1]
        pv = None
        qs_next = stack_heads(qn_ref, si) if last else None

        def stage1(u):
            if last:
                put(nxt, si, u, *head_slot(si, u, j_next, qs_next, maskn_ref))
            else:
                n = k * GROUP + u
                nc = jnp.minimum(n, nb - 1)
                put(nxt, si, u, scores(si, nc, qs[si]), far_row(si, nc, n < n_far, mask_ref))

        for u in range(GROUP):
            stage1(u)
            shift = rows[u] - m_new
            halves = []
            for hq in range(KV_GROUP):
                cols = slice(hq * tq, (hq + 1) * tq)
                p = jnp.exp2(s_buf[si, u, :, cols] + shift[:, cols])
                halves.append((col_sum(p), _dot(vt_ref[0, si, blocks[u]], p.astype(BF16))))
            l_new = l_new + jnp.concatenate([h[0] for h in halves], axis=1)
            d = jnp.concatenate([h[1] for h in halves], axis=1)
            pv = d if pv is None else pv + d
        acc_scr[si] = alpha * acc_scr[si] + pv
        l_scr[si, 0:1] = l_new
        m_scr[si, 0:1] = m_new

    buf_a = (sa_scr, cma_scr, rowa_scr)
    buf_b = (sb_scr, cmb_scr, rowb_scr)
    m_scr[...] = jnp.full_like(m_scr, NEG_INF)
    l_scr[...] = jnp.zeros_like(l_scr)
    acc_scr[...] = jnp.zeros_like(acc_scr)

    @pl.when(j == 0)
    def _():
        for si in streams:
            for u in range(GROUP):
                put(buf_a, si, u, *head_slot(si, u, j, qs[si], mask_ref))
        phase_scr[0] = 0

    phase = phase_scr[0]

    def run(k, buf, nxt):
        @pl.when((k >= 0) & (k + 1 < n_groups))
        def _():
            for si in streams:
                step(si, k, buf, nxt, last=False)

        @pl.when((k >= 0) & (k + 1 == n_groups))
        def _():
            for si in streams:
                step(si, k, buf, nxt, last=True)

    def pair(t, carry):
        run(2 * t - phase, buf_a, buf_b)
        run(2 * t + 1 - phase, buf_b, buf_a)
        return carry

    lax.fori_loop(0, (n_groups + phase + 1) // 2, pair, 0)
    phase_scr[0] = (n_groups + phase) % 2

    for si in streams:
        o = acc_scr[si] * (1.0 / l_scr[si, 0:1])
        o_ref[0, :, si * pair_w:(si + 1) * pair_w] = jnp.concatenate(
            [o[:, :tq].T, o[:, tq:].T], axis=1).astype(BF16)


def _moba_attn(rel_bias, q, k, vt, mask, table):
    b, s, dq = q.shape
    nb = k.shape[2]
    tq = MOBA_BLOCK
    ns = ATTN_STREAMS
    gw = ns * KV_GROUP * HEAD_DIM
    lanes = KV_GROUP * tq
    return pl.pallas_call(
        _attn_kernel,
        grid=(b, N_KV_HEADS // ns, nb),
        in_specs=[pl.BlockSpec(memory_space=pltpu.SMEM),
                  pl.BlockSpec((1, tq, gw), lambda i, g, j: (i, j, g)),
                  pl.BlockSpec((1, tq, gw), lambda i, g, j: (i, jnp.minimum(j + 1, nb - 1), g)),
                  pl.BlockSpec((1, ns, nb, tq, HEAD_DIM), lambda i, g, j: (i, g, 0, 0, 0),
                               pipeline_mode=pl.Buffered(1)),
                  pl.BlockSpec((1, ns, nb, HEAD_DIM, tq), lambda i, g, j: (i, g, 0, 0, 0),
                               pipeline_mode=pl.Buffered(1)),
                  pl.BlockSpec((1, ns, 1, nb, lanes), lambda i, g, j: (i, g, j, 0, 0)),
                  pl.BlockSpec((1, ns, 1, nb, lanes),
                               lambda i, g, j: (i, g, jnp.minimum(j + 1, nb - 1), 0, 0)),
                  pl.BlockSpec((ns * KV_GROUP, 2 * MOBA_BLOCK, MOBA_BLOCK), lambda i, g, j: (g, 0, 0))],
        out_specs=pl.BlockSpec((1, tq, gw), lambda i, g, j: (i, j, g)),
        out_shape=jax.ShapeDtypeStruct((b, s, dq), BF16),
        scratch_shapes=[pltpu.VMEM((ns, SUBLANES, lanes), F32),
                        pltpu.VMEM((ns, SUBLANES, lanes), F32),
                        pltpu.VMEM((ns, HEAD_DIM, lanes), F32),
                        pltpu.VMEM((ns, GROUP, MOBA_BLOCK, lanes), F32),
                        pltpu.VMEM((ns, GROUP, MOBA_BLOCK, lanes), F32),
                        pltpu.VMEM((ns, SUBLANES, lanes), F32),
                        pltpu.VMEM((ns, SUBLANES, lanes), F32),
                        pltpu.VMEM((ns, SUBLANES, lanes), F32),
                        pltpu.VMEM((ns, SUBLANES, lanes), F32),
                        pltpu.SMEM((1,), jnp.int32)],
        compiler_params=_cparams(3),
        name="moba_attn",
    )(rel_bias.astype(F32), q, q, k, vt, mask, mask, table)


def kernel(x, c, mod_w, mod_b, norm_mix, norm_ffn, lru_w_in, lru_conv_w, lru_conv_b, lru_w_gates,
           lru_b_gates, lru_lambda, lru_w_out, kv_mod_w, kv_mod_b, kv_norm, w_kv, attn_w_q, attn_w_o,
           rel_bias, ffn_w_gate, ffn_w_up, ffn_w_down, final_norm):
    b, s, d = x.shape
    assert s % TM_FFN == 0 and s % MOBA_BLOCK == 0 and b <= SUBLANES
    dkv = N_KV_HEADS * HEAD_DIM

    c_pad = jnp.zeros((SUBLANES, d), F32).at[:b].set(c.astype(F32))
    mod = _adaln_mod(c_pad, b, mod_w, mod_b)[:, :b]
    kvmod = _adaln_mod(c_pad, b, kv_mod_w[None], kv_mod_b[None])[0, :b]
    mod0 = mod[0][:, None, :]
    mod1 = mod[1][:, None, :]
    kvmod = kvmod[:, None, :]

    x = _rglru_layer(x, mod0, norm_mix[0], lru_w_in[0].astype(BF16), lru_conv_w[0], lru_conv_b[0],
                     lru_w_gates[0].astype(BF16), lru_b_gates[0], lru_lambda[0], lru_w_out[0].astype(BF16))
    ffn_w = (ffn_w_gate, ffn_w_up, ffn_w_down)
    x = _ffn_layer(x, mod0, norm_ffn[0], 0, *ffn_w)

    q, k, vt, mask = _qkv_gate(x, kvmod, mod1, kv_norm, norm_mix[1], w_kv[:, :dkv].astype(BF16),
                               w_kv[:, dkv:].T.astype(BF16), attn_w_q[0].astype(BF16))
    table = _t5_table(rel_bias)
    attn = _moba_attn(rel_bias, q, k, vt, mask, table)
    return _ffn_layer(x, mod1, norm_ffn[1], 1, *ffn_w, attn=attn, w_o=attn_w_o[0].astype(BF16),
                      final_gain=final_norm)
```

```python
import functools
import math

import numpy as np
import jax
import jax.numpy as jnp
from jax import lax
from jax.experimental import pallas as pl
from jax.experimental.pallas import tpu as pltpu

F32 = jnp.float32
BF16 = jnp.bfloat16

LRU_BLOCK = 256
CONV_WIDTH = 4
LRU_C = 8.0
N_HEADS = 8
N_KV_HEADS = 4
HEAD_DIM = 128
KV_GROUP = N_HEADS // N_KV_HEADS
MOBA_BLOCK = 256
MOBA_TOPK = 3
REL_BUCKETS = 32
REL_MAX_DIST = 128
RMS_EPS = 1e-6
NEG_INF = -1e30
LOG2E = math.log2(math.e)
GELU_C1 = 2.0 * math.sqrt(2.0 / math.pi)
GELU_C2 = GELU_C1 * 0.044715

SUBLANES = 8
LANES = 128
VMEM_LIMIT = 56 * 1024 * 1024

LRU_TILE = 256
TM_LRU = 512
TM_FFN = 512
FF_CHUNK = 256
MOD_TN = 2048
GROUP = 4
NEAR_FAR = GROUP - 2
ATTN_STREAMS = 4


def _cparams(n_axes):
    return pltpu.CompilerParams(dimension_semantics=("arbitrary",) * n_axes,
                                vmem_limit_bytes=VMEM_LIMIT)


def _const_spec(shape):
    nd = len(shape)
    return pl.BlockSpec(shape, lambda *_: (0,) * nd, pipeline_mode=pl.Buffered(1))


def _dot(a, b):
    return jnp.dot(a, b, preferred_element_type=F32)


def _dot_nt(a, b, precision=None):
    return lax.dot_general(a, b, (((1,), (1,)), ((), ())), preferred_element_type=F32,
                           precision=precision)


def _norm_mod(xt, gain, shift, scale):
    inv = lax.rsqrt(jnp.mean(xt * xt, axis=-1, keepdims=True) + RMS_EPS)
    return (xt * inv) * (gain * (1.0 + scale)) + shift


def _mod_kernel(c_ref, w_ref, b_ref, o_ref, *, nbatch):
    c = c_ref[...]
    cs_t = (c * jax.nn.sigmoid(c)).T
    w = w_ref[0]
    o_ref[0] = jnp.zeros(o_ref.shape[1:], F32)
    for bi in range(nbatch):
        o_ref[0, bi:bi + 1, :] = jnp.sum(cs_t[:, bi:bi + 1] * w, axis=0, keepdims=True) + b_ref[0]


def _adaln_mod(c_pad, nbatch, w, b):
    n_layers, d, n = w.shape
    tn = min(MOD_TN, n)
    return pl.pallas_call(
        functools.partial(_mod_kernel, nbatch=nbatch),
        grid=(n_layers, n // tn),
        in_specs=[pl.BlockSpec((SUBLANES, d), lambda l, j: (0, 0)),
                  pl.BlockSpec((1, d, tn), lambda l, j: (l, 0, j)),
                  pl.BlockSpec((1, 1, tn), lambda l, j: (l, 0, j))],
        out_specs=pl.BlockSpec((1, SUBLANES, tn), lambda l, j: (l, 0, j)),
        out_shape=jax.ShapeDtypeStruct((n_layers, SUBLANES, n), F32),
        compiler_params=_cparams(2),
        name="adaln_mod",
    )(c_pad, w, b.reshape(n_layers, 1, n))


def _t5_thresholds():
    max_exact = REL_BUCKETS // 2
    d = np.arange(0, 4 * REL_MAX_DIST)
    dd = np.maximum(d, 1).astype(np.float32)
    large = max_exact + (np.log(dd / max_exact) / math.log(REL_MAX_DIST / max_exact)
                         * (REL_BUCKETS - max_exact)).astype(np.int32)
    large = np.minimum(large, REL_BUCKETS - 1)
    bucket = np.where(d < max_exact, d, large)
    return [int(np.argmax(bucket >= b)) for b in range(REL_BUCKETS)]


def _t5_kernel(rb_ref, o_ref, *, thr):
    h = pl.program_id(0)
    shape = (2 * MOBA_BLOCK, MOBA_BLOCK)
    key = lax.broadcasted_iota(jnp.int32, shape, 0)
    qry = lax.broadcasted_iota(jnp.int32, shape, 1)
    dist = qry + MOBA_BLOCK - key
    val = jnp.full(shape, rb_ref[h, REL_BUCKETS - 1], F32)
    for b in range(REL_BUCKETS - 2, -1, -1):
        val = jnp.where(dist < thr[b + 1], rb_ref[h, b], val)
    o_ref[0] = jnp.where(dist < 0, NEG_INF, LOG2E * val)


def _t5_table(rel_bias):
    return pl.pallas_call(
        functools.partial(_t5_kernel, thr=_t5_thresholds()),
        grid=(N_HEADS,),
        in_specs=[pl.BlockSpec(memory_space=pltpu.SMEM)],
        out_specs=pl.BlockSpec((1, 2 * MOBA_BLOCK, MOBA_BLOCK), lambda h: (h, 0, 0)),
        out_shape=jax.ShapeDtypeStruct((N_HEADS, 2 * MOBA_BLOCK, MOBA_BLOCK), F32),
        compiler_params=_cparams(1),
        name="t5_table",
    )(rel_bias.astype(F32))


def _time_permutation(tm):
    seg = tm // SUBLANES
    row = np.arange(tm)
    perm = np.zeros((tm, tm), np.float32)
    perm[row, (row % SUBLANES) * seg + row // SUBLANES] = 1.0
    return jnp.asarray(perm, BF16), jnp.asarray(perm.T, BF16)


def _vrow(v, g):
    return v[g * SUBLANES:(g + 1) * SUBLANES]


def _sublane_scan(a, b, h0):
    sub = lax.broadcasted_iota(jnp.int32, a.shape, 0)
    s = 1
    while s < SUBLANES:
        a_sh = jnp.where(sub < s, 1.0, pltpu.roll(a, s, 0))
        b_sh = jnp.where(sub < s, 0.0, pltpu.roll(b, s, 0))
        b = a * b_sh + b
        a = a * a_sh
        s *= 2
    return a * h0 + b


def _rglru_kernel(x_ref, mod_ref, g_ref, perm_ref, permt_ref, win32_ref, cw_ref, cb_ref, wg32_ref, bg_ref,
                  lam_ref, wout32_ref, o_ref, conv_scr, h_scr, win_ref, wg_ref, wout_ref):
    nbatch = x_ref.shape[0]
    d = x_ref.shape[-1]
    w = lam_ref.shape[-1]
    tm = perm_ref.shape[0]
    seg = tm // SUBLANES
    halo = CONV_WIDTH - 1

    @pl.when(pl.program_id(0) == 0)
    def _():
        conv_scr[...] = jnp.zeros_like(conv_scr)
        h_scr[...] = jnp.zeros_like(h_scr)
        for c in range(0, 2 * w, LRU_BLOCK):
            win_ref[:, c:c + LRU_BLOCK] = win32_ref[:, c:c + LRU_BLOCK].astype(BF16)
        for hd in range(w // LRU_BLOCK):
            wg_ref[hd] = wg32_ref[hd].astype(BF16)
            wout_ref[hd * LRU_BLOCK:(hd + 1) * LRU_BLOCK, :] = (
                wout32_ref[hd * LRU_BLOCK:(hd + 1) * LRU_BLOCK, :].astype(BF16))

    sub = lax.broadcasted_iota(jnp.int32, (SUBLANES, LRU_BLOCK), 0)
    blocks = [(hd * LRU_BLOCK, (hd + 1) * LRU_BLOCK) for hd in range(w // LRU_BLOCK)]
    keys = [(ti, b) for ti in range(x_ref.shape[1] // tm) for b in range(nbatch)]
    streams = [(bi, hd) for bi in keys for hd in range(len(blocks))]
    hp, uy, ux, xc, gates, hy = {}, {}, {}, {}, {}, {}

    def rows(bi):
        return slice(bi[0] * tm, (bi[0] + 1) * tm)

    def normalize(bi):
        mod = mod_ref[bi[1]]
        h = _norm_mod(x_ref[bi[1], rows(bi), :], g_ref[...], mod[:, 0:d], mod[:, d:2 * d]).astype(BF16)
        hp[bi] = _dot(perm_ref[...], h).astype(BF16)

    def in_proj(bi, hd):
        lo, hi = blocks[hd]
        uy[bi, hd] = _dot(hp[bi], win_ref[:, lo:hi])
        ux[bi, hd] = _dot(hp[bi], win_ref[:, w + lo:w + hi])

    def conv(bi, hd):
        lo, hi = blocks[hd]
        u = ux[bi, hd]
        tops = []
        for i in range(halo):
            cur = pltpu.roll(_vrow(u, seg - halo + i), 1, 0)
            prv = pltpu.roll(conv_scr[bi[1], i, :, lo:hi], 1, 0)
            tops.append(jnp.where(sub == 0, prv, cur))
            conv_scr[bi[1], i, :, lo:hi] = _vrow(u, seg - halo + i)
        cw = cw_ref[:, lo:hi]
        v = u * cw[CONV_WIDTH - 1:CONV_WIDTH]
        for k in range(1, CONV_WIDTH):
            shifted = jnp.concatenate(tops[halo - k:] + [u[:tm - k * SUBLANES]], axis=0)
            v = v + shifted * cw[CONV_WIDTH - 1 - k:CONV_WIDTH - k]
        xc[bi, hd] = v + cb_ref[:, lo:hi]

    def gate_proj(bi, hd):
        gates[bi, hd] = _dot(xc[bi, hd].astype(BF16), wg_ref[hd])

    def recur(bi, hd):
        lo, hi = blocks[hd]
        u, v, gt = uy[bi, hd], xc[bi, hd], gates[bi, hd]
        y = u * jax.nn.sigmoid(u * (GELU_C1 + GELU_C2 * (u * u)))
        r = jax.nn.sigmoid(gt[:, :LRU_BLOCK] + bg_ref[0:1, lo:hi])
        i_gate = jax.nn.sigmoid(gt[:, LRU_BLOCK:] + bg_ref[1:2, lo:hi])
        lam = lam_ref[:, lo:hi]
        softplus_neg = jnp.maximum(-lam, 0.0) + jnp.log1p(jnp.exp(-jnp.abs(lam)))
        neg_log_a = r * (LRU_C * softplus_neg)
        a = jnp.exp(-neg_log_a)
        one_minus_a2 = (1.0 + a * a) * jnp.tanh(neg_log_a)
        mult = jnp.where(one_minus_a2 == 0.0, 0.0, one_minus_a2 * lax.rsqrt(one_minus_a2))
        b = mult * (i_gate * v)
        loc = [_vrow(b, 0)]
        prod = [_vrow(a, 0)]
        for g in range(1, seg):
            ag = _vrow(a, g)
            loc.append(ag * loc[-1] + _vrow(b, g))
            prod.append(ag * prod[-1])
        h_in = h_scr[bi[1], 0:1, lo:hi]
        seg_end = _sublane_scan(prod[-1], loc[-1], h_in)
        seg_in = jnp.where(sub == 0, h_in, pltpu.roll(seg_end, 1, 0))
        h_scr[bi[1], 0:1, lo:hi] = seg_end[SUBLANES - 1:SUBLANES]
        hs = jnp.concatenate([loc[g] + prod[g] * seg_in for g in range(seg)], axis=0)
        hy[bi, hd] = (hs * y).astype(BF16)

    def out_proj(bi):
        hy_seg = jnp.concatenate([hy[bi, hd] for hd in range(len(blocks))], axis=1)
        hy_time = _dot(permt_ref[...], hy_seg).astype(BF16)
        o_ref[bi[1], rows(bi), :] = (x_ref[bi[1], rows(bi), :]
                                     + mod_ref[bi[1]][:, 2 * d:3 * d] * _dot(hy_time, wout_ref[...]))

    stages = (in_proj, conv, gate_proj, recur)
    normalize(keys[0])
    for slot in range(len(streams) + len(stages) - 1):
        for depth, stage in enumerate(stages):
            i = slot - depth
            if 0 <= i < len(streams):
                bi, hd = streams[i]
                nxt = keys.index(bi) + 1
                if stage is in_proj and hd == len(blocks) - 2 and nxt < len(keys):
                    normalize(keys[nxt])
                stage(bi, hd)
                if stage is recur and hd == len(blocks) - 1:
                    out_proj(bi)


def _rglru_layer(x, mod, gain, w_in, conv_w, conv_b, w_gates, b_gates, lam, w_out):
    b, s, d = x.shape
    w = lam.shape[-1]
    tm = TM_LRU
    perm, perm_t = _time_permutation(LRU_TILE)
    return pl.pallas_call(
        _rglru_kernel,
        grid=(s // tm,),
        in_specs=[pl.BlockSpec((b, tm, d), lambda t: (0, t, 0)),
                  _const_spec(mod.shape),
                  _const_spec((1, d)),
                  _const_spec(perm.shape),
                  _const_spec(perm_t.shape),
                  _const_spec(w_in.shape),
                  _const_spec(conv_w.shape),
                  _const_spec((1, w)),
                  _const_spec(w_gates.shape),
                  _const_spec(b_gates.shape),
                  _const_spec((1, w)),
                  _const_spec(w_out.shape)],
        out_specs=pl.BlockSpec((b, tm, d), lambda t: (0, t, 0)),
        out_shape=jax.ShapeDtypeStruct(x.shape, F32),
        scratch_shapes=[pltpu.VMEM((b, CONV_WIDTH - 1, SUBLANES, w), F32),
                        pltpu.VMEM((b, SUBLANES, w), F32),
                        pltpu.VMEM(w_in.shape, BF16),
                        pltpu.VMEM(w_gates.shape, BF16),
                        pltpu.VMEM(w_out.shape, BF16)],
        compiler_params=_cparams(1),
        name="rglru",
    )(x, mod, gain.reshape(1, d), perm, perm_t, w_in, conv_w, conv_b.reshape(1, w), w_gates, b_gates,
      lam.reshape(1, w), w_out)


def _ffn_kernel(*refs, has_attn, final):
    x_ref, mod_ref, g_ref, wg_ref, wu_ref, wd_ref = refs[:6]
    rest = list(refs[6:])
    att_ref = wo_ref = fn_ref = None
    if has_attn:
        att_ref, wo_ref = rest[:2]
        rest = rest[2:]
    if final:
        fn_ref = rest[0]
        rest = rest[1:]
    o_ref, = rest
    d = x_ref.shape[-1]
    ff = wg_ref.shape[-1]
    tm = x_ref.shape[1]

    xt = x_ref[0]
    mod = mod_ref[0]
    if has_attn:
        xt = xt + mod[:, 2 * d:3 * d] * _dot(att_ref[0], wo_ref[...])
    h = _norm_mod(xt, g_ref[...], mod[:, 3 * d:4 * d], mod[:, 4 * d:5 * d]).astype(BF16)
    acc = jnp.zeros((tm, d), F32)
    for c in range(ff // FF_CHUNK):
        lo = c * FF_CHUNK
        hi = lo + FF_CHUNK
        gate = _dot(h, wg_ref[0, :, lo:hi].astype(BF16))
        up = _dot(h, wu_ref[0, :, lo:hi].astype(BF16))
        act = (gate * jax.nn.sigmoid(gate)) * up
        acc = acc + _dot(act.astype(BF16), wd_ref[0, lo:hi, :].astype(BF16))
    y = xt + mod[:, 5 * d:6 * d] * acc
    if final:
        y = y * lax.rsqrt(jnp.mean(y * y, axis=-1, keepdims=True) + RMS_EPS) * fn_ref[...]
    o_ref[0] = y


def _layer_spec(shape, layer):
    nd = len(shape)
    return pl.BlockSpec((1,) + tuple(shape[1:]), lambda *_: (layer,) + (0,) * (nd - 1),
                        pipeline_mode=pl.Buffered(1))


def _ffn_layer(x, mod, gain, layer, w_gate, w_up, w_down, attn=None, w_o=None, final_gain=None):
    b, s, d = x.shape
    tm = TM_FFN
    has_attn = attn is not None
    final = final_gain is not None
    in_specs = [pl.BlockSpec((1, tm, d), lambda i, t: (i, t, 0)),
                pl.BlockSpec((1, 1, mod.shape[-1]), lambda i, t: (i, 0, 0)),
                _const_spec((1, d)),
                _layer_spec(w_gate.shape, layer),
                _layer_spec(w_up.shape, layer),
                _layer_spec(w_down.shape, layer)]
    args = [x, mod, gain.reshape(1, d), w_gate, w_up, w_down]
    if has_attn:
        in_specs += [pl.BlockSpec((1, tm, d), lambda i, t: (i, t, 0)), _const_spec(w_o.shape)]
        args += [attn, w_o]
    if final:
        in_specs += [_const_spec((1, d))]
        args += [final_gain.reshape(1, d)]
    return pl.pallas_call(
        functools.partial(_ffn_kernel, has_attn=has_attn, final=final),
        grid=(b, s // tm),
        in_specs=in_specs,
        out_specs=pl.BlockSpec((1, tm, d), lambda i, t: (i, t, 0)),
        out_shape=jax.ShapeDtypeStruct(x.shape, F32),
        compiler_params=_cparams(2),
        name="ffn_attn_final" if has_attn else "ffn",
    )(*args)


def _qkv_kernel(x_ref, xn_ref, kvmod_ref, mod_ref, kvg_ref, qg_ref, wk32_ref, wvt32_ref, wq32_ref,
                q_ref, k_ref, vt_ref, mask_ref, km_scr, hqa_scr, hkva_scr, hqb_scr, hkvb_scr,
                wk_ref, wvt_ref, wq_ref):
    nbatch = x_ref.shape[0]
    d = x_ref.shape[-1]
    tm = MOBA_BLOCK
    nb = mask_ref.shape[3]
    step = pl.program_id(0)
    buf_a = (hqa_scr, hkva_scr)
    buf_b = (hqb_scr, hkvb_scr)
    blk = lax.broadcasted_iota(jnp.int32, (nb, tm), 0).astype(F32)
    q_scale = HEAD_DIM ** -0.5 * LOG2E
    pair_w = KV_GROUP * HEAD_DIM

    def normalize(src_ref, row0, bufs):
        for bi in range(nbatch):
            xt = src_ref[bi, row0:row0 + tm, :]
            xn = xt * lax.rsqrt(jnp.mean(xt * xt, axis=-1, keepdims=True) + RMS_EPS)
            mod = mod_ref[bi]
            kvmod = kvmod_ref[bi]
            rows = pl.ds(bi * tm, tm)
            bufs[0][rows, :] = (xn * (qg_ref[...] * (1.0 + mod[:, d:2 * d])) + mod[:, 0:d]).astype(BF16)
            bufs[1][rows, :] = (xn * (kvg_ref[...] * (1.0 + kvmod[:, d:2 * d])) + kvmod[:, 0:d]).astype(BF16)

    def process(half, cur, nxt_src, nxt_row0, nxt):
        j = 2 * step + half
        past = blk < j.astype(F32)
        out_rows = slice(half * tm, (half + 1) * tm)
        q_pair = {}
        h_q = cur[0][...]
        h_kv = cur[1][...]
        normalize(nxt_src, nxt_row0, nxt)

        def q_proj(hk):
            qp = _dot(h_q, wq_ref[:, hk * pair_w:(hk + 1) * pair_w]) * q_scale
            for bi in range(nbatch):
                q_ref[bi, out_rows, hk * pair_w:(hk + 1) * pair_w] = qp[bi * tm:(bi + 1) * tm].astype(BF16)
            q_pair[hk] = qp

        def select_blocks(bi, hq):
            hk, sub = divmod(hq, KV_GROUP)
            q_h = q_pair[hk][bi * tm:(bi + 1) * tm, sub * HEAD_DIM:(sub + 1) * HEAD_DIM]
            km_h = km_scr[bi, :, hk * HEAD_DIM:(hk + 1) * HEAD_DIM]
            gate = _dot_nt(km_h.astype(BF16), q_h.astype(BF16))
            gate = jnp.where(past, gate, NEG_INF)
            mask = jnp.full((nb, tm), NEG_INF, F32)
            for _ in range(MOBA_TOPK):
                best = jnp.max(gate, axis=0, keepdims=True)
                first = jnp.min(jnp.where(gate == best, blk, nb), axis=0, keepdims=True)
                hit = blk == first
                mask = jnp.where(hit, 0.0, mask)
                gate = jnp.where(hit, -jnp.inf, gate)
            mask = jnp.where(past, mask, NEG_INF)
            mask_ref[bi, hk, half, :, sub * tm:(sub + 1) * tm] = mask

        def k_proj():
            k = _dot(h_kv, wk_ref[...])
            for bi in range(nbatch):
                kb = k[bi * tm:(bi + 1) * tm]
                for hk in range(N_KV_HEADS):
                    k_ref[bi, hk, half] = kb[:, hk * HEAD_DIM:(hk + 1) * HEAD_DIM].astype(BF16)
            return k

        def v_proj():
            vt = _dot_nt(wvt_ref[...], h_kv)
            for bi in range(nbatch):
                for hk in range(N_KV_HEADS):
                    vt_ref[bi, hk, half] = vt[hk * HEAD_DIM:(hk + 1) * HEAD_DIM,
                                              bi * tm:(bi + 1) * tm].astype(BF16)

        k = None
        q_proj(0)
        for hk in range(N_KV_HEADS):
            if hk + 1 < N_KV_HEADS:
                q_proj(hk + 1)
            if hk == N_KV_HEADS - 2:
                k = k_proj()
            if hk == N_KV_HEADS - 1:
                v_proj()
            for bi in range(nbatch):
                for sub in range(KV_GROUP):
                    select_blocks(bi, hk * KV_GROUP + sub)
        for bi in range(nbatch):
            km_scr[bi, pl.ds(j, 1), :] = jnp.mean(k[bi * tm:(bi + 1) * tm], axis=0, keepdims=True)

    @pl.when(step == 0)
    def _():
        km_scr[...] = jnp.zeros_like(km_scr)
        for c in range(0, wq_ref.shape[1], pair_w):
            wq_ref[:, c:c + pair_w] = wq32_ref[:, c:c + pair_w].astype(BF16)
        for c in range(0, wk_ref.shape[1], pair_w):
            wk_ref[:, c:c + pair_w] = wk32_ref[:, c:c + pair_w].astype(BF16)
        for c in range(0, wvt_ref.shape[1], pair_w):
            wvt_ref[:, c:c + pair_w] = wvt32_ref[:, c:c + pair_w].astype(BF16)
        normalize(x_ref, 0, buf_a)

    process(0, buf_a, x_ref, tm, buf_b)
    process(1, buf_b, xn_ref, 0, buf_a)


def _qkv_gate(x, kvmod, mod, kv_gain, q_gain, w_k, w_vt, w_q):
    b, s, d = x.shape
    tm = MOBA_BLOCK
    nb = s // tm
    dkv = N_KV_HEADS * HEAD_DIM
    dq = N_HEADS * HEAD_DIM
    out_shape = (jax.ShapeDtypeStruct((b, s, dq), BF16),
                 jax.ShapeDtypeStruct((b, N_KV_HEADS, nb, tm, HEAD_DIM), BF16),
                 jax.ShapeDtypeStruct((b, N_KV_HEADS, nb, HEAD_DIM, tm), BF16),
                 jax.ShapeDtypeStruct((b, N_KV_HEADS, nb, nb, KV_GROUP * tm), F32))
    out_specs = (pl.BlockSpec((b, 2 * tm, dq), lambda t: (0, t, 0)),
                 pl.BlockSpec((b, N_KV_HEADS, 2, tm, HEAD_DIM), lambda t: (0, 0, t, 0, 0)),
                 pl.BlockSpec((b, N_KV_HEADS, 2, HEAD_DIM, tm), lambda t: (0, 0, t, 0, 0)),
                 pl.BlockSpec((b, N_KV_HEADS, 2, nb, KV_GROUP * tm), lambda t: (0, 0, t, 0, 0)))
    return pl.pallas_call(
        _qkv_kernel,
        grid=(nb // 2,),
        in_specs=[pl.BlockSpec((b, 2 * tm, d), lambda t: (0, t, 0)),
                  pl.BlockSpec((b, tm, d), lambda t: (0, jnp.minimum(2 * t + 2, nb - 1), 0)),
                  _const_spec(kvmod.shape),
                  _const_spec(mod.shape),
                  _const_spec((1, d)),
                  _const_spec((1, d)),
                  _const_spec(w_k.shape),
                  _const_spec(w_vt.shape),
                  _const_spec(w_q.shape)],
        out_specs=out_specs,
        out_shape=out_shape,
        scratch_shapes=[pltpu.VMEM((b, nb, dkv), F32),
                        pltpu.VMEM((b * tm, d), BF16),
                        pltpu.VMEM((b * tm, d), BF16),
                        pltpu.VMEM((b * tm, d), BF16),
                        pltpu.VMEM((b * tm, d), BF16),
                        pltpu.VMEM(w_k.shape, BF16),
                        pltpu.VMEM(w_vt.shape, BF16),
                        pltpu.VMEM(w_q.shape, BF16)],
        compiler_params=_cparams(1),
        name="qkv_gate",
    )(x, x, kvmod, mod, kv_gain.reshape(1, d), q_gain.reshape(1, d), w_k, w_vt, w_q)


def _attn_kernel(rb_ref, q_ref, qn_ref, k_ref, vt_ref, mask_ref, maskn_ref, tab_ref, o_ref, m_scr, l_scr,
                 acc_scr, sa_scr, sb_scr, cma_scr, cmb_scr, rowa_scr, rowb_scr, phase_scr):
    g0 = pl.program_id(1) * ATTN_STREAMS
    j = pl.program_id(2)
    nb = k_ref.shape[2]
    tq = q_ref.shape[1]
    pair_w = KV_GROUP * HEAD_DIM
    streams = range(ATTN_STREAMS)

    def stack_heads(qref, si):
        lo = si * pair_w
        return jnp.concatenate([qref[0, :, lo:lo + HEAD_DIM], qref[0, :, lo + HEAD_DIM:lo + pair_w]], axis=0)

    qs = [stack_heads(q_ref, si) for si in streams]

    def scores(si, n, qx):
        kb = k_ref[0, si, n]
        return [_dot_nt(kb, qx[hq * tq:(hq + 1) * tq]) for hq in range(KV_GROUP)]

    def tile_bias(si, lo):
        return [tab_ref[KV_GROUP * si + hq, lo:lo + MOBA_BLOCK, :] for hq in range(KV_GROUP)]

    def mask_row(si, n, mref):
        return mref[0, si, 0, pl.ds(n, 1), :]

    def col_max(s):
        return jnp.max(s, axis=0, keepdims=True)

    def col_sum(s):
        return jnp.sum(s, axis=0, keepdims=True)

    lane = lax.broadcasted_iota(jnp.int32, (1, KV_GROUP * tq), 1)
    far_bias = [LOG2E * jnp.where(lane < tq, rb_ref[KV_GROUP * (g0 + si), REL_BUCKETS - 1],
                                  rb_ref[KV_GROUP * (g0 + si) + 1, REL_BUCKETS - 1]) for si in streams]

    def far_row(si, n, valid, mref):
        return jnp.where(valid, mask_row(si, n, mref) + far_bias[si], NEG_INF)

    n_far = jnp.maximum(j - 1 - NEAR_FAR, 0)
    n_groups = 1 + (n_far + GROUP - 1) // GROUP
    jp = jnp.maximum(j - 1, 0)
    head_blocks = [j, jp] + [jnp.maximum(j - 2 - u, 0) for u in range(NEAR_FAR)]

    def group_blocks(k):
        return [jnp.where(k == 0, head_blocks[u], jnp.minimum((k - 1) * GROUP + u, nb - 1))
                for u in range(GROUP)]

    def put(buf, si, u, halves, row):
        s_buf, cm_buf, row_buf = buf
        for hq, s in enumerate(halves):
            cols = slice(hq * tq, (hq + 1) * tq)
            s_buf[si, u, :, cols] = s
            cm_buf[si, u:u + 1, cols] = col_max(s)
        row_buf[si, u:u + 1] = row

    def head_slot(si, u, jq, qx, mref):
        if u == 0:
            return ([s + t for s, t in zip(scores(si, jq, qx), tile_bias(si, MOBA_BLOCK))],
                    jnp.zeros((1, KV_GROUP * tq), F32))
        if u == 1:
            jqp = jnp.maximum(jq - 1, 0)
            return ([s + t for s, t in zip(scores(si, jqp, qx), tile_bias(si, 0))],
                    jnp.where(jq >= 1, mask_row(si, jqp, mref), NEG_INF))
        n = jnp.maximum(jq - u, 0)
        return scores(si, n, qx), far_row(si, n, jq - u >= 0, mref)

    j_next = jnp.minimum(j + 1, nb - 1)

    def step(si, k, buf, nxt, last):
        s_buf, cm_buf, row_buf = buf
        blocks = group_blocks(k)
        m_old = m_scr[si, 0:1]
        rows = [row_buf[si, u:u + 1] for u in range(GROUP)]
        m_new = m_old
        for u in range(GROUP):
            m_new = jnp.maximum(m_new, cm_buf[si, u:u + 1] + rows[u])
        alpha = jnp.exp2(m_old - m_new)
        l_new = alpha * l_scr[si, 0:1]
        pv = None
        qs_next = stack_heads(qn_ref, si) if last else None

        def stage1(u):
            if last:
                put(nxt, si, u, *head_slot(si, u, j_next, qs_next, maskn_ref))
            else:
                n = k * GROUP + u
                nc = jnp.minimum(n, nb - 1)
                put(nxt, si, u, scores(si, nc, qs[si]), far_row(si, nc, n < n_far, mask_ref))

        for u in range(GROUP):
            stage1(u)
            shift = rows[u] - m_new
            halves = []
            for hq in range(KV_GROUP):
                cols = slice(hq * tq, (hq + 1) * tq)
                p = jnp.exp2(s_buf[si, u, :, cols] + shift[:, cols])
                halves.append((col_sum(p), _dot(vt_ref[0, si, blocks[u]], p.astype(BF16))))
            l_new = l_new + jnp.concatenate([h[0] for h in halves], axis=1)
            d = jnp.concatenate([h[1] for h in halves], axis=1)
            pv = d if pv is None else pv + d
        acc_scr[si] = alpha * acc_scr[si] + pv
        l_scr[si, 0:1] = l_new
        m_scr[si, 0:1] = m_new

    buf_a = (sa_scr, cma_scr, rowa_scr)
    buf_b = (sb_scr, cmb_scr, rowb_scr)
    m_scr[...] = jnp.full_like(m_scr, NEG_INF)
    l_scr[...] = jnp.zeros_like(l_scr)
    acc_scr[...] = jnp.zeros_like(acc_scr)

    @pl.when(j == 0)
    def _():
        for si in streams:
            for u in range(GROUP):
                put(buf_a, si, u, *head_slot(si, u, j, qs[si], mask_ref))
        phase_scr[0] = 0

    phase = phase_scr[0]

    def run(k, buf, nxt):
        @pl.when((k >= 0) & (k + 1 < n_groups))
        def _():
            for si in streams:
                step(si, k, buf, nxt, last=False)

        @pl.when((k >= 0) & (k + 1 == n_groups))
        def _():
            for si in streams:
                step(si, k, buf, nxt, last=True)

    def pair(t, carry):
        run(2 * t - phase, buf_a, buf_b)
        run(2 * t + 1 - phase, buf_b, buf_a)
        return carry

    lax.fori_loop(0, (n_groups + phase + 1) // 2, pair, 0)
    phase_scr[0] = (n_groups + phase) % 2

    for si in streams:
        o = acc_scr[si] * (1.0 / l_scr[si, 0:1])
        o_ref[0, :, si * pair_w:(si + 1) * pair_w] = jnp.concatenate(
            [o[:, :tq].T, o[:, tq:].T], axis=1).astype(BF16)


def _moba_attn(rel_bias, q, k, vt, mask, table):
    b, s, dq = q.shape
    nb = k.shape[2]
    tq = MOBA_BLOCK
    ns = ATTN_STREAMS
    gw = ns * KV_GROUP * HEAD_DIM
    lanes = KV_GROUP * tq
    return pl.pallas_call(
        _attn_kernel,
        grid=(b, N_KV_HEADS // ns, nb),
        in_specs=[pl.BlockSpec(memory_space=pltpu.SMEM),
                  pl.BlockSpec((1, tq, gw), lambda i, g, j: (i, j, g)),
                  pl.BlockSpec((1, tq, gw), lambda i, g, j: (i, jnp.minimum(j + 1, nb - 1), g)),
                  pl.BlockSpec((1, ns, nb, tq, HEAD_DIM), lambda i, g, j: (i, g, 0, 0, 0),
                               pipeline_mode=pl.Buffered(1)),
                  pl.BlockSpec((1, ns, nb, HEAD_DIM, tq), lambda i, g, j: (i, g, 0, 0, 0),
                               pipeline_mode=pl.Buffered(1)),
                  pl.BlockSpec((1, ns, 1, nb, lanes), lambda i, g, j: (i, g, j, 0, 0)),
                  pl.BlockSpec((1, ns, 1, nb, lanes),
                               lambda i, g, j: (i, g, jnp.minimum(j + 1, nb - 1), 0, 0)),
                  pl.BlockSpec((ns * KV_GROUP, 2 * MOBA_BLOCK, MOBA_BLOCK), lambda i, g, j: (g, 0, 0))],
        out_specs=pl.BlockSpec((1, tq, gw), lambda i, g, j: (i, j, g)),
        out_shape=jax.ShapeDtypeStruct((b, s, dq), BF16),
        scratch_shapes=[pltpu.VMEM((ns, SUBLANES, lanes), F32),
                        pltpu.VMEM((ns, SUBLANES, lanes), F32),
                        pltpu.VMEM((ns, HEAD_DIM, lanes), F32),
                        pltpu.VMEM((ns, GROUP, MOBA_BLOCK, lanes), F32),
                        pltpu.VMEM((ns, GROUP, MOBA_BLOCK, lanes), F32),
                        pltpu.VMEM((ns, SUBLANES, lanes), F32),
                        pltpu.VMEM((ns, SUBLANES, lanes), F32),
                        pltpu.VMEM((ns, SUBLANES, lanes), F32),
                        pltpu.VMEM((ns, SUBLANES, lanes), F32),
                        pltpu.SMEM((1,), jnp.int32)],
        compiler_params=_cparams(3),
        name="moba_attn",
    )(rel_bias.astype(F32), q, q, k, vt, mask, mask, table)


def kernel(x, c, mod_w, mod_b, norm_mix, norm_ffn, lru_w_in, lru_conv_w, lru_conv_b, lru_w_gates,
           lru_b_gates, lru_lambda, lru_w_out, kv_mod_w, kv_mod_b, kv_norm, w_kv, attn_w_q, attn_w_o,
           rel_bias, ffn_w_gate, ffn_w_up, ffn_w_down, final_norm):
    b, s, d = x.shape
    assert s % TM_FFN == 0 and s % TM_LRU == 0 and s % (2 * MOBA_BLOCK) == 0 and b <= SUBLANES
    dkv = N_KV_HEADS * HEAD_DIM

    c_pad = jnp.zeros((SUBLANES, d), F32).at[:b].set(c.astype(F32))
    mod = _adaln_mod(c_pad, b, mod_w, mod_b)[:, :b]
    kvmod = _adaln_mod(c_pad, b, kv_mod_w[None], kv_mod_b[None])[0, :b]
    mod0 = mod[0][:, None, :]
    mod1 = mod[1][:, None, :]
    kvmod = kvmod[:, None, :]

    x = _rglru_layer(x, mod0, norm_mix[0], lru_w_in[0], lru_conv_w[0], lru_conv_b[0],
                     lru_w_gates[0], lru_b_gates[0], lru_lambda[0], lru_w_out[0])
    ffn_w = (ffn_w_gate, ffn_w_up, ffn_w_down)
    x = _ffn_layer(x, mod0, norm_ffn[0], 0, *ffn_w)

    q, k, vt, mask = _qkv_gate(x, kvmod, mod1, kv_norm, norm_mix[1], w_kv[:, :dkv], w_kv[:, dkv:].T,
                               attn_w_q[0])
    table = _t5_table(rel_bias)
    attn = _moba_attn(rel_bias, q, k, vt, mask, table)
    return _ffn_layer(x, mod1, norm_ffn[1], 1, *ffn_w, attn=attn, w_o=attn_w_o[0].astype(BF16),
                      final_gain=final_norm)
```
